```python
import math
import jax
import jax.numpy as jnp
from jax import lax
import numpy as np

D_MODEL = 2048
BATCH = 8
SEQ = 2048
DEPTH = 2
DEC_BATCH = 32
DEC_SEQ = 64
PAST_LEN = 1024

CHUNK = 64
N_META = 16
ROPE_THETA = 10000.0
EPS = 1e-6
Q_BLOCK = 128

N_AB = (DEPTH + 1) // 2
N_CD = DEPTH // 2

A_HEADS = 8
A_KV_HEADS = 2
A_HEAD_DIM = D_MODEL // 2 // A_HEADS
IDX_HEADS = 16
IDX_DIM = 64
TOPK_MAX = 256
B_HEADS = 8
B_Q_RANK = D_MODEL // 4
B_KV_RANK = D_MODEL // 8
B_NOPE = 128
B_ROPE = 64
B_V = D_MODEL // 2 // B_HEADS
C_WIDTH = D_MODEL // 2
C_BLOCKS = 8
C_BLOCK = C_WIDTH // C_BLOCKS
CONV_W = 4
RG_C = 8.0
D_HEADS = 4
D_KDIM = D_MODEL // 2 // D_HEADS
D_VDIM = 256
D_FF = 4 * D_MODEL

AB_SPLITS = [A_HEADS * A_HEAD_DIM, A_KV_HEADS * A_HEAD_DIM, A_KV_HEADS * A_HEAD_DIM,
             IDX_HEADS * IDX_DIM, IDX_DIM, IDX_HEADS, B_Q_RANK, B_KV_RANK, B_ROPE]
AB_IN = sum(AB_SPLITS)
AB_OUT = A_HEADS * A_HEAD_DIM + B_HEADS * B_V
CD_SPLITS = [C_WIDTH, C_WIDTH, D_HEADS * D_KDIM, D_HEADS * D_KDIM, D_HEADS * D_VDIM, D_HEADS * D_VDIM]
CD_IN = sum(CD_SPLITS)
CD_OUT = C_WIDTH + D_HEADS * D_VDIM

kernel_name = 'hybrid_streaming_dsa_mla_rglru_retention_step'

F32 = jnp.float32


def rmsnorm(x, g):
    x32 = x.astype(F32)
    y = x32 * lax.rsqrt(jnp.mean(x32 * x32, axis=-1, keepdims=True) + EPS)
    return (y * g.astype(F32)).astype(x.dtype)


def split_cols(z, sizes):
    return jnp.split(z, [int(c) for c in np.cumsum(sizes)[:-1]], axis=-1)


def rope(x, pos):
    d = x.shape[-1]
    inv = ROPE_THETA ** (-jnp.arange(0, d, 2, dtype=F32) / d)
    ang = pos.astype(F32)[:, None] * inv[None, :]
    shape = (pos.shape[0],) + (1,) * (x.ndim - 3) + (d // 2,)
    cos = jnp.cos(ang).reshape(shape)
    sin = jnp.sin(ang).reshape(shape)
    x32 = x.astype(F32)
    x1, x2 = x32[..., : d // 2], x32[..., d // 2:]
    return jnp.concatenate([x1 * cos - x2 * sin, x1 * sin + x2 * cos], axis=-1).astype(x.dtype)


def sweep_query_blocks(fn, q_arrays, cq):
    nb_batch, t = q_arrays[0].shape[:2]
    nblk = -(-t // Q_BLOCK)
    pad = nblk * Q_BLOCK - t
    padded = [jnp.pad(a, [(0, 0), (0, pad)] + [(0, 0)] * (a.ndim - 2)) for a in q_arrays]
    cqp = jnp.pad(cq, (0, pad), mode='edge')

    def body(j):
        start = j * Q_BLOCK
        blk = [lax.dynamic_slice_in_dim(a, start, Q_BLOCK, axis=1) for a in padded]
        return fn(*blk, lax.dynamic_slice_in_dim(cqp, start, Q_BLOCK))

    out = lax.map(body, jnp.arange(nblk))
    out = jnp.moveaxis(out, 0, 1).reshape(nb_batch, nblk * Q_BLOCK, out.shape[-1])
    return out[:, :t]


def dsa_attend(q, qi, wi, cq, k, v, ki, ck, topk):
    s_idx = jnp.einsum('bqhd,bsd->bqhs', qi.astype(F32), ki.astype(F32))
    s_idx = jnp.einsum('bqhs,bqh->bqs', jax.nn.relu(s_idx), wi.astype(F32))
    admissible = ck[None, :] <= cq[:, None]
    s_idx = jnp.where(admissible[None], s_idx, -jnp.inf)
    _, sel = lax.top_k(s_idx, topk)
    valid = ck[sel] <= cq[None, :, None]
    gather = jax.vmap(lambda rows, ids: rows[ids])
    k_sel = gather(k, sel).astype(F32)
    v_sel = gather(v, sel).astype(F32)
    nb, nq = q.shape[:2]
    qg = q.reshape(nb, nq, A_KV_HEADS, A_HEADS // A_KV_HEADS, A_HEAD_DIM).astype(F32)
    s = jnp.einsum('bqcgd,bqscd->bqcgs', qg, k_sel) * (A_HEAD_DIM ** -0.5)
    s = jnp.where(valid[:, :, None, None, :], s, -jnp.inf)
    p = jax.nn.softmax(s, axis=-1)
    o = jnp.einsum('bqcgs,bqscd->bqcgd', p, v_sel)
    return o.reshape(nb, nq, A_HEADS * A_HEAD_DIM).astype(q.dtype)


def mla_attend(qn, qr, cq, kn, kr, v, ck):
    s = (jnp.einsum('bqhd,bshd->bhqs', qn.astype(F32), kn.astype(F32))
         + jnp.einsum('bqhr,bsr->bhqs', qr.astype(F32), kr.astype(F32))) * ((B_NOPE + B_ROPE) ** -0.5)
    mask = ck[None, :] <= cq[:, None]
    s = jnp.where(mask[None, None], s, -jnp.inf)
    p = jax.nn.softmax(s, axis=-1)
    o = jnp.einsum('bhqs,bshd->bqhd', p, v.astype(F32))
    nb, nq = qn.shape[:2]
    return o.reshape(nb, nq, B_HEADS * B_V).astype(qn.dtype)


def ab_mix(u, pos, cq, ck, past, topk, blocked, w_in, w_out, g_aq, g_ak, g_qlat, w_uq,
           g_kvlat, w_ukv, g_qn, g_qr, g_kn, g_kr):
    nb, t, _ = u.shape
    qa, ka, va, qi, ki, wi, cqlat, ckv, kr = split_cols(u @ w_in, AB_SPLITS)
    q = rope(rmsnorm(qa.reshape(nb, t, A_HEADS, A_HEAD_DIM), g_aq), pos)
    k_new = rope(rmsnorm(ka.reshape(nb, t, A_KV_HEADS, A_HEAD_DIM), g_ak), pos)
    v_new = va.reshape(nb, t, A_KV_HEADS, A_HEAD_DIM)
    qi = rope(qi.reshape(nb, t, IDX_HEADS, IDX_DIM), pos) * (IDX_DIM ** -0.5)
    ki_new = rope(ki, pos)
    wi = wi * (IDX_HEADS ** -0.5)
    qb = (rmsnorm(cqlat, g_qlat) @ w_uq).reshape(nb, t, B_HEADS, B_NOPE + B_ROPE)
    qn = rmsnorm(qb[..., :B_NOPE], g_qn)
    qr = rope(rmsnorm(qb[..., B_NOPE:], g_qr), pos)
    lat_new = rmsnorm(ckv, g_kvlat)
    kr_new = rope(rmsnorm(kr, g_kr), pos)
    new_rows = (k_new, v_new, ki_new, lat_new, kr_new)
    if past is None:
        k_all, v_all, ki_all, lat_all, kr_all = new_rows
    else:
        k_all, v_all, ki_all, lat_all, kr_all = [jnp.concatenate([p_.astype(n_.dtype), n_], axis=1)
                                                 for p_, n_ in zip(past, new_rows)]
    kvb = (lat_all @ w_ukv).reshape(nb, lat_all.shape[1], B_HEADS, B_NOPE + B_V)
    kn_all = rmsnorm(kvb[..., :B_NOPE], g_kn)
    vb_all = kvb[..., B_NOPE:]

    def attn_a(q_, qi_, wi_, c_):
        return dsa_attend(q_, qi_, wi_, c_, k_all, v_all, ki_all, ck, topk)

    def attn_b(qn_, qr_, c_):
        return mla_attend(qn_, qr_, c_, kn_all, kr_all, vb_all, ck)

    if blocked:
        oa = sweep_query_blocks(attn_a, (q, qi, wi), cq)
        ob = sweep_query_blocks(attn_b, (qn, qr), cq)
    else:
        oa = attn_a(q, qi, wi, cq)
        ob = attn_b(qn, qr, cq)
    return jnp.concatenate([oa, ob], axis=-1) @ w_out, new_rows


def rglru_scan(a, b, h0):
    b = b.at[:, 0].add(a[:, 0] * h0)

    def combine(l, r):
        return (l[0] * r[0], r[0] * l[1] + r[1])

    _, h = lax.associative_scan(combine, (a, b), axis=1)
    return h, h[:, -1]


def retention_block(q, k, v, s, log_gamma):
    n = q.shape[1]
    i = jnp.arange(n, dtype=F32)
    diff = i[:, None] - i[None, :]
    decay = jnp.where(diff >= 0, jnp.exp(log_gamma[:, None, None] * jnp.maximum(diff, 0.0)), 0.0)
    o = jnp.einsum('bihd,bjhd->bhij', q, k) * decay[None]
    o = jnp.einsum('bhij,bjhe->bihe', o, v)
    inter = jnp.exp(log_gamma[None, :] * (i[:, None] + 1.0))
    o = o + jnp.einsum('bihd,bhde->bihe', q, s) * inter[None, :, :, None]
    kdec = jnp.exp(log_gamma[None, :] * (n - 1.0 - i)[:, None])
    s = (jnp.exp(log_gamma * n)[None, :, None, None] * s
         + jnp.einsum('bjhd,bjhe->bhde', k * kdec[None, :, :, None], v))
    return o, s


def retention(q, k, v, s0, log_gamma):
    t = q.shape[1]
    lead = t % CHUNK
    s = s0
    outs = []
    if lead:
        o0, s = retention_block(q[:, :lead], k[:, :lead], v[:, :lead], s, log_gamma)
        outs.append(o0)
    nc = (t - lead) // CHUNK
    if nc:
        def to_chunks(a):
            return jnp.moveaxis(a[:, lead:].reshape(a.shape[0], nc, CHUNK, *a.shape[2:]), 1, 0)

        def step(s_, blk):
            o_, s_ = retention_block(blk[0], blk[1], blk[2], s_, log_gamma)
            return s_, o_

        s, oc = lax.scan(step, s, (to_chunks(q), to_chunks(k), to_chunks(v)))
        outs.append(jnp.moveaxis(oc, 0, 1).reshape(q.shape[0], t - lead, *oc.shape[3:]))
    return jnp.concatenate(outs, axis=1), s


def cd_mix(u, pos, conv_buf, h0, s0, log_gamma, w_in, w_out, conv_w, conv_b, ga_w, ga_b,
           gx_w, gx_b, lam, g_dn):
    nb, t, _ = u.shape
    cx, cg, dq, dk, dv, dg = split_cols(u @ w_in, CD_SPLITS)
    full = jnp.concatenate([conv_buf.astype(cx.dtype), cx], axis=1)
    xc = conv_b + full[:, :t] * conv_w[0]
    for j in range(1, CONV_W):
        xc = xc + full[:, j:j + t] * conv_w[j]
    new_buf = full[:, t:]
    xc32 = xc.astype(F32)
    xb = xc32.reshape(nb, t, C_BLOCKS, C_BLOCK)
    r = jax.nn.sigmoid(jnp.einsum('btni,nij->btnj', xb, ga_w.astype(F32)).reshape(nb, t, C_WIDTH) + ga_b)
    ig = jax.nn.sigmoid(jnp.einsum('btni,nij->btnj', xb, gx_w.astype(F32)).reshape(nb, t, C_WIDTH) + gx_b)
    log_a = -RG_C * r * jax.nn.softplus(-lam.astype(F32))
    a = jnp.exp(log_a)
    b = jnp.sqrt(-jnp.expm1(2.0 * log_a)) * (ig * xc32)
    hc, h_last = rglru_scan(a, b, h0.astype(F32))
    out_c = (hc * jax.nn.gelu(cg.astype(F32))).astype(u.dtype)
    q = rope(dq.reshape(nb, t, D_HEADS, D_KDIM), pos).astype(F32)
    k = rope(dk.reshape(nb, t, D_HEADS, D_KDIM), pos).astype(F32) * (D_KDIM ** -0.5)
    v = dv.reshape(nb, t, D_HEADS, D_VDIM).astype(F32)
    od, s_last = retention(q, k, v, s0.astype(F32), log_gamma)
    od = rmsnorm(od.astype(u.dtype), g_dn).reshape(nb, t, D_HEADS * D_VDIM) * jax.nn.silu(dg)
    out = jnp.concatenate([out_c, od.astype(u.dtype)], axis=-1) @ w_out
    return out, (new_buf, h_last.astype(u.dtype), s_last.astype(u.dtype))


def sq_relu_mlp(x, g, w_up, w_down):
    return jnp.square(jax.nn.relu(rmsnorm(x, g) @ w_up)) @ w_down


def setup_inputs(seed: int = 0) -> dict:
    key = jax.random.key(seed)
    keys = iter(jax.random.split(key, 48))

    def nrm(shape, scale):
        return jax.random.normal(next(keys), shape, jnp.float32) * scale

    def gain(shape):
        return 1.0 + nrm(shape, 0.02)

    a0 = jax.random.uniform(next(keys), (N_CD, C_WIDTH), jnp.float32, 0.9, 0.999)
    p0 = a0 ** (1.0 / RG_C)
    c_lambda = jnp.log(p0) - jnp.log1p(-p0)
    return {
        'x_prompt': nrm((BATCH, SEQ, D_MODEL), 1.0),
        'x_sample': nrm((DEC_BATCH, DEC_SEQ, D_MODEL), 1.0),
        'cache_a_k': nrm((N_AB, DEC_BATCH, PAST_LEN, A_KV_HEADS, A_HEAD_DIM), 1.0),
        'cache_a_v': nrm((N_AB, DEC_BATCH, PAST_LEN, A_KV_HEADS, A_HEAD_DIM), 1.0),
        'cache_a_kidx': nrm((N_AB, DEC_BATCH, PAST_LEN, IDX_DIM), 1.0),
        'cache_b_latent': nrm((N_AB, DEC_BATCH, PAST_LEN, B_KV_RANK), 1.0),
        'cache_b_krope': nrm((N_AB, DEC_BATCH, PAST_LEN, B_ROPE), 1.0),
        'state_c_conv': nrm((N_CD, DEC_BATCH, CONV_W - 1, C_WIDTH), 1.0),
        'state_c_h': nrm((N_CD, DEC_BATCH, C_WIDTH), 0.5),
        'state_d_s': nrm((N_CD, DEC_BATCH, D_HEADS, D_KDIM, D_VDIM), 0.5),
        'meta_tokens': nrm((N_META, D_MODEL), 1.0),
        'norm_mix': gain((DEPTH, D_MODEL)),
        'norm_mlp': gain((DEPTH, D_MODEL)),
        'ab_w_in': nrm((N_AB, D_MODEL, AB_IN), D_MODEL ** -0.5),
        'ab_w_out': nrm((N_AB, AB_OUT, D_MODEL), AB_OUT ** -0.5),
        'a_q_norm': gain((N_AB, A_HEAD_DIM)),
        'a_k_norm': gain((N_AB, A_HEAD_DIM)),
        'b_q_lat_norm': gain((N_AB, B_Q_RANK)),
        'b_w_uq': nrm((N_AB, B_Q_RANK, B_HEADS * (B_NOPE + B_ROPE)), B_Q_RANK ** -0.5),
        'b_kv_lat_norm': gain((N_AB, B_KV_RANK)),
        'b_w_ukv': nrm((N_AB, B_KV_RANK, B_HEADS * (B_NOPE + B_V)), B_KV_RANK ** -0.5),
        'b_qn_norm': gain((N_AB, B_NOPE)),
        'b_qr_norm': gain((N_AB, B_ROPE)),
        'b_kn_norm': gain((N_AB, B_NOPE)),
        'b_kr_norm': gain((N_AB, B_ROPE)),
        'cd_w_in': nrm((N_CD, D_MODEL, CD_IN), D_MODEL ** -0.5),
        'cd_w_out': nrm((N_CD, CD_OUT, D_MODEL), CD_OUT ** -0.5),
        'c_conv_w': nrm((N_CD, CONV_W, C_WIDTH), 0.5),
        'c_conv_b': nrm((N_CD, C_WIDTH), 0.02),
        'c_gate_a_w': nrm((N_CD, C_BLOCKS, C_BLOCK, C_BLOCK), C_BLOCK ** -0.5),
        'c_gate_a_b': nrm((N_CD, C_WIDTH), 0.02),
        'c_gate_x_w': nrm((N_CD, C_BLOCKS, C_BLOCK, C_BLOCK), C_BLOCK ** -0.5),
        'c_gate_x_b': nrm((N_CD, C_WIDTH), 0.02),
        'c_lambda': c_lambda,
        'd_out_norm': gain((N_CD, D_VDIM)),
        'mlp_w_up': nrm((DEPTH, D_MODEL, D_FF), D_MODEL ** -0.5),
        'mlp_w_down': nrm((DEPTH, D_FF, D_MODEL), D_FF ** -0.5),
    }


def reference(x_prompt, x_sample, cache_a_k, cache_a_v, cache_a_kidx, cache_b_latent, cache_b_krope,
              state_c_conv, state_c_h, state_d_s, meta_tokens, norm_mix, norm_mlp, ab_w_in, ab_w_out,
              a_q_norm, a_k_norm, b_q_lat_norm, b_w_uq, b_kv_lat_norm, b_w_ukv, b_qn_norm, b_qr_norm,
              b_kn_norm, b_kr_norm, cd_w_in, cd_w_out, c_conv_w, c_conv_b, c_gate_a_w, c_gate_a_b,
              c_gate_x_w, c_gate_x_b, c_lambda, d_out_norm, mlp_w_up, mlp_w_down):
    nb_p, seq_p = x_prompt.shape[:2]
    seq_s = x_sample.shape[1]
    past = cache_a_k.shape[2]
    hp = jnp.concatenate([jnp.broadcast_to(meta_tokens[None].astype(x_prompt.dtype), (nb_p, N_META, D_MODEL)),
                          x_prompt], axis=1)
    pos_p = jnp.arange(N_META + seq_p)
    ck_p = jnp.concatenate([jnp.zeros((N_META,), jnp.int32), 1 + jnp.arange(seq_p, dtype=jnp.int32) // CHUNK])
    pos_s_all = jnp.arange(past + seq_s)
    ck_s = pos_s_all // CHUNK
    pos_s = pos_s_all[past:]
    cq_s = ck_s[past:]
    topk_p = min(TOPK_MAX, seq_p // 4)
    topk_s = min(TOPK_MAX, (past + seq_s) // 4)
    log_gamma = jnp.log(1.0 - 2.0 ** (-5.0 - jnp.arange(D_HEADS, dtype=F32)))

    hs = x_sample
    ab_p, ab_s, cd_p, cd_s = [], [], [], []
    for layer in range(DEPTH):
        i = layer // 2
        g_mix = norm_mix[layer]
        if layer % 2 == 0:
            w = (ab_w_in[i], ab_w_out[i], a_q_norm[i], a_k_norm[i], b_q_lat_norm[i], b_w_uq[i],
                 b_kv_lat_norm[i], b_w_ukv[i], b_qn_norm[i], b_qr_norm[i], b_kn_norm[i], b_kr_norm[i])
            yp, rows_p = ab_mix(rmsnorm(hp, g_mix), pos_p, ck_p, ck_p, None, topk_p, True, *w)
            past_rows = (cache_a_k[i], cache_a_v[i], cache_a_kidx[i], cache_b_latent[i], cache_b_krope[i])
            ys, rows_s = ab_mix(rmsnorm(hs, g_mix), pos_s, cq_s, ck_s, past_rows, topk_s, False, *w)
            ab_p.append(rows_p)
            ab_s.append(rows_s)
        else:
            w = (cd_w_in[i], cd_w_out[i], c_conv_w[i], c_conv_b[i], c_gate_a_w[i], c_gate_a_b[i],
                 c_gate_x_w[i], c_gate_x_b[i], c_lambda[i], d_out_norm[i])
            zc = jnp.zeros((nb_p, CONV_W - 1, C_WIDTH), hp.dtype)
            zh = jnp.zeros((nb_p, C_WIDTH), hp.dtype)
            zs = jnp.zeros((nb_p, D_HEADS, D_KDIM, D_VDIM), hp.dtype)
            yp, st_p = cd_mix(rmsnorm(hp, g_mix), pos_p, zc, zh, zs, log_gamma, *w)
            ys, st_s = cd_mix(rmsnorm(hs, g_mix), pos_s, state_c_conv[i], state_c_h[i], state_d_s[i],
                              log_gamma, *w)
            cd_p.append(st_p)
            cd_s.append(st_s)
        hp = hp + yp
        hs = hs + ys
        hp = hp + sq_relu_mlp(hp, norm_mlp[layer], mlp_w_up[layer], mlp_w_down[layer])
        hs = hs + sq_relu_mlp(hs, norm_mlp[layer], mlp_w_up[layer], mlp_w_down[layer])

    def stack(entries, j):
        return jnp.stack([e[j] for e in entries])

    return (hp[:, N_META:], hs,
            stack(ab_p, 0), stack(ab_p, 1), stack(ab_p, 2), stack(ab_p, 3), stack(ab_p, 4),
            stack(cd_p, 0), stack(cd_p, 1), stack(cd_p, 2),
            stack(ab_s, 0), stack(ab_s, 1), stack(ab_s, 2), stack(ab_s, 3), stack(ab_s, 4),
            stack(cd_s, 0), stack(cd_s, 1), stack(cd_s, 2))
```

```python
import functools

import jax
import jax.numpy as jnp
import numpy as np
from jax import lax
from jax.experimental import pallas as pl
from jax.experimental.pallas import tpu as pltpu

F32 = jnp.float32
BF16 = jnp.bfloat16

CHUNK = 64
ROPE_THETA = 10000.0
EPS = 1e-6
A_HEADS, A_KV_HEADS, A_HEAD_DIM = 8, 2, 128
IDX_HEADS, IDX_DIM = 16, 64
TOPK_MAX = 256
B_HEADS, B_NOPE, B_ROPE, B_V = 8, 128, 64, 128
C_BLOCKS, C_BLOCK, CONV_W = 8, 128, 4
RG_C = 8.0
D_HEADS, D_KDIM, D_VDIM = 4, 256, 256

LANES = 128
ROW_BLOCK = 128
NEG = -1e30
INT_MIN = -2 ** 31
VMEM_LIMIT = 56 * 1024 * 1024


def _cparams(*sem):
    return pltpu.CompilerParams(dimension_semantics=sem, vmem_limit_bytes=VMEM_LIMIT)


def _const_spec(shape):
    nd = len(shape)
    return pl.BlockSpec(shape, lambda *_: (0,) * nd)


def _rms(x, g):
    ms = jnp.mean(x * x, axis=-1, keepdims=True)
    return x * lax.rsqrt(ms + EPS) * g


def _rms_half(x, g):
    lane = lax.broadcasted_iota(jnp.int32, x.shape, 1)
    lo = lane < 64
    xx = x * x
    s_lo = jnp.sum(jnp.where(lo, xx, 0.0), axis=-1, keepdims=True)
    s_hi = jnp.sum(jnp.where(lo, 0.0, xx), axis=-1, keepdims=True)
    ms = jnp.where(lo, s_lo, s_hi) * (1.0 / 64.0)
    return x * lax.rsqrt(ms + EPS) * g


def _rope128(x, cos, sin_signed):
    return x * cos + pltpu.roll(x, 64, 1) * sin_signed


def _rope64(x, cos, sin_signed):
    lane = lax.broadcasted_iota(jnp.int32, x.shape, 1)
    first = (lane % 64) < 32
    rot = jnp.where(first, pltpu.roll(x, 96, 1), pltpu.roll(x, 32, 1))
    return x * cos + rot * sin_signed


AB_COLS = dict(qa=(0, 1024), ka=(1024, 1280), va=(1280, 1536), qi=(1536, 2560), cq=(2560, 3072),
               ckv=(3072, 3328), kikr=(3328, 3456), wi=(3456, 3584))
AB_IN_PAD = 3584


def _ab_proj_kernel(h_ref, gmix_ref, win_ref, wuq_ref, gaq_ref, gak_ref, gqlat_ref, gkvlat_ref, gqn_ref,
                    gqr_ref, gkr_ref, c128_ref, s128_ref, c64_ref, s64_ref,
                    q_ref, kf_ref, vf_ref, kb_ref, vb_ref, qi_ref, kikr_ref, kikrb_ref, wi_ref, qn_ref, qr_ref,
                    lat_ref):
    xn = _rms(h_ref[...], gmix_ref[...]).astype(BF16)
    c128, s128, c64, s64 = c128_ref[...], s128_ref[...], c64_ref[...], s64_ref[...]

    def proj(name):
        a, b = AB_COLS[name]
        return jnp.dot(xn, win_ref[:, a:b], preferred_element_type=F32)

    z = proj('qa')
    qscale = A_HEAD_DIM ** -0.5
    for h in range(A_HEADS):
        x = _rope128(_rms(z[:, h * 128:(h + 1) * 128], gaq_ref[...]), c128, s128)
        q_ref[:, h * 128:(h + 1) * 128] = (x * qscale).astype(BF16)
    z = proj('ka')
    for h in range(A_KV_HEADS):
        x = _rope128(_rms(z[:, h * 128:(h + 1) * 128], gak_ref[...]), c128, s128)
        kf_ref[:, h * 128:(h + 1) * 128] = x
        kb_ref[:, h * 128:(h + 1) * 128] = x.astype(BF16)
    z = proj('va')
    vf_ref[...] = z
    vb_ref[...] = z.astype(BF16)
    z = proj('qi')
    for p in range(IDX_HEADS // 2):
        x = _rope64(z[:, p * 128:(p + 1) * 128], c64, s64)
        qi_ref[:, p * 128:(p + 1) * 128] = (x * (IDX_DIM ** -0.5)).astype(BF16)
    z = proj('kikr')
    lane = lax.broadcasted_iota(jnp.int32, z.shape, 1)
    x = jnp.where(lane < 64, z, _rms_half(z, gkr_ref[...]))
    x = _rope64(x, c64, s64)
    kikr_ref[...] = x
    kikrb_ref[...] = x.astype(BF16)
    wi_ref[...] = proj('wi') * (IDX_HEADS ** -0.5)
    lat = _rms(proj('ckv'), gkvlat_ref[...])
    lat_ref[...] = lat
    cq = _rms(proj('cq'), gqlat_ref[...]).astype(BF16)
    bscale = (B_NOPE + B_ROPE) ** -0.5
    zq = jnp.dot(cq, wuq_ref[:, :B_HEADS * B_NOPE], preferred_element_type=F32)
    for h in range(B_HEADS):
        x = _rms(zq[:, h * 128:(h + 1) * 128], gqn_ref[...])
        qn_ref[:, h * 128:(h + 1) * 128] = (x * bscale).astype(BF16)
    zq = jnp.dot(cq, wuq_ref[:, B_HEADS * B_NOPE:], preferred_element_type=F32)
    for p in range(B_HEADS // 2):
        x = _rope64(_rms_half(zq[:, p * 128:(p + 1) * 128], gqr_ref[...]), c64, s64)
        qr_ref[:, p * 128:(p + 1) * 128] = (x * bscale).astype(BF16)


def _ab_proj(h, gmix, win, wuq, gaq, gak, gqlat, gkvlat, gqn, gqr2, gkr2, tabs, tm):
    r, d = h.shape
    row = lambda w: pl.BlockSpec((tm, w), lambda i: (i, 0))
    outs = [(1024, BF16), (256, F32), (256, F32), (256, BF16), (256, BF16), (1024, BF16), (128, F32),
            (128, BF16), (128, F32), (1024, BF16), (512, BF16), (256, F32)]
    return pl.pallas_call(
        _ab_proj_kernel,
        grid=(r // tm,),
        in_specs=[row(d), _const_spec(gmix.shape), _const_spec(win.shape), _const_spec(wuq.shape),
                  _const_spec(gaq.shape), _const_spec(gak.shape), _const_spec(gqlat.shape),
                  _const_spec(gkvlat.shape), _const_spec(gqn.shape), _const_spec(gqr2.shape),
                  _const_spec(gkr2.shape), row(128), row(128), row(128), row(128)],
        out_specs=[row(w) for w, _ in outs],
        out_shape=[jax.ShapeDtypeStruct((r, w), dt) for w, dt in outs],
        compiler_params=_cparams("parallel"),
        name="ab_proj",
    )(h, gmix, win, wuq, gaq, gak, gqlat, gkvlat, gqn, gqr2, gkr2, *tabs)


def _ukv_kernel(lat_ref, w_ref, gkn_ref, kn_ref, vb_ref):
    lat = lat_ref[...].astype(BF16)
    z = jnp.dot(lat, w_ref[:, :B_HEADS * B_NOPE], preferred_element_type=F32)
    for h in range(B_HEADS):
        kn_ref[:, h * 128:(h + 1) * 128] = _rms(z[:, h * 128:(h + 1) * 128], gkn_ref[...]).astype(BF16)
    vb_ref[...] = jnp.dot(lat, w_ref[:, B_HEADS * B_NOPE:], preferred_element_type=F32).astype(BF16)


def _ukv(lat, w, gkn, tm):
    n = lat.shape[0]
    row = lambda wd: pl.BlockSpec((tm, wd), lambda i: (i, 0))
    return pl.pallas_call(
        _ukv_kernel,
        grid=(n // tm,),
        in_specs=[row(lat.shape[1]), _const_spec(w.shape), _const_spec(gkn.shape)],
        out_specs=[row(1024), row(1024)],
        out_shape=[jax.ShapeDtypeStruct((n, 1024), BF16)] * 2,
        compiler_params=_cparams("parallel"),
        name="mla_ukv",
    )(lat, w, gkn)


def _dot_t(a, b):
    return lax.dot_general(a, b, (((1,), (1,)), ((), ())), preferred_element_type=F32)


def _float_key(x):
    bits = pltpu.bitcast(x, jnp.int32)
    return bits ^ ((bits >> 31) & jnp.int32(0x7FFFFFFF))


def _kth_largest(keys, k):
    rows = keys[0].shape[0]

    def body(i, t):
        bit = jnp.int32(31) - i
        cand = t + jnp.left_shift(jnp.int32(1), bit)
        cnt = jnp.zeros((rows, 1), F32)
        for key in keys:
            cnt = cnt + jnp.sum(jnp.where(key >= cand, 1.0, 0.0), axis=-1, keepdims=True)
        return jnp.where(cnt >= float(k), cand, t)

    return lax.fori_loop(0, 32, body, jnp.full((rows, 1), INT_MIN, jnp.int32))


def _dsa_core(qi, wi, q, cq, pieces, topk):
    nq = qi.shape[0]
    keys, adms = [], []
    for ki, _, _, ck in pieces:
        score = jnp.zeros((nq, ki.shape[0]), F32)
        for h in range(IDX_HEADS):
            s_h = _dot_t(qi[:, h * 64:(h + 1) * 64], ki)
            score = score + jnp.maximum(s_h, 0.0) * wi[:, h:h + 1]
        adm = ck <= cq
        keys.append(jnp.where(adm, _float_key(score), INT_MIN))
        adms.append(adm)
    thr = _kth_largest(keys, topk)
    group = A_HEADS // A_KV_HEADS
    biases = [jnp.concatenate([jnp.where(jnp.logical_and(adm, key >= thr), 0.0, NEG)] * group, axis=0)
              for key, adm in zip(keys, adms)]
    outs = []
    for c in range(A_KV_HEADS):
        qg = jnp.concatenate([q[:, (c * group + g) * 128:(c * group + g + 1) * 128] for g in range(group)], axis=0)
        ss = [_dot_t(qg, k[:, c * 128:(c + 1) * 128]) + bias for (_, k, _, _), bias in zip(pieces, biases)]
        m =functools.reduce(jnp.maximum, [jnp.max(s, axis=-1, keepdims=True) for s in ss])
        ps = [jnp.exp(s - m) for s in ss]
        den = functools.reduce(jnp.add, [jnp.sum(p, axis=-1, keepdims=True) for p in ps])
        o = functools.reduce(jnp.add, [jnp.dot(p.astype(BF16), v[:, c * 128:(c + 1) * 128],
                                               preferred_element_type=F32)
                                       for p, (_, _, v, _) in zip(ps, pieces)])
        o = o / den
        outs.extend([o[g * nq:(g + 1) * nq] for g in range(group)])
    return jnp.concatenate(outs, axis=-1)


def _mla_core(qn, qr, cq, pieces):
    outs = []
    for h in range(B_HEADS):
        ss = []
        for kn, kr, _, ck in pieces:
            s = _dot_t(qn[:, h * 128:(h + 1) * 128], kn[:, h * 128:(h + 1) * 128])
            s = s + _dot_t(qr[:, h * 64:(h + 1) * 64], kr)
            ss.append(jnp.where(ck <= cq, s, NEG))
        m = functools.reduce(jnp.maximum, [jnp.max(s, axis=-1, keepdims=True) for s in ss])
        ps = [jnp.exp(s - m) for s in ss]
        den = functools.reduce(jnp.add, [jnp.sum(p, axis=-1, keepdims=True) for p in ps])
        o = functools.reduce(jnp.add, [jnp.dot(p.astype(BF16), vb[:, h * 128:(h + 1) * 128],
                                               preferred_element_type=F32)
                                       for p, (_, _, vb, _) in zip(ps, pieces)])
        outs.append(o / den)
    return jnp.concatenate(outs, axis=-1)


def _attn_prompt_kernel(qi_ref, wi_ref, q_ref, qn_ref, qr_ref, cq_ref, kikr_ref, k_ref, v_ref, kn_ref, vb_ref,
                        ck_ref, oa_ref, ob_ref, *, topk):
    cq = cq_ref[:, 0:1]
    ck = ck_ref[...]
    valid = cq >= 0
    kikr = kikr_ref[...]
    oa = _dsa_core(qi_ref[...], wi_ref[...], q_ref[...], cq, [(kikr[:, :64], k_ref[...], v_ref[...], ck)], topk)
    oa_ref[...] = jnp.where(valid, oa, 0.0).astype(BF16)
    ob = _mla_core(qn_ref[...], qr_ref[...], cq, [(kn_ref[...], kikr[:, 64:], vb_ref[...], ck)])
    ob_ref[...] = jnp.where(valid, ob, 0.0).astype(BF16)


def _attn_prompt(qi, wi, q, qn, qr, cq, kikr, k, v, kn, vb, ck, nb, tp, topk):
    tq = ROW_BLOCK
    nj = tp // tq
    qspec = lambda w: pl.BlockSpec((tq, w), lambda b, j: (b * nj + j, 0))
    kspec = lambda w: pl.BlockSpec((tp, w), lambda b, j: (b, 0))
    return pl.pallas_call(
        functools.partial(_attn_prompt_kernel, topk=topk),
        grid=(nb, nj),
        in_specs=[qspec(1024), qspec(128), qspec(1024), qspec(1024), qspec(512), qspec(128),
                  kspec(128), kspec(256), kspec(256), kspec(1024), kspec(1024), _const_spec(ck.shape)],
        out_specs=[qspec(1024), qspec(1024)],
        out_shape=[jax.ShapeDtypeStruct((nb * tp, 1024), BF16)] * 2,
        compiler_params=_cparams("parallel", "arbitrary"),
        name="attn_prompt",
    )(qi, wi, q, qn, qr, cq, kikr, k, v, kn, vb, ck)


def _attn_sample_kernel(qi_ref, wi_ref, q_ref, qn_ref, qr_ref, cq_ref, kikr_ref, k_ref, v_ref, kn_ref, vb_ref,
                        pki_ref, pk_ref, pv_ref, pkn_ref, pkr_ref, pvb_ref, ckp_ref, ckn_ref,
                        oa_ref, ob_ref, *, topk):
    cq = cq_ref[:, 0:1]
    ckp, ckn = ckp_ref[...], ckn_ref[...]
    kikr = kikr_ref[...]
    past_a = (pki_ref[0].astype(BF16), pk_ref[0].astype(BF16), pv_ref[0].astype(BF16), ckp)
    new_a = (kikr[:, :64], k_ref[...], v_ref[...], ckn)
    oa_ref[...] = _dsa_core(qi_ref[...], wi_ref[...], q_ref[...], cq, [past_a, new_a], topk).astype(BF16)
    past_b = (pkn_ref[...], pkr_ref[0].astype(BF16), pvb_ref[...], ckp)
    new_b = (kn_ref[...], kikr[:, 64:], vb_ref[...], ckn)
    ob_ref[...] = _mla_core(qn_ref[...], qr_ref[...], cq, [past_b, new_b]).astype(BF16)


def _attn_sample(qi, wi, q, qn, qr, cq, kikr, k, v, kn, vb, pki, pk, pv, pkn, pkr, pvb, ckp, ckn,
                 row0, nb, ts, topk):
    past = pk.shape[1]
    blk0 = row0 // ts
    nspec = lambda w: pl.BlockSpec((ts, w), lambda b: (blk0 + b, 0))
    pspec = lambda w: pl.BlockSpec((1, past, w), lambda b: (b, 0, 0))
    p2spec = lambda w: pl.BlockSpec((past, w), lambda b: (b, 0))
    return pl.pallas_call(
        functools.partial(_attn_sample_kernel, topk=topk),
        grid=(nb,),
        in_specs=[nspec(1024), nspec(128), nspec(1024), nspec(1024), nspec(512), nspec(128),
                  nspec(128), nspec(256), nspec(256), nspec(1024), nspec(1024),
                  pspec(64), pspec(256), pspec(256), p2spec(1024), pspec(64), p2spec(1024),
                  _const_spec(ckp.shape), _const_spec(ckn.shape)],
        out_specs=[pl.BlockSpec((ts, 1024), lambda b: (b, 0))] * 2,
        out_shape=[jax.ShapeDtypeStruct((nb * ts, 1024), BF16)] * 2,
        compiler_params=_cparams("parallel"),
        name="attn_sample",
    )(qi, wi, q, qn, qr, cq, kikr, k, v, kn, vb, pki, pk, pv, pkn, pkr, pvb, ckp, ckn)


def _out_proj_kernel(h_ref, oa_ref, ob_ref, w_ref, o_ref):
    half = oa_ref.shape[1]
    y = jnp.dot(oa_ref[...], w_ref[:half, :], preferred_element_type=F32)
    y = y + jnp.dot(ob_ref[...], w_ref[half:, :], preferred_element_type=F32)
    o_ref[...] = h_ref[...] + y


def _out_proj(h, oa, ob, w, tm):
    r, d = h.shape
    row = lambda wd: pl.BlockSpec((tm, wd), lambda i: (i, 0))
    return pl.pallas_call(
        _out_proj_kernel,
        grid=(r // tm,),
        in_specs=[row(d), row(oa.shape[1]), row(ob.shape[1]), _const_spec(w.shape)],
        out_specs=row(d),
        out_shape=jax.ShapeDtypeStruct((r, d), F32),
        compiler_params=_cparams("parallel"),
        name="out_proj",
    )(h, oa, ob, w)


def _mlp_kernel(h_ref, g_ref, wup_ref, wdn_ref, o_ref, xn_ref):
    f = pl.program_id(1)

    @pl.when(f == 0)
    def _():
        x = h_ref[...]
        xn_ref[...] = _rms(x, g_ref[...]).astype(BF16)
        o_ref[...] = x

    u = jnp.dot(xn_ref[...], wup_ref[...], preferred_element_type=F32)
    u = jnp.maximum(u, 0.0)
    o_ref[...] += jnp.dot((u * u).astype(BF16), wdn_ref[...], preferred_element_type=F32)


def _mlp(h, g, wup, wdn, tm, tf):
    r, d = h.shape
    dff = wup.shape[1]
    return pl.pallas_call(
        _mlp_kernel,
        grid=(r // tm, dff // tf),
        in_specs=[pl.BlockSpec((tm, d), lambda i, f: (i, 0)), pl.BlockSpec((1, d), lambda i, f: (0, 0)),
                  pl.BlockSpec((d, tf), lambda i, f: (0, f)), pl.BlockSpec((tf, d), lambda i, f: (f, 0))],
        out_specs=pl.BlockSpec((tm, d), lambda i, f: (i, 0)),
        out_shape=jax.ShapeDtypeStruct((r, d), F32),
        scratch_shapes=[pltpu.VMEM((tm, d), BF16)],
        compiler_params=_cparams("parallel", "arbitrary"),
        name="mlp",
    )(h, g, wup, wdn)


def _cd_proj_kernel(h_ref, g_ref, w_ref, z_ref, xn_ref):
    @pl.when(pl.program_id(1) == 0)
    def _():
        xn_ref[...] = _rms(h_ref[...], g_ref[...]).astype(BF16)

    z_ref[...] = jnp.dot(xn_ref[...], w_ref[...], preferred_element_type=F32)


def _cd_proj(h, g, w, tm, tn):
    r, d = h.shape
    n = w.shape[1]
    return pl.pallas_call(
        _cd_proj_kernel,
        grid=(r // tm, n // tn),
        in_specs=[pl.BlockSpec((tm, d), lambda i, j: (i, 0)), pl.BlockSpec((1, d), lambda i, j: (0, 0)),
                  pl.BlockSpec((d, tn), lambda i, j: (0, j))],
        out_specs=pl.BlockSpec((tm, tn), lambda i, j: (i, j)),
        out_shape=jax.ShapeDtypeStruct((r, n), F32),
        scratch_shapes=[pltpu.VMEM((tm, d), BF16)],
        compiler_params=_cparams("parallel", "arbitrary"),
        name="cd_proj",
    )(h, g, w)


def _rglru_kernel(cx_ref, cg_ref, valid_ref, prev0_ref, h0_ref, cw_ref, cb_ref, gw_ref, gab_ref, gxb_ref, lam_ref,
                  o_ref, hl_ref, prev_s, h_s, a_s, b_s):
    @pl.when(pl.program_id(1) == 0)
    def _():
        prev_s[...] = prev0_ref[0]
        h_s[...] = jnp.broadcast_to(h0_ref[0], h_s.shape)

    cx = cx_ref[...]
    tm = cx.shape[0]
    full = jnp.concatenate([prev_s[...], cx], axis=0)
    prev_s[...] = cx[tm - 8:, :]
    xc = cb_ref[...] + cx * cw_ref[CONV_W - 1:CONV_W, :]
    for j in range(CONV_W - 1):
        sh = CONV_W - 1 - j
        xc = xc + full[8 - sh:8 - sh + tm, :] * cw_ref[j:j + 1, :]
    valid = valid_ref[:, 0:1] >= 0
    softplus = jnp.logaddexp(-lam_ref[...], 0.0)
    for n in range(C_BLOCKS):
        sl = slice(n * C_BLOCK, (n + 1) * C_BLOCK)
        xb = xc[:, sl]
        gates = jnp.dot(xb.astype(BF16), gw_ref[n], preferred_element_type=F32)
        r = jax.nn.sigmoid(gates[:, :C_BLOCK] + gab_ref[:, sl])
        ig = jax.nn.sigmoid(gates[:, C_BLOCK:] + gxb_ref[:, sl])
        log_a = -RG_C * r * softplus[:, sl]
        a = jnp.exp(log_a)
        a_s[:, sl] = a
        b = jnp.sqrt(1.0 - a * a) * (ig * xb)
        b_s[:, sl] = jnp.where(valid, b, 0.0)

    rid = lax.broadcasted_iota(jnp.int32, h_s.shape, 0)

    def group(g, h):
        rows = pl.ds(pl.multiple_of(g * 8, 8), 8)
        ca, cb = a_s[rows, :], b_s[rows, :]
        for s in (1, 2, 4):
            m = rid >= s
            cb = jnp.where(m, ca * pltpu.roll(cb, s, 0) + cb, cb)
            ca = jnp.where(m, ca * pltpu.roll(ca, s, 0), ca)
        hh = ca * h + cb
        b_s[rows, :] = hh
        return jnp.broadcast_to(hh[7:8, :], hh.shape)

    h = lax.fori_loop(0, tm // 8, group, h_s[...])
    h_s[...] = h
    hl_ref[0] = h[0:1, :]
    o_ref[...] = (b_s[...] * jax.nn.gelu(cg_ref[...])).astype(BF16)


def _rglru(z, valid, prev0, h0, cw, cb, gw, gab, gxb, lam, row0, nb, t, tm):
    cwid = C_BLOCKS * C_BLOCK
    nj = t // tm
    b0 = row0 // tm
    zspec = lambda c: pl.BlockSpec((tm, cwid), lambda b, j: (b0 + b * nj + j, c))
    return pl.pallas_call(
        _rglru_kernel,
        grid=(nb, nj),
        in_specs=[zspec(0), zspec(1), pl.BlockSpec((tm, LANES), lambda b, j: (b0 + b * nj + j, 0)),
                  pl.BlockSpec((1, 8, cwid), lambda b, j: (b, 0, 0)), pl.BlockSpec((1, 1, cwid), lambda b, j: (b, 0, 0)),
                  _const_spec(cw.shape), _const_spec(cb.shape), _const_spec(gw.shape), _const_spec(gab.shape),
                  _const_spec(gxb.shape), _const_spec(lam.shape)],
        out_specs=[pl.BlockSpec((tm, cwid), lambda b, j: (b * nj + j, 0)),
                   pl.BlockSpec((1, 1, cwid), lambda b, j: (b, 0, 0))],
        out_shape=[jax.ShapeDtypeStruct((nb * t, cwid), BF16), jax.ShapeDtypeStruct((nb, 1, cwid), F32)],
        scratch_shapes=[pltpu.VMEM((8, cwid), F32), pltpu.VMEM((8, cwid), F32), pltpu.VMEM((tm, cwid), F32),
                        pltpu.VMEM((tm, cwid), F32)],
        compiler_params=_cparams("parallel", "arbitrary"),
        name="rglru",
    )(z, z, valid, prev0, h0, cw, cb, gw, gab, gxb, lam)


def _retention_kernel(q_ref, k_ref, v_ref, g_ref, cos_ref, sin_ref, s0_ref, gdn_ref, lg_ref,
                      o_ref, s_ref, *, blk):
    j = pl.program_id(2)
    hd = pl.program_id(1)

    @pl.when(j == 0)
    def _():
        s_ref[0, 0] = s0_ref[0, 0]

    cos, sin_signed = cos_ref[...], sin_ref[...]

    def rope(x):
        return x * cos + pltpu.roll(x, D_KDIM // 2, 1) * sin_signed

    lg = lg_ref[hd]
    q = rope(q_ref[...])
    k = rope(k_ref[...]) * (D_KDIM ** -0.5)
    v = v_ref[...]
    ri = lax.broadcasted_iota(jnp.int32, (blk, blk), 0)
    ci = lax.broadcasted_iota(jnp.int32, (blk, blk), 1)
    diff = (ri - ci).astype(F32)
    decay = jnp.where(diff >= 0, jnp.exp(lg * jnp.maximum(diff, 0.0)), 0.0)
    qb, kb, vb = q.astype(BF16), k.astype(BF16), v.astype(BF16)
    att = _dot_t(qb, kb) * decay
    o = jnp.dot(att.astype(BF16), vb, preferred_element_type=F32)
    rows = lax.broadcasted_iota(jnp.int32, (blk, 1), 0).astype(F32)
    s_prev = s_ref[0, 0]
    o = o + jnp.dot(qb, s_prev.astype(BF16), preferred_element_type=F32) * jnp.exp(lg * (rows + 1.0))
    kdec = (k * jnp.exp(lg * (blk - 1.0 - rows))).astype(BF16)
    s_new = jnp.exp(lg * blk) * s_prev + lax.dot_general(kdec, vb, (((0,), (0,)), ((), ())),
                                                         preferred_element_type=F32)
    s_ref[0, 0] = s_new
    od = _rms(o, gdn_ref[...]) * jax.nn.silu(g_ref[...])
    o_ref[...] = od.astype(BF16)


def _retention(z, cos, sin_signed, s0, gdn, log_gamma, cols, row0, nb, t, blk):
    nj = t // blk
    b0 = row0 // blk
    cq, ck, cv, cg = [c // D_KDIM for c in cols]
    zspec = lambda c0: pl.BlockSpec((blk, D_KDIM), lambda b, h, j: (b0 + b * nj + j, c0 + h))
    tspec = pl.BlockSpec((blk, D_KDIM), lambda b, h, j: (b0 + b * nj + j, 0))
    sspec = pl.BlockSpec((1, 1, D_KDIM, D_VDIM), lambda b, h, j: (b, h, 0, 0))
    return pl.pallas_call(
        functools.partial(_retention_kernel, blk=blk),
        grid=(nb, D_HEADS, nj),
        in_specs=[zspec(cq), zspec(ck), zspec(cv), zspec(cg), tspec, tspec, sspec, _const_spec(gdn.shape),
                  pl.BlockSpec(memory_space=pltpu.SMEM)],
        out_specs=[pl.BlockSpec((blk, D_VDIM), lambda b, h, j: (b * nj + j, h)), sspec],
        out_shape=[jax.ShapeDtypeStruct((nb * t, D_HEADS * D_VDIM), BF16),
                   jax.ShapeDtypeStruct(s0.shape, F32)],
        compiler_params=_cparams("parallel", "parallel", "arbitrary"),
        name="retention",
    )(z, z, z, z, cos, sin_signed, s0, gdn, log_gamma)


def _rope_tables(pos, d, width):
    inv = ROPE_THETA ** (-jnp.arange(0, d, 2, dtype=F32) / d)
    ang = pos.astype(F32)[:, None] * inv[None, :]
    cos, sin = jnp.cos(ang), jnp.sin(ang)
    reps = width // d
    return (jnp.tile(jnp.concatenate([cos, cos], axis=-1), (1, reps)),
            jnp.tile(jnp.concatenate([-sin, sin], axis=-1), (1, reps)))


def kernel(x_prompt, x_sample, cache_a_k, cache_a_v, cache_a_kidx, cache_b_latent, cache_b_krope, state_c_conv,
           state_c_h, state_d_s, meta_tokens, norm_mix, norm_mlp, ab_w_in, ab_w_out, a_q_norm, a_k_norm,
           b_q_lat_norm, b_w_uq, b_kv_lat_norm, b_w_ukv, b_qn_norm, b_qr_norm, b_kn_norm, b_kr_norm, cd_w_in,
           cd_w_out, c_conv_w, c_conv_b, c_gate_a_w, c_gate_a_b, c_gate_x_w, c_gate_x_b, c_lambda, d_out_norm,
           mlp_w_up, mlp_w_down):
    nb_p, seq_p, d_model = x_prompt.shape
    nb_s, seq_s, _ = x_sample.shape
    past = cache_a_k.shape[2]
    n_meta = meta_tokens.shape[0]
    depth = norm_mix.shape[0]
    t_real = n_meta + seq_p
    tp = -(-t_real // ROW_BLOCK) * ROW_BLOCK
    pad = tp - t_real
    rp = nb_p * tp
    rs = nb_s * seq_s
    assert seq_s % 8 == 0 and rp % seq_s == 0 and past % 8 == 0
    r_real = rp + rs
    tile = 512
    r_tot = -(-r_real // tile) * tile
    topk_p = min(TOPK_MAX, seq_p // 4)
    topk_s = min(TOPK_MAX, (past + seq_s) // 4)

    pos_p = jnp.concatenate([jnp.zeros((pad,), jnp.int32), jnp.arange(t_real, dtype=jnp.int32)])
    ck_p = jnp.concatenate([jnp.zeros((n_meta,), jnp.int32), 1 + jnp.arange(seq_p, dtype=jnp.int32) // CHUNK])
    cq_p = jnp.concatenate([jnp.full((pad,), -1, jnp.int32), ck_p])
    ckk_p = jnp.concatenate([jnp.full((pad,), 2 ** 30, jnp.int32), ck_p])[None, :]
    pos_s_all = jnp.arange(past + seq_s, dtype=jnp.int32)
    ck_s = pos_s_all // CHUNK
    pos_s = pos_s_all[past:]
    tail = r_tot - r_real
    pos_rows = jnp.concatenate([jnp.tile(pos_p, nb_p), jnp.tile(pos_s, nb_s), jnp.zeros((tail,), jnp.int32)])
    cq_rows = jnp.concatenate([jnp.tile(cq_p, nb_p), jnp.tile(ck_s[past:], nb_s), jnp.full((tail,), -1, jnp.int32)])
    cq_rows = jnp.broadcast_to(cq_rows[:, None], (r_tot, LANES))
    tabs128 = _rope_tables(pos_rows, 128, 128)
    tabs64 = _rope_tables(pos_rows, 64, 128)
    tabs256 = _rope_tables(pos_rows, 256, 256)

    hp = jnp.concatenate([jnp.zeros((nb_p, pad, d_model), F32),
                          jnp.broadcast_to(meta_tokens[None].astype(F32), (nb_p, n_meta, d_model)),
                          x_prompt], axis=1).reshape(rp, d_model)
    h = jnp.concatenate([hp, x_sample.reshape(rs, d_model), jnp.zeros((tail, d_model), F32)], axis=0)

    def prompt_rows(x):
        return x[:rp].reshape(nb_p, tp, -1)[:, pad:]

    def sample_rows(x):
        return x[rp:rp + rs].reshape(nb_s, seq_s, -1)

    ab_p, ab_s, cd_p, cd_s = [], [], [], []
    for layer in range(depth):
        i = layer // 2
        gmix = norm_mix[layer][None, :]
        if layer % 2 == 0:
            w = ab_w_in[i]
            offs = np.cumsum([0, 1024, 256, 256, 1024, 64, 16, 512, 256, 64])
            sec = lambda n: w[:, offs[n]:offs[n + 1]]
            win = jnp.concatenate([sec(0), sec(1), sec(2), sec(3), sec(6), sec(7), sec(4), sec(8), sec(5),
                                   jnp.zeros((d_model, AB_IN_PAD - 3472), F32)], axis=1).astype(BF16)
            wuq = b_w_uq[i].reshape(-1, B_HEADS, B_NOPE + B_ROPE)
            wuq = jnp.concatenate([wuq[:, :, :B_NOPE].reshape(-1, B_HEADS * B_NOPE),
                                   wuq[:, :, B_NOPE:].reshape(-1, B_HEADS * B_ROPE)], axis=1).astype(BF16)
            wukv = b_w_ukv[i].reshape(-1, B_HEADS, B_NOPE + B_V)
            wukv = jnp.concatenate([wukv[:, :, :B_NOPE].reshape(-1, B_HEADS * B_NOPE),
                                    wukv[:, :, B_NOPE:].reshape(-1, B_HEADS * B_V)], axis=1).astype(BF16)
            two = lambda g: jnp.concatenate([g, g])[None, :]
            gkr2 = jnp.concatenate([jnp.ones((64,), F32), b_kr_norm[i]])[None, :]
            (q, kf, vf, kb, vb_a, qi, kikr, kikrb, wi, qn, qr, lat) = _ab_proj(
                h, gmix, win, wuq, a_q_norm[i][None, :], a_k_norm[i][None, :], b_q_lat_norm[i][None, :],
                b_kv_lat_norm[i][None, :], b_qn_norm[i][None, :], two(b_qr_norm[i]), gkr2,
                tabs128 + tabs64, 256)
            gkn = b_kn_norm[i][None, :]
            kn, vbb = _ukv(lat, wukv, gkn, 512)
            pkn, pvb = _ukv(cache_b_latent[i].reshape(nb_s * past, -1), wukv, gkn, 512)
            oa_p, ob_p = _attn_prompt(qi, wi, q, qn, qr, cq_rows, kikrb, kb, vb_a, kn, vbb, ckk_p, nb_p, tp, topk_p)
            oa_s, ob_s = _attn_sample(qi, wi, q, qn, qr, cq_rows, kikrb, kb, vb_a, kn, vbb,
                                      cache_a_kidx[i], cache_a_k[i].reshape(nb_s, past, -1),
                                      cache_a_v[i].reshape(nb_s, past, -1), pkn, cache_b_krope[i], pvb,
                                      ck_s[None, :past], ck_s[None, past:], rp, nb_s, seq_s, topk_s)
            zt = jnp.zeros((tail, 1024), BF16)
            oa = jnp.concatenate([oa_p[:rp], oa_s, zt], axis=0)
            ob = jnp.concatenate([ob_p[:rp], ob_s, zt], axis=0)
            h = _out_proj(h, oa, ob, ab_w_out[i].astype(BF16), 512)
            kv4 = lambda x: x.reshape(x.shape[0], x.shape[1], A_KV_HEADS, A_HEAD_DIM)
            ab_p.append((kv4(prompt_rows(kf)), kv4(prompt_rows(vf)), prompt_rows(kikr)[..., :64],
                         prompt_rows(lat), prompt_rows(kikr)[..., 64:]))
            ab_s.append((kv4(sample_rows(kf)), kv4(sample_rows(vf)), sample_rows(kikr)[..., :64],
                         sample_rows(lat), sample_rows(kikr)[..., 64:]))
        else:
            z = _cd_proj(h, gmix, cd_w_in[i].astype(BF16), 512, 1024)
            cwid = C_BLOCKS * C_BLOCK
            gw = jnp.concatenate([c_gate_a_w[i], c_gate_x_w[i]], axis=-1).astype(BF16)
            cargs = (c_conv_w[i], c_conv_b[i][None, :], gw, c_gate_a_b[i][None, :], c_gate_x_b[i][None, :],
                     c_lambda[i][None, :])
            oc_p, hl_p = _rglru(z, cq_rows, jnp.zeros((nb_p, 8, cwid), F32), jnp.zeros((nb_p, 1, cwid), F32),
                                *cargs, 0, nb_p, tp, ROW_BLOCK)
            prev_s = jnp.concatenate([jnp.zeros((nb_s, 8 - (CONV_W - 1), cwid), F32), state_c_conv[i]], axis=1)
            oc_s, hl_s = _rglru(z, cq_rows, prev_s, state_c_h[i][:, None, :], *cargs, rp, nb_s, seq_s, seq_s)
            hl_p, hl_s = hl_p[:, 0], hl_s[:, 0]
            log_gamma = jnp.log(1.0 - 2.0 ** (-5.0 - jnp.arange(D_HEADS, dtype=F32)))
            dcols = (2 * cwid, 2 * cwid + 1024, 2 * cwid + 2048, 2 * cwid + 3072)
            gdn = d_out_norm[i][None, :]
            od_p, ds_p = _retention(z, *tabs256, jnp.zeros((nb_p, D_HEADS, D_KDIM, D_VDIM), F32), gdn, log_gamma,
                                    dcols, 0, nb_p, tp, ROW_BLOCK)
            od_s, ds_s = _retention(z, *tabs256, state_d_s[i], gdn, log_gamma, dcols, rp, nb_s, seq_s, seq_s)
            zt = jnp.zeros((tail, 1024), BF16)
            oc = jnp.concatenate([oc_p, oc_s, zt], axis=0)
            od = jnp.concatenate([od_p, od_s, zt], axis=0)
            h = _out_proj(h, oc, od, cd_w_out[i].astype(BF16), 512)
            cx = z[:, :cwid]
            cd_p.append((prompt_rows(cx)[:, t_real - (CONV_W - 1):], hl_p, ds_p))
            full_s = jnp.concatenate([state_c_conv[i], sample_rows(cx)], axis=1)
            cd_s.append((full_s[:, seq_s:], hl_s, ds_s))
        h = _mlp(h, norm_mlp[layer][None, :], mlp_w_up[layer].astype(BF16), mlp_w_down[layer].astype(BF16),
                 512, 512)

    def stack(entries, j):
        return jnp.stack([e[j] for e in entries])

    y_p = prompt_rows(h)[:, n_meta:]
    y_s = sample_rows(h)
    return (y_p, y_s,
            stack(ab_p, 0), stack(ab_p, 1), stack(ab_p, 2), stack(ab_p, 3), stack(ab_p, 4),
            stack(cd_p, 0), stack(cd_p, 1), stack(cd_p, 2),
            stack(ab_s, 0), stack(ab_s, 1), stack(ab_s, 2), stack(ab_s, 3), stack(ab_s, 4),
            stack(cd_s, 0), stack(cd_s, 1), stack(cd_s, 2))
```

```python
import functools
import math

import jax
import jax.numpy as jnp
import numpy as np
from jax import lax
from jax.experimental import pallas as pl
from jax.experimental.pallas import tpu as pltpu

F32 = jnp.float32
BF16 = jnp.bfloat16

CHUNK = 64
ROPE_THETA = 10000.0
EPS = 1e-6
A_HEADS, A_KV_HEADS, A_HEAD_DIM = 8, 2, 128
IDX_HEADS, IDX_DIM = 16, 64
TOPK_MAX = 256
B_HEADS, B_NOPE, B_ROPE, B_V = 8, 128, 64, 128
C_BLOCKS, C_BLOCK, CONV_W = 8, 128, 4
RG_C = 8.0
D_HEADS, D_KDIM, D_VDIM = 4, 256, 256

LANES = 128
ROW_BLOCK = 128
NEG = -1e30
INT_MIN = -2 ** 31
VMEM_LIMIT = 56 * 1024 * 1024


def _cparams(*sem):
    return pltpu.CompilerParams(dimension_semantics=sem, vmem_limit_bytes=VMEM_LIMIT)


def _const_spec(shape):
    nd = len(shape)
    return pl.BlockSpec(shape, lambda *_: (0,) * nd)


def _rms(x, g):
    ms = jnp.mean(x * x, axis=-1, keepdims=True)
    return x * lax.rsqrt(ms + EPS) * g


def _rms_half(x, g):
    lane = lax.broadcasted_iota(jnp.int32, x.shape, 1)
    lo = lane < 64
    xx = x * x
    s_lo = jnp.sum(jnp.where(lo, xx, 0.0), axis=-1, keepdims=True)
    s_hi = jnp.sum(jnp.where(lo, 0.0, xx), axis=-1, keepdims=True)
    ms = jnp.where(lo, s_lo, s_hi) * (1.0 / 64.0)
    return x * lax.rsqrt(ms + EPS) * g


def _rope128(x, cos, sin_signed):
    return x * cos + pltpu.roll(x, 64, 1) * sin_signed


def _rope64(x, cos, sin_signed):
    lane = lax.broadcasted_iota(jnp.int32, x.shape, 1)
    first = (lane % 64) < 32
    rot = jnp.where(first, pltpu.roll(x, 96, 1), pltpu.roll(x, 32, 1))
    return x * cos + rot * sin_signed


AB_COLS = dict(qa=(0, 1024), ka=(1024, 1280), va=(1280, 1536), qi=(1536, 2560), cq=(2560, 3072),
               ckv=(3072, 3328), kikr=(3328, 3456), wi=(3456, 3584))
AB_IN_PAD = 3584


def _ab_proj_kernel(h_ref, gmix_ref, win_ref, wuq_ref, gaq_ref, gak_ref, gqlat_ref, gkvlat_ref, gqn_ref,
                    gqr_ref, gkr_ref, c128_ref, s128_ref, c64_ref, s64_ref,
                    q_ref, kf_ref, vf_ref, kb_ref, vb_ref, qi_ref, kikr_ref, kikrb_ref, wi_ref, qn_ref, qr_ref,
                    lat_ref):
    xn = _rms(h_ref[...], gmix_ref[...]).astype(BF16)
    c128, s128, c64, s64 = c128_ref[...], s128_ref[...], c64_ref[...], s64_ref[...]

    def proj(name):
        a, b = AB_COLS[name]
        return jnp.dot(xn, win_ref[:, a:b], preferred_element_type=F32)

    z = proj('qa')
    qscale = A_HEAD_DIM ** -0.5
    for h in range(A_HEADS):
        x = _rope128(_rms(z[:, h * 128:(h + 1) * 128], gaq_ref[...]), c128, s128)
        q_ref[:, h * 128:(h + 1) * 128] = (x * qscale).astype(BF16)
    z = proj('ka')
    for h in range(A_KV_HEADS):
        x = _rope128(_rms(z[:, h * 128:(h + 1) * 128], gak_ref[...]), c128, s128)
        kf_ref[:, h * 128:(h + 1) * 128] = x
        kb_ref[:, h * 128:(h + 1) * 128] = x.astype(BF16)
    z = proj('va')
    vf_ref[...] = z
    vb_ref[...] = z.astype(BF16)
    z = proj('qi')
    for p in range(IDX_HEADS // 2):
        x = _rope64(z[:, p * 128:(p + 1) * 128], c64, s64)
        qi_ref[:, p * 128:(p + 1) * 128] = (x * (IDX_DIM ** -0.5)).astype(BF16)
    z = proj('kikr')
    lane = lax.broadcasted_iota(jnp.int32, z.shape, 1)
    x = jnp.where(lane < 64, z, _rms_half(z, gkr_ref[...]))
    x = _rope64(x, c64, s64)
    kikr_ref[...] = x
    kikrb_ref[...] = x.astype(BF16)
    wi_ref[...] = proj('wi') * (IDX_HEADS ** -0.5)
    lat = _rms(proj('ckv'), gkvlat_ref[...])
    lat_ref[...] = lat
    cq = _rms(proj('cq'), gqlat_ref[...]).astype(BF16)
    bscale = (B_NOPE + B_ROPE) ** -0.5
    zq = jnp.dot(cq, wuq_ref[:, :B_HEADS * B_NOPE], preferred_element_type=F32)
    for h in range(B_HEADS):
        x = _rms(zq[:, h * 128:(h + 1) * 128], gqn_ref[...])
        qn_ref[:, h * 128:(h + 1) * 128] = (x * bscale).astype(BF16)
    zq = jnp.dot(cq, wuq_ref[:, B_HEADS * B_NOPE:], preferred_element_type=F32)
    for p in range(B_HEADS // 2):
        x = _rope64(_rms_half(zq[:, p * 128:(p + 1) * 128], gqr_ref[...]), c64, s64)
        qr_ref[:, p * 128:(p + 1) * 128] = (x * bscale).astype(BF16)


def _ab_proj(h, gmix, win, wuq, gaq, gak, gqlat, gkvlat, gqn, gqr2, gkr2, tabs, tm):
    r, d = h.shape
    row = lambda w: pl.BlockSpec((tm, w), lambda i: (i, 0))
    outs = [(1024, BF16), (256, F32), (256, F32), (256, BF16), (256, BF16), (1024, BF16), (128, F32),
            (128, BF16), (128, F32), (1024, BF16), (512, BF16), (256, F32)]
    return pl.pallas_call(
        _ab_proj_kernel,
        grid=(r // tm,),
        in_specs=[row(d), _const_spec(gmix.shape), _const_spec(win.shape), _const_spec(wuq.shape),
                  _const_spec(gaq.shape), _const_spec(gak.shape), _const_spec(gqlat.shape),
                  _const_spec(gkvlat.shape), _const_spec(gqn.shape), _const_spec(gqr2.shape),
                  _const_spec(gkr2.shape), row(128), row(128), row(128), row(128)],
        out_specs=[row(w) for w, _ in outs],
        out_shape=[jax.ShapeDtypeStruct((r, w), dt) for w, dt in outs],
        compiler_params=_cparams("parallel"),
        name="ab_proj",
    )(h, gmix, win, wuq, gaq, gak, gqlat, gkvlat, gqn, gqr2, gkr2, *tabs)


def _ukv_kernel(lat_ref, w_ref, gkn_ref, kn_ref, vb_ref):
    lat = lat_ref[...].astype(BF16)
    z = jnp.dot(lat, w_ref[:, :B_HEADS * B_NOPE], preferred_element_type=F32)
    for h in range(B_HEADS):
        kn_ref[:, h * 128:(h + 1) * 128] = _rms(z[:, h * 128:(h + 1) * 128], gkn_ref[...]).astype(BF16)
    vb_ref[...] = jnp.dot(lat, w_ref[:, B_HEADS * B_NOPE:], preferred_element_type=F32).astype(BF16)


def _ukv(lat, w, gkn, tm):
    n = lat.shape[0]
    row = lambda wd: pl.BlockSpec((tm, wd), lambda i: (i, 0))
    return pl.pallas_call(
        _ukv_kernel,
        grid=(n // tm,),
        in_specs=[row(lat.shape[1]), _const_spec(w.shape), _const_spec(gkn.shape)],
        out_specs=[row(1024), row(1024)],
        out_shape=[jax.ShapeDtypeStruct((n, 1024), BF16)] * 2,
        compiler_params=_cparams("parallel"),
        name="mla_ukv",
    )(lat, w, gkn)


def _dot_t(a, b):
    return lax.dot_general(a, b, (((1,), (1,)), ((), ())), preferred_element_type=F32)


def _float_key(x):
    bits = pltpu.bitcast(x, jnp.int32)
    return bits ^ ((bits >> 31) & jnp.int32(0x7FFFFFFF))


def _kth_largest(keys, k):
    rows = keys[0].shape[0]

    def body(i, t):
        bit = jnp.int32(31) - i
        cand = t + jnp.left_shift(jnp.int32(1), bit)
        cnt = jnp.zeros((rows, 1), F32)
        for key in keys:
            cnt = cnt + jnp.sum(jnp.where(key >= cand, 1.0, 0.0), axis=-1, keepdims=True)
        return jnp.where(cnt >= float(k), cand, t)

    return lax.fori_loop(0, 32, body, jnp.full((rows, 1), INT_MIN, jnp.int32))


def _dsa_core(qi, wi, q, cq, pieces, topk):
    nq = qi.shape[0]
    keys, adms = [], []
    for ki, _, _, ck in pieces:
        score = jnp.zeros((nq, ki.shape[0]), F32)
        for h in range(IDX_HEADS):
            s_h = _dot_t(qi[:, h * 64:(h + 1) * 64], ki)
            score = score + jnp.maximum(s_h, 0.0) * wi[:, h:h + 1]
        adm = ck <= cq
        keys.append(jnp.where(adm, _float_key(score), INT_MIN))
        adms.append(adm)
    thr = _kth_largest(keys, topk)
    group = A_HEADS // A_KV_HEADS
    biases = [jnp.concatenate([jnp.where(jnp.logical_and(adm, key >= thr), 0.0, NEG)] * group, axis=0)
              for key, adm in zip(keys, adms)]
    outs = []
    for c in range(A_KV_HEADS):
        qg = jnp.concatenate([q[:, (c * group + g) * 128:(c * group + g + 1) * 128] for g in range(group)], axis=0)
        ss = [_dot_t(qg, k[:, c * 128:(c + 1) * 128]) + bias for (_, k, _, _), bias in zip(pieces, biases)]
        m =functools.reduce(jnp.maximum, [jnp.max(s, axis=-1, keepdims=True) for s in ss])
        ps = [jnp.exp(s - m) for s in ss]
        den = functools.reduce(jnp.add, [jnp.sum(p, axis=-1, keepdims=True) for p in ps])
        o = functools.reduce(jnp.add, [jnp.dot(p.astype(BF16), v[:, c * 128:(c + 1) * 128],
                                               preferred_element_type=F32)
                                       for p, (_, _, v, _) in zip(ps, pieces)])
        o = o / den
        outs.extend([o[g * nq:(g + 1) * nq] for g in range(group)])
    return jnp.concatenate(outs, axis=-1)


def _mla_core(qn, qr, cq, pieces):
    outs = []
    biases = [jnp.where(ck <= cq, 0.0, NEG) for _, _, _, ck in pieces]
    for h in range(B_HEADS):
        ss = []
        for (kn, kr, _, _), bias in zip(pieces, biases):
            s = _dot_t(qn[:, h * 128:(h + 1) * 128], kn[:, h * 128:(h + 1) * 128])
            s = s + _dot_t(qr[:, h * 64:(h + 1) * 64], kr)
            ss.append(s + bias)
        m = functools.reduce(jnp.maximum, [jnp.max(s, axis=-1, keepdims=True) for s in ss])
        ps = [jnp.exp(s - m) for s in ss]
        den = functools.reduce(jnp.add, [jnp.sum(p, axis=-1, keepdims=True) for p in ps])
        o = functools.reduce(jnp.add, [jnp.dot(p.astype(BF16), vb[:, h * 128:(h + 1) * 128],
                                               preferred_element_type=F32)
                                       for p, (_, _, vb, _) in zip(ps, pieces)])
        outs.append(o / den)
    return jnp.concatenate(outs, axis=-1)


def _attn_prompt_kernel(qi_ref, wi_ref, q_ref, qn_ref, qr_ref, cq_ref, kikr_ref, k_ref, v_ref, kn_ref, vb_ref,
                        ck_ref, oa_ref, ob_ref, *, topk, ranges):
    cq = cq_ref[:, 0:1]
    valid = cq >= 0
    j = pl.program_id(1)

    def attend(s_len):
        ck = ck_ref[:, :s_len]
        kikr = kikr_ref[:s_len, :]
        oa = _dsa_core(qi_ref[...], wi_ref[...], q_ref[...], cq,
                       [(kikr[:, :64], k_ref[:s_len, :], v_ref[:s_len, :], ck)], topk)
        oa_ref[...] = jnp.where(valid, oa, 0.0).astype(BF16)
        ob = _mla_core(qn_ref[...], qr_ref[...], cq, [(kn_ref[:s_len, :], kikr[:, 64:], vb_ref[:s_len, :], ck)])
        ob_ref[...] = jnp.where(valid, ob, 0.0).astype(BF16)

    for j0, j1 in ranges:
        pl.when(jnp.logical_and(j >= j0, j < j1))(functools.partial(attend, j1 * ROW_BLOCK))


def _attn_prompt(qi, wi, q, qn, qr, cq, kikr, k, v, kn, vb, ck, nb, tp, topk, n_ranges):
    tq = ROW_BLOCK
    nj = tp // tq
    edges = sorted({-(-nj * r // n_ranges) for r in range(n_ranges + 1)})
    ranges = tuple(zip(edges[:-1], edges[1:]))
    qspec = lambda w: pl.BlockSpec((tq, w), lambda b, j: (b * nj + j, 0))
    kspec = lambda w: pl.BlockSpec((tp, w), lambda b, j: (b, 0))
    return pl.pallas_call(
        functools.partial(_attn_prompt_kernel, topk=topk, ranges=ranges),
        grid=(nb, nj),
        in_specs=[qspec(1024), qspec(128), qspec(1024), qspec(1024), qspec(512), qspec(128),
                  kspec(128), kspec(256), kspec(256), kspec(1024), kspec(1024), _const_spec(ck.shape)],
        out_specs=[qspec(1024), qspec(1024)],
        out_shape=[jax.ShapeDtypeStruct((nb * tp, 1024), BF16)] * 2,
        compiler_params=_cparams("parallel", "arbitrary"),
        name="attn_prompt",
    )(qi, wi, q, qn, qr, cq, kikr, k, v, kn, vb, ck)


def _attn_sample_kernel(qi_ref, wi_ref, q_ref, qn_ref, qr_ref, cq_ref, kikr_ref, k_ref, v_ref, kn_ref, vb_ref,
                        pki_ref, pk_ref, pv_ref, pkn_ref, pkr_ref, pvb_ref, ckp_ref, ckn_ref,
                        oa_ref, ob_ref, *, topk):
    cq = cq_ref[:, 0:1]
    ckp, ckn = ckp_ref[...], ckn_ref[...]
    kikr = kikr_ref[...]
    past_a = (pki_ref[0].astype(BF16), pk_ref[0].astype(BF16), pv_ref[0].astype(BF16), ckp)
    new_a = (kikr[:, :64], k_ref[...], v_ref[...], ckn)
    oa_ref[...] = _dsa_core(qi_ref[...], wi_ref[...], q_ref[...], cq, [past_a, new_a], topk).astype(BF16)
    past_b = (pkn_ref[...], pkr_ref[0].astype(BF16), pvb_ref[...], ckp)
    new_b = (kn_ref[...], kikr[:, 64:], vb_ref[...], ckn)
    ob_ref[...] = _mla_core(qn_ref[...], qr_ref[...], cq, [past_b, new_b]).astype(BF16)


def _attn_sample(qi, wi, q, qn, qr, cq, kikr, k, v, kn, vb, pki, pk, pv, pkn, pkr, pvb, ckp, ckn,
                 row0, nb, ts, topk):
    past = pk.shape[1]
    blk0 = row0 // ts
    nspec = lambda w: pl.BlockSpec((ts, w), lambda b: (blk0 + b, 0))
    pspec = lambda w: pl.BlockSpec((1, past, w), lambda b: (b, 0, 0))
    p2spec = lambda w: pl.BlockSpec((past, w), lambda b: (b, 0))
    return pl.pallas_call(
        functools.partial(_attn_sample_kernel, topk=topk),
        grid=(nb,),
        in_specs=[nspec(1024), nspec(128), nspec(1024), nspec(1024), nspec(512), nspec(128),
                  nspec(128), nspec(256), nspec(256), nspec(1024), nspec(1024),
                  pspec(64), pspec(256), pspec(256), p2spec(1024), pspec(64), p2spec(1024),
                  _const_spec(ckp.shape), _const_spec(ckn.shape)],
        out_specs=[pl.BlockSpec((ts, 1024), lambda b: (b, 0))] * 2,
        out_shape=[jax.ShapeDtypeStruct((nb * ts, 1024), BF16)] * 2,
        compiler_params=_cparams("parallel"),
        name="attn_sample",
    )(qi, wi, q, qn, qr, cq, kikr, k, v, kn, vb, pki, pk, pv, pkn, pkr, pvb, ckp, ckn)


def _out_proj_kernel(h_ref, oap_ref, obp_ref, oas_ref, obs_ref, w_ref, o_ref, *, n_p, n_ps):
    half = oap_ref.shape[1]
    i = pl.program_id(0)

    def project(oa_ref, ob_ref):
        y = jnp.dot(oa_ref[...], w_ref[:half, :], preferred_element_type=F32)
        y = y + jnp.dot(ob_ref[...], w_ref[half:, :], preferred_element_type=F32)
        o_ref[...] = h_ref[...] + y

    pl.when(i < n_p)(functools.partial(project, oap_ref, obp_ref))
    pl.when(jnp.logical_and(i >= n_p, i < n_ps))(functools.partial(project, oas_ref, obs_ref))

    @pl.when(i >= n_ps)
    def _():
        o_ref[...] = h_ref[...]


def _out_proj(h, oa_p, ob_p, oa_s, ob_s, w, tm):
    r, d = h.shape
    n_p, n_s = oa_p.shape[0] // tm, oa_s.shape[0] // tm
    assert oa_p.shape[0] % tm == 0 and oa_s.shape[0] % tm == 0
    row = lambda wd: pl.BlockSpec((tm, wd), lambda i: (i, 0))
    pspec = lambda wd: pl.BlockSpec((tm, wd), lambda i: (jnp.minimum(i, n_p - 1), 0))
    sspec = lambda wd: pl.BlockSpec((tm, wd), lambda i: (jnp.clip(i - n_p, 0, n_s - 1), 0))
    return pl.pallas_call(
        functools.partial(_out_proj_kernel, n_p=n_p, n_ps=n_p + n_s),
        grid=(r // tm,),
        in_specs=[row(d), pspec(oa_p.shape[1]), pspec(ob_p.shape[1]), sspec(oa_s.shape[1]), sspec(ob_s.shape[1]),
                  _const_spec(w.shape)],
        out_specs=row(d),
        out_shape=jax.ShapeDtypeStruct((r, d), F32),
        compiler_params=_cparams("parallel"),
        name="out_proj",
    )(h, oa_p, ob_p, oa_s, ob_s, w)


def _mlp_kernel(h_ref, g_ref, wup_ref, wdn_ref, o_ref, xn_ref):
    f = pl.program_id(1)

    @pl.when(f == 0)
    def _():
        x = h_ref[...]
        xn_ref[...] = _rms(x, g_ref[...]).astype(BF16)
        o_ref[...] = x

    u = jnp.dot(xn_ref[...], wup_ref[...], preferred_element_type=F32)
    u = jnp.maximum(u, 0.0)
    o_ref[...] += jnp.dot((u * u).astype(BF16), wdn_ref[...], preferred_element_type=F32)


def _mlp(h, g, wup, wdn, tm, tf):
    r, d = h.shape
    dff = wup.shape[1]
    return pl.pallas_call(
        _mlp_kernel,
        grid=(r // tm, dff // tf),
        in_specs=[pl.BlockSpec((tm, d), lambda i, f: (i, 0)), pl.BlockSpec((1, d), lambda i, f: (0, 0)),
                  pl.BlockSpec((d, tf), lambda i, f: (0, f)), pl.BlockSpec((tf, d), lambda i, f: (f, 0))],
        out_specs=pl.BlockSpec((tm, d), lambda i, f: (i, 0)),
        out_shape=jax.ShapeDtypeStruct((r, d), F32),
        scratch_shapes=[pltpu.VMEM((tm, d), BF16)],
        compiler_params=_cparams("parallel", "arbitrary"),
        name="mlp",
    )(h, g, wup, wdn)


def _cd_proj_kernel(h_ref, g_ref, w_ref, z_ref, xn_ref):
    @pl.when(pl.program_id(1) == 0)
    def _():
        xn_ref[...] = _rms(h_ref[...], g_ref[...]).astype(BF16)

    z_ref[...] = jnp.dot(xn_ref[...], w_ref[...], preferred_element_type=F32)


def _cd_proj(h, g, w, tm, tn):
    r, d = h.shape
    n = w.shape[1]
    return pl.pallas_call(
        _cd_proj_kernel,
        grid=(r // tm, n // tn),
        in_specs=[pl.BlockSpec((tm, d), lambda i, j: (i, 0)), pl.BlockSpec((1, d), lambda i, j: (0, 0)),
                  pl.BlockSpec((d, tn), lambda i, j: (0, j))],
        out_specs=pl.BlockSpec((tm, tn), lambda i, j: (i, j)),
        out_shape=jax.ShapeDtypeStruct((r, n), F32),
        scratch_shapes=[pltpu.VMEM((tm, d), BF16)],
        compiler_params=_cparams("parallel", "arbitrary"),
        name="cd_proj",
    )(h, g, w)


def _rglru_kernel(cx_ref, cg_ref, valid_ref, prev0_ref, h0_ref, cw_ref, cb_ref, gw_ref, gab_ref, gxb_ref, lam_ref,
                  o_ref, hl_ref, prev_s, h_s, a_s, b_s):
    @pl.when(pl.program_id(1) == 0)
    def _():
        prev_s[...] = prev0_ref[0]
        h_s[...] = jnp.broadcast_to(h0_ref[0], h_s.shape)

    cx = cx_ref[...]
    tm = cx.shape[0]
    full = jnp.concatenate([prev_s[...], cx], axis=0)
    prev_s[...] = cx[tm - 8:, :]
    xc = cb_ref[...] + cx * cw_ref[CONV_W - 1:CONV_W, :]
    for j in range(CONV_W - 1):
        sh = CONV_W - 1 - j
        xc = xc + full[8 - sh:8 - sh + tm, :] * cw_ref[j:j + 1, :]
    valid = valid_ref[:, 0:1] >= 0
    softplus = jnp.logaddexp(-lam_ref[...], 0.0)
    for n in range(C_BLOCKS):
        sl = slice(n * C_BLOCK, (n + 1) * C_BLOCK)
        xb = xc[:, sl]
        gates = jnp.dot(xb.astype(BF16), gw_ref[n], preferred_element_type=F32)
        r = jax.nn.sigmoid(gates[:, :C_BLOCK] + gab_ref[:, sl])
        ig = jax.nn.sigmoid(gates[:, C_BLOCK:] + gxb_ref[:, sl])
        log_a = -RG_C * r * softplus[:, sl]
        a = jnp.exp(log_a)
        a_s[:, sl] = a
        b = jnp.sqrt(1.0 - a * a) * (ig * xb)
        b_s[:, sl] = jnp.where(valid, b, 0.0)

    rid = lax.broadcasted_iota(jnp.int32, h_s.shape, 0)

    def group(g, h):
        rows = pl.ds(pl.multiple_of(g * 8, 8), 8)
        ca, cb = a_s[rows, :], b_s[rows, :]
        for s in (1, 2, 4):
            m = rid >= s
            cb = jnp.where(m, ca * pltpu.roll(cb, s, 0) + cb, cb)
            ca = jnp.where(m, ca * pltpu.roll(ca, s, 0), ca)
        hh = ca * h + cb
        b_s[rows, :] = hh
        return jnp.broadcast_to(hh[7:8, :], hh.shape)

    h = lax.fori_loop(0, tm // 8, group, h_s[...])
    h_s[...] = h
    hl_ref[0] = h[0:1, :]
    o_ref[...] = (b_s[...] * jax.nn.gelu(cg_ref[...])).astype(BF16)


def _rglru(z, valid, prev0, h0, cw, cb, gw, gab, gxb, lam, row0, nb, t, tm):
    cwid = C_BLOCKS * C_BLOCK
    nj = t // tm
    b0 = row0 // tm
    zspec = lambda c: pl.BlockSpec((tm, cwid), lambda b, j: (b0 + b * nj + j, c))
    return pl.pallas_call(
        _rglru_kernel,
        grid=(nb, nj),
        in_specs=[zspec(0), zspec(1), pl.BlockSpec((tm, LANES), lambda b, j: (b0 + b * nj + j, 0)),
                  pl.BlockSpec((1, 8, cwid), lambda b, j: (b, 0, 0)), pl.BlockSpec((1, 1, cwid), lambda b, j: (b, 0, 0)),
                  _const_spec(cw.shape), _const_spec(cb.shape), _const_spec(gw.shape), _const_spec(gab.shape),
                  _const_spec(gxb.shape), _const_spec(lam.shape)],
        out_specs=[pl.BlockSpec((tm, cwid), lambda b, j: (b * nj + j, 0)),
                   pl.BlockSpec((1, 1, cwid), lambda b, j: (b, 0, 0))],
        out_shape=[jax.ShapeDtypeStruct((nb * t, cwid), BF16), jax.ShapeDtypeStruct((nb, 1, cwid), F32)],
        scratch_shapes=[pltpu.VMEM((8, cwid), F32), pltpu.VMEM((8, cwid), F32), pltpu.VMEM((tm, cwid), F32),
                        pltpu.VMEM((tm, cwid), F32)],
        compiler_params=_cparams("parallel", "arbitrary"),
        name="rglru",
    )(z, z, valid, prev0, h0, cw, cb, gw, gab, gxb, lam)


def _retention_kernel(q_ref, k_ref, v_ref, g_ref, cos_ref, sin_ref, s0_ref, gdn_ref, lg_ref,
                      o_ref, s_ref, *, blk):
    @pl.when(pl.program_id(1) == 0)
    def _():
        s_ref[...] = s0_ref[...]

    cos, sin_signed = cos_ref[...], sin_ref[...]

    def rope(x):
        return x * cos + pltpu.roll(x, D_KDIM // 2, 1) * sin_signed

    ri = lax.broadcasted_iota(jnp.int32, (blk, blk), 0)
    ci = lax.broadcasted_iota(jnp.int32, (blk, blk), 1)
    diff = (ri - ci).astype(F32)
    rows = lax.broadcasted_iota(jnp.int32, (blk, 1), 0).astype(F32)
    for hd in range(D_HEADS):
        lg = lg_ref[hd]
        qsl = slice(hd * D_KDIM, (hd + 1) * D_KDIM)
        vsl = slice(hd * D_VDIM, (hd + 1) * D_VDIM)
        q = rope(q_ref[:, qsl])
        k = rope(k_ref[:, qsl]) * (D_KDIM ** -0.5)
        decay = jnp.where(diff >= 0, jnp.exp(lg * jnp.maximum(diff, 0.0)), 0.0)
        qb, kb, vb = q.astype(BF16), k.astype(BF16), v_ref[:, vsl].astype(BF16)
        att = _dot_t(qb, kb) * decay
        o = jnp.dot(att.astype(BF16), vb, preferred_element_type=F32)
        s_prev = s_ref[0, hd]
        o = o + jnp.dot(qb, s_prev.astype(BF16), preferred_element_type=F32) * jnp.exp(lg * (rows + 1.0))
        kdec = (k * jnp.exp(lg * (blk - 1.0 - rows))).astype(BF16)
        s_ref[0, hd] = jnp.exp(lg * blk) * s_prev + lax.dot_general(kdec, vb, (((0,), (0,)), ((), ())),
                                                                    preferred_element_type=F32)
        od = _rms(o, gdn_ref[...]) * jax.nn.silu(g_ref[:, vsl])
        o_ref[:, vsl] = od.astype(BF16)


def _retention(z, cos, sin_signed, s0, gdn, log_gamma, col0, row0, nb, t, blk):
    nj = t // blk
    b0 = row0 // blk
    wid = D_HEADS * D_KDIM
    c0 = col0 // wid
    zspec = lambda c: pl.BlockSpec((blk, wid), lambda b, j: (b0 + b * nj + j, c0 + c))
    tspec = pl.BlockSpec((blk, D_KDIM), lambda b, j: (b0 + b * nj + j, 0))
    sspec = pl.BlockSpec((1, D_HEADS, D_KDIM, D_VDIM), lambda b, j: (b, 0, 0, 0))
    return pl.pallas_call(
        functools.partial(_retention_kernel, blk=blk),
        grid=(nb, nj),
        in_specs=[zspec(0), zspec(1), zspec(2), zspec(3), tspec, tspec, sspec, _const_spec(gdn.shape),
                  pl.BlockSpec(memory_space=pltpu.SMEM)],
        out_specs=[pl.BlockSpec((blk, wid), lambda b, j: (b * nj + j, 0)), sspec],
        out_shape=[jax.ShapeDtypeStruct((nb * t, wid), BF16), jax.ShapeDtypeStruct(s0.shape, F32)],
        compiler_params=_cparams("parallel", "arbitrary"),
        name="retention",
    )(z, z, z, z, cos, sin_signed, s0, gdn, log_gamma)


def _rope_tables(pos_groups, reps_groups, tail, d, width):
    inv = ROPE_THETA ** (-jnp.arange(0, d, 2, dtype=F32) / d)
    cos_rows, sin_rows = [], []
    for pos, reps in zip(pos_groups, reps_groups):
        ang = pos.astype(F32)[:, None] * inv[None, :]
        cos, sin = lax.optimization_barrier((jnp.cos(ang), jnp.sin(ang)))
        cos_rows.append(jnp.tile(jnp.tile(jnp.concatenate([cos, cos], axis=-1), (1, width // d)), (reps, 1)))
        sin_rows.append(jnp.tile(jnp.tile(jnp.concatenate([-sin, sin], axis=-1), (1, width // d)), (reps, 1)))
    cos_rows.append(jnp.ones((tail, width), F32))
    sin_rows.append(jnp.zeros((tail, width), F32))
    return jnp.concatenate(cos_rows, axis=0), jnp.concatenate(sin_rows, axis=0)


def kernel(x_prompt, x_sample, cache_a_k, cache_a_v, cache_a_kidx, cache_b_latent, cache_b_krope, state_c_conv,
           state_c_h, state_d_s, meta_tokens, norm_mix, norm_mlp, ab_w_in, ab_w_out, a_q_norm, a_k_norm,
           b_q_lat_norm, b_w_uq, b_kv_lat_norm, b_w_ukv, b_qn_norm, b_qr_norm, b_kn_norm, b_kr_norm, cd_w_in,
           cd_w_out, c_conv_w, c_conv_b, c_gate_a_w, c_gate_a_b, c_gate_x_w, c_gate_x_b, c_lambda, d_out_norm,
           mlp_w_up, mlp_w_down):
    nb_p, seq_p, d_model = x_prompt.shape
    nb_s, seq_s, _ = x_sample.shape
    past = cache_a_k.shape[2]
    n_meta = meta_tokens.shape[0]
    depth = norm_mix.shape[0]
    t_real = n_meta + seq_p
    tp = -(-t_real // ROW_BLOCK) * ROW_BLOCK
    pad = tp - t_real
    rp = nb_p * tp
    rs = nb_s * seq_s
    assert seq_s % 8 == 0 and rp % seq_s == 0 and past % 8 == 0
    assert seq_p % ROW_BLOCK == 0 and ROW_BLOCK % CHUNK == 0
    r_real = rp + rs
    tile = 512
    r_tot = -(-r_real // tile) * tile
    tm_out = math.gcd(tile, rp, rs)
    topk_p = min(TOPK_MAX, seq_p // 4)
    topk_s = min(TOPK_MAX, (past + seq_s) // 4)

    pos_p = jnp.concatenate([jnp.zeros((pad,), jnp.int32), jnp.arange(t_real, dtype=jnp.int32)])
    ck_p = jnp.concatenate([jnp.zeros((n_meta,), jnp.int32), 1 + jnp.arange(seq_p, dtype=jnp.int32) // CHUNK])
    cq_p = jnp.concatenate([jnp.full((pad,), -1, jnp.int32), ck_p])
    ckk_p = jnp.concatenate([jnp.full((pad,), 2 ** 30, jnp.int32), ck_p])[None, :]
    pos_s_all = jnp.arange(past + seq_s, dtype=jnp.int32)
    ck_s = pos_s_all // CHUNK
    pos_s = pos_s_all[past:]
    tail = r_tot - r_real
    cq_rows =jnp.concatenate([jnp.tile(cq_p, nb_p), jnp.tile(ck_s[past:], nb_s), jnp.full((tail,), -1, jnp.int32)])
    cq_rows = jnp.broadcast_to(cq_rows[:, None], (r_tot, LANES))
    tabs128 = _rope_tables((pos_p, pos_s), (nb_p, nb_s), tail, 128, 128)
    tabs64 = _rope_tables((pos_p, pos_s), (nb_p, nb_s), tail, 64, 128)
    tabs256 = _rope_tables((pos_p, pos_s), (nb_p, nb_s), tail, 256, 256)

    hp = jnp.concatenate([jnp.zeros((nb_p, pad, d_model), F32),
                          jnp.broadcast_to(meta_tokens[None].astype(F32), (nb_p, n_meta, d_model)),
                          x_prompt], axis=1).reshape(rp, d_model)
    h = jnp.concatenate([hp, x_sample.reshape(rs, d_model), jnp.zeros((tail, d_model), F32)], axis=0)

    def prompt_rows(x):
        return x[:rp].reshape(nb_p, tp, -1)[:, pad:]

    def sample_rows(x):
        return x[rp:rp + rs].reshape(nb_s, seq_s, -1)

    ab_p, ab_s, cd_p, cd_s = [], [], [], []
    for layer in range(depth):
        i = layer // 2
        gmix = norm_mix[layer][None, :]
        if layer % 2 == 0:
            w = ab_w_in[i]
            offs = np.cumsum([0, 1024, 256, 256, 1024, 64, 16, 512, 256, 64])
            sec = lambda n: w[:, offs[n]:offs[n + 1]]
            win = jnp.concatenate([sec(0), sec(1), sec(2), sec(3), sec(6), sec(7), sec(4), sec(8), sec(5),
                                   jnp.zeros((d_model, AB_IN_PAD - 3472), F32)], axis=1).astype(BF16)
            wuq = b_w_uq[i].reshape(-1, B_HEADS, B_NOPE + B_ROPE)
            wuq = jnp.concatenate([wuq[:, :, :B_NOPE].reshape(-1, B_HEADS * B_NOPE),
                                   wuq[:, :, B_NOPE:].reshape(-1, B_HEADS * B_ROPE)], axis=1).astype(BF16)
            wukv = b_w_ukv[i].reshape(-1, B_HEADS, B_NOPE + B_V)
            wukv = jnp.concatenate([wukv[:, :, :B_NOPE].reshape(-1, B_HEADS * B_NOPE),
                                    wukv[:, :, B_NOPE:].reshape(-1, B_HEADS * B_V)], axis=1).astype(BF16)
            two = lambda g: jnp.concatenate([g, g])[None, :]
            gkr2 = jnp.concatenate([jnp.ones((64,), F32), b_kr_norm[i]])[None, :]
            (q, kf, vf, kb, vb_a, qi, kikr, kikrb, wi, qn, qr, lat) = _ab_proj(
                h, gmix, win, wuq, a_q_norm[i][None, :], a_k_norm[i][None, :], b_q_lat_norm[i][None, :],
                b_kv_lat_norm[i][None, :], b_qn_norm[i][None, :], two(b_qr_norm[i]), gkr2,
                tabs128 + tabs64, 256)
            gkn = b_kn_norm[i][None, :]
            kn, vbb = _ukv(lat, wukv, gkn, 512)
            pkn, pvb = _ukv(cache_b_latent[i].reshape(nb_s * past, -1), wukv, gkn, 512)
            oa_p, ob_p = _attn_prompt(qi, wi, q, qn, qr, cq_rows, kikrb, kb, vb_a, kn, vbb, ckk_p, nb_p, tp, topk_p, 4)
            oa_s, ob_s = _attn_sample(qi, wi, q, qn, qr, cq_rows, kikrb, kb, vb_a, kn, vbb,
                                      cache_a_kidx[i], cache_a_k[i].reshape(nb_s, past, -1),
                                      cache_a_v[i].reshape(nb_s, past, -1), pkn, cache_b_krope[i], pvb,
                                      ck_s[None, :past], ck_s[None, past:], rp, nb_s, seq_s, topk_s)
            h = _out_proj(h, oa_p, ob_p, oa_s, ob_s, ab_w_out[i].astype(BF16), tm_out)
            kv4 = lambda x: x.reshape(x.shape[0], x.shape[1], A_KV_HEADS, A_HEAD_DIM)
            ab_p.append((kv4(prompt_rows(kf)), kv4(prompt_rows(vf)), prompt_rows(kikr)[..., :64],
                         prompt_rows(lat), prompt_rows(kikr)[..., 64:]))
            ab_s.append((kv4(sample_rows(kf)), kv4(sample_rows(vf)), sample_rows(kikr)[..., :64],
                         sample_rows(lat), sample_rows(kikr)[..., 64:]))
        else:
            z = _cd_proj(h, gmix, cd_w_in[i].astype(BF16), 1024 if r_tot % 1024 == 0 else tile, 1024)
            cwid = C_BLOCKS * C_BLOCK
            gw = jnp.concatenate([c_gate_a_w[i], c_gate_x_w[i]], axis=-1).astype(BF16)
            cargs = (c_conv_w[i], c_conv_b[i][None, :], gw, c_gate_a_b[i][None, :], c_gate_x_b[i][None, :],
                     c_lambda[i][None, :])
            oc_p, hl_p = _rglru(z, cq_rows, jnp.zeros((nb_p, 8, cwid), F32), jnp.zeros((nb_p, 1, cwid), F32),
                                *cargs, 0, nb_p, tp, ROW_BLOCK)
            prev_s = jnp.concatenate([jnp.zeros((nb_s, 8 - (CONV_W - 1), cwid), F32), state_c_conv[i]], axis=1)
            oc_s, hl_s = _rglru(z, cq_rows, prev_s, state_c_h[i][:, None, :], *cargs, rp, nb_s, seq_s, seq_s)
            hl_p, hl_s = hl_p[:, 0], hl_s[:, 0]
            log_gamma = jnp.log(1.0 - 2.0 ** (-5.0 - jnp.arange(D_HEADS, dtype=F32)))
            gdn = d_out_norm[i][None, :]
            od_p, ds_p = _retention(z, *tabs256, jnp.zeros((nb_p, D_HEADS, D_KDIM, D_VDIM), F32), gdn, log_gamma,
                                    2 * cwid, 0, nb_p, tp, ROW_BLOCK)
            od_s, ds_s = _retention(z, *tabs256, state_d_s[i], gdn, log_gamma, 2 * cwid, rp, nb_s, seq_s, seq_s)
            h = _out_proj(h, oc_p, od_p, oc_s, od_s, cd_w_out[i].astype(BF16), tm_out)
            nc = CONV_W - 1
            cx_tail_p = z[:rp].reshape(nb_p, tp, -1)[:, tp - nc:, :cwid]
            cx_tail_s = z[rp:rp + rs].reshape(nb_s, seq_s, -1)[:, max(seq_s - nc, 0):, :cwid]
            cd_p.append((cx_tail_p, hl_p, ds_p))
            cd_s.append((jnp.concatenate([state_c_conv[i], cx_tail_s], axis=1)[:, -nc:], hl_s, ds_s))
        h = _mlp(h, norm_mlp[layer][None, :], mlp_w_up[layer].astype(BF16), mlp_w_down[layer].astype(BF16),
                 512, 512)

    def stack(entries, j):
        return jnp.stack([e[j] for e in entries])

    y_p = prompt_rows(h)[:, n_meta:]
    y_s = sample_rows(h)
    return (y_p, y_s,
            stack(ab_p, 0), stack(ab_p, 1), stack(ab_p, 2), stack(ab_p, 3), stack(ab_p, 4),
            stack(cd_p, 0), stack(cd_p, 1), stack(cd_p, 2),
            stack(ab_s, 0), stack(ab_s, 1), stack(ab_s, 2), stack(ab_s, 3), stack(ab_s, 4),
            stack(cd_s, 0), stack(cd_s, 1), stack(cd_s, 2))
```

```python
import functools
import math

import jax
import jax.numpy as jnp
import numpy as np
from jax import lax
from jax.experimental import pallas as pl
from jax.experimental.pallas import tpu as pltpu

F32 = jnp.float32
BF16 = jnp.bfloat16

CHUNK = 64
ROPE_THETA = 10000.0
EPS = 1e-6
A_HEADS, A_KV_HEADS, A_HEAD_DIM = 8, 2, 128
IDX_HEADS, IDX_DIM = 16, 64
TOPK_MAX = 256
B_HEADS, B_NOPE, B_ROPE, B_V = 8, 128, 64, 128
C_BLOCKS, C_BLOCK, CONV_W = 8, 128, 4
RG_C = 8.0
D_HEADS, D_KDIM, D_VDIM = 4, 256, 256

LANES = 128
ROW_BLOCK = 128
NEG = -1e30
INT_MIN = -2 ** 31
VMEM_LIMIT = 56 * 1024 * 1024


def _cparams(*sem):
    return pltpu.CompilerParams(dimension_semantics=sem, vmem_limit_bytes=VMEM_LIMIT)


def _const_spec(shape):
    nd = len(shape)
    return pl.BlockSpec(shape, lambda *_: (0,) * nd)


def _rms(x, g):
    ms = jnp.mean(x * x, axis=-1, keepdims=True)
    return x * lax.rsqrt(ms + EPS) * g


def _rms_half(x, g):
    lane = lax.broadcasted_iota(jnp.int32, x.shape, 1)
    lo = lane < 64
    xx = x * x
    s_lo = jnp.sum(jnp.where(lo, xx, 0.0), axis=-1, keepdims=True)
    s_hi = jnp.sum(jnp.where(lo, 0.0, xx), axis=-1, keepdims=True)
    ms = jnp.where(lo, s_lo, s_hi) * (1.0 / 64.0)
    return x * lax.rsqrt(ms + EPS) * g


def _rope128(x, cos, sin_signed):
    return x * cos + pltpu.roll(x, 64, 1) * sin_signed


def _rope64(x, cos, sin_signed):
    lane = lax.broadcasted_iota(jnp.int32, x.shape, 1)
    first = (lane % 64) < 32
    rot = jnp.where(first, pltpu.roll(x, 96, 1), pltpu.roll(x, 32, 1))
    return x * cos + rot * sin_signed


AB_COLS = dict(qa=(0, 1024), ka=(1024, 1280), va=(1280, 1536), qi=(1536, 2560), cq=(2560, 3072),
               ckv=(3072, 3328), kikr=(3328, 3456), wi=(3456, 3584))
AB_IN_PAD = 3584


def _ab_proj_kernel(h_ref, gmix_ref, win_ref, wuq_ref, gaq_ref, gak_ref, gqlat_ref, gkvlat_ref, gqn_ref,
                    gqr_ref, gkr_ref, c128_ref, s128_ref, c64_ref, s64_ref,
                    q_ref, kf_ref, vf_ref, kb_ref, vb_ref, qi_ref, kikr_ref, kikrb_ref, wi_ref, qn_ref, qr_ref,
                    lat_ref):
    xn = _rms(h_ref[...], gmix_ref[...]).astype(BF16)
    c128, s128, c64, s64 = c128_ref[...], s128_ref[...], c64_ref[...], s64_ref[...]

    def proj(name):
        a, b = AB_COLS[name]
        return jnp.dot(xn, win_ref[:, a:b], preferred_element_type=F32)

    z = proj('qa')
    qscale = A_HEAD_DIM ** -0.5
    for h in range(A_HEADS):
        x = _rope128(_rms(z[:, h * 128:(h + 1) * 128], gaq_ref[...]), c128, s128)
        q_ref[:, h * 128:(h + 1) * 128] = (x * qscale).astype(BF16)
    z = proj('ka')
    for h in range(A_KV_HEADS):
        x = _rope128(_rms(z[:, h * 128:(h + 1) * 128], gak_ref[...]), c128, s128)
        kf_ref[:, h * 128:(h + 1) * 128] = x
        kb_ref[:, h * 128:(h + 1) * 128] = x.astype(BF16)
    z = proj('va')
    vf_ref[...] = z
    vb_ref[...] = z.astype(BF16)
    z = proj('qi')
    for p in range(IDX_HEADS // 2):
        x = _rope64(z[:, p * 128:(p + 1) * 128], c64, s64)
        qi_ref[:, p * 128:(p + 1) * 128] = (x * (IDX_DIM ** -0.5)).astype(BF16)
    z = proj('kikr')
    lane = lax.broadcasted_iota(jnp.int32, z.shape, 1)
    x = jnp.where(lane < 64, z, _rms_half(z, gkr_ref[...]))
    x = _rope64(x, c64, s64)
    kikr_ref[...] = x
    kikrb_ref[...] = x.astype(BF16)
    wi_ref[...] = proj('wi') * (IDX_HEADS ** -0.5)
    lat = _rms(proj('ckv'), gkvlat_ref[...])
    lat_ref[...] = lat
    cq = _rms(proj('cq'), gqlat_ref[...]).astype(BF16)
    bscale = (B_NOPE + B_ROPE) ** -0.5
    zq = jnp.dot(cq, wuq_ref[:, :B_HEADS * B_NOPE], preferred_element_type=F32)
    for h in range(B_HEADS):
        x = _rms(zq[:, h * 128:(h + 1) * 128], gqn_ref[...])
        qn_ref[:, h * 128:(h + 1) * 128] = (x * bscale).astype(BF16)
    zq = jnp.dot(cq, wuq_ref[:, B_HEADS * B_NOPE:], preferred_element_type=F32)
    for p in range(B_HEADS // 2):
        x = _rope64(_rms_half(zq[:, p * 128:(p + 1) * 128], gqr_ref[...]), c64, s64)
        qr_ref[:, p * 128:(p + 1) * 128] = (x * bscale).astype(BF16)


def _ab_proj(h, gmix, win, wuq, gaq, gak, gqlat, gkvlat, gqn, gqr2, gkr2, tabs, tm):
    r, d = h.shape
    row = lambda w: pl.BlockSpec((tm, w), lambda i: (i, 0))
    outs = [(1024, BF16), (256, F32), (256, F32), (256, BF16), (256, BF16), (1024, BF16), (128, F32),
            (128, BF16), (128, F32), (1024, BF16), (512, BF16), (256, F32)]
    return pl.pallas_call(
        _ab_proj_kernel,
        grid=(r // tm,),
        in_specs=[row(d), _const_spec(gmix.shape), _const_spec(win.shape), _const_spec(wuq.shape),
                  _const_spec(gaq.shape), _const_spec(gak.shape), _const_spec(gqlat.shape),
                  _const_spec(gkvlat.shape), _const_spec(gqn.shape), _const_spec(gqr2.shape),
                  _const_spec(gkr2.shape), row(128), row(128), row(128), row(128)],
        out_specs=[row(w) for w, _ in outs],
        out_shape=[jax.ShapeDtypeStruct((r, w), dt) for w, dt in outs],
        compiler_params=_cparams("parallel"),
        name="ab_proj",
    )(h, gmix, win, wuq, gaq, gak, gqlat, gkvlat, gqn, gqr2, gkr2, *tabs)


def _ukv_kernel(lat_ref, w_ref, gkn_ref, kn_ref, vb_ref):
    lat = lat_ref[...].astype(BF16)
    z = jnp.dot(lat, w_ref[:, :B_HEADS * B_NOPE], preferred_element_type=F32)
    for h in range(B_HEADS):
        kn_ref[:, h * 128:(h + 1) * 128] = _rms(z[:, h * 128:(h + 1) * 128], gkn_ref[...]).astype(BF16)
    vb_ref[...] = jnp.dot(lat, w_ref[:, B_HEADS * B_NOPE:], preferred_element_type=F32).astype(BF16)


def _ukv(lat, w, gkn, tm):
    n = lat.shape[0]
    row = lambda wd: pl.BlockSpec((tm, wd), lambda i: (i, 0))
    return pl.pallas_call(
        _ukv_kernel,
        grid=(n // tm,),
        in_specs=[row(lat.shape[1]), _const_spec(w.shape), _const_spec(gkn.shape)],
        out_specs=[row(1024), row(1024)],
        out_shape=[jax.ShapeDtypeStruct((n, 1024), BF16)] * 2,
        compiler_params=_cparams("parallel"),
        name="mla_ukv",
    )(lat, w, gkn)


def _dot_t(a, b):
    return lax.dot_general(a, b, (((1,), (1,)), ((), ())), preferred_element_type=F32)


def _float_key(x):
    bits = pltpu.bitcast(x, jnp.int32)
    return bits ^ ((bits >> 31) & jnp.int32(0x7FFFFFFF))


def _kth_largest(keys, k):
    rows = keys[0].shape[0]

    def body(i, t):
        bit = jnp.int32(31) - i
        cand = t + jnp.left_shift(jnp.int32(1), bit)
        cnt = jnp.zeros((rows, 1), F32)
        for key in keys:
            cnt = cnt + jnp.sum(jnp.where(key >= cand, 1.0, 0.0), axis=-1, keepdims=True)
        return jnp.where(cnt >= float(k), cand, t)

    return lax.fori_loop(0, 32, body, jnp.full((rows, 1), INT_MIN, jnp.int32))


def _kv_head(x, c):
    return x[c] if isinstance(x, tuple) else x[:, c * 128:(c + 1) * 128]


def _dsa_core(qi, wi, q, cq, pieces, topk):
    nq = qi.shape[0]
    keys, adms = [], []
    for ki, _, _, ck in pieces:
        score = jnp.zeros((nq, ki.shape[0]), F32)
        for h in range(IDX_HEADS):
            s_h = _dot_t(qi[:, h * 64:(h + 1) * 64], ki)
            score = score + jnp.maximum(s_h, 0.0) * wi[:, h:h + 1]
        adm = ck <= cq
        keys.append(jnp.where(adm, _float_key(score), INT_MIN))
        adms.append(adm)
    thr = _kth_largest(keys, topk)
    group = A_HEADS // A_KV_HEADS
    biases = [jnp.concatenate([jnp.where(jnp.logical_and(adm, key >= thr), 0.0, NEG)] * group, axis=0)
              for key, adm in zip(keys, adms)]
    outs = []
    for c in range(A_KV_HEADS):
        qg = jnp.concatenate([q[:, (c * group + g) * 128:(c * group + g + 1) * 128] for g in range(group)], axis=0)
        ss = [_dot_t(qg, _kv_head(k, c)) + bias for (_, k, _, _), bias in zip(pieces, biases)]
        m = functools.reduce(jnp.maximum, [jnp.max(s, axis=-1, keepdims=True) for s in ss])
        ps = [jnp.exp(s - m) for s in ss]
        den = functools.reduce(jnp.add, [jnp.sum(p, axis=-1, keepdims=True) for p in ps])
        o = functools.reduce(jnp.add, [jnp.dot(p.astype(BF16), _kv_head(v, c), preferred_element_type=F32)
                                       for p, (_, _, v, _) in zip(ps, pieces)])
        o = o / den
        outs.extend([o[g * nq:(g + 1) * nq] for g in range(group)])
    return jnp.concatenate(outs, axis=-1)


def _mla_core(qn, qr, cq, pieces):
    outs = []
    biases = [jnp.where(ck <= cq, 0.0, NEG) for _, _, _, ck in pieces]
    for h in range(B_HEADS):
        ss = []
        for (kn, kr, _, _), bias in zip(pieces, biases):
            s = _dot_t(qn[:, h * 128:(h + 1) * 128], kn[:, h * 128:(h + 1) * 128])
            s = s + _dot_t(qr[:, h * 64:(h + 1) * 64], kr)
            ss.append(s + bias)
        m = functools.reduce(jnp.maximum, [jnp.max(s, axis=-1, keepdims=True) for s in ss])
        ps = [jnp.exp(s - m) for s in ss]
        den = functools.reduce(jnp.add, [jnp.sum(p, axis=-1, keepdims=True) for p in ps])
        o = functools.reduce(jnp.add, [jnp.dot(p.astype(BF16), vb[:, h * 128:(h + 1) * 128],
                                               preferred_element_type=F32)
                                       for p, (_, _, vb, _) in zip(ps, pieces)])
        outs.append(o / den)
    return jnp.concatenate(outs, axis=-1)


def _attn_prompt_kernel(qi_ref, wi_ref, q_ref, qn_ref, qr_ref, cq_ref, kikr_ref, k_ref, v_ref, kn_ref, vb_ref,
                        ck_ref, *rest, topk, s_len):
    oa_ref, ob_ref = rest[-2:]
    cq = cq_ref[:, 0:1]
    valid = cq >= 0
    ck = ck_ref[:, :s_len]
    kikr = kikr_ref[:s_len, :]
    oa = _dsa_core(qi_ref[...], wi_ref[...], q_ref[...], cq,
                   [(kikr[:, :64], k_ref[:s_len, :], v_ref[:s_len, :], ck)], topk)
    oa_ref[...] = jnp.where(valid, oa, 0.0).astype(BF16)
    ob = _mla_core(qn_ref[...], qr_ref[...], cq, [(kn_ref[:s_len, :], kikr[:, 64:], vb_ref[:s_len, :], ck)])
    ob_ref[...] = jnp.where(valid, ob, 0.0).astype(BF16)


def _attn_prompt(qi, wi, q, qn, qr, cq, kikr, k, v, kn, vb, ck, nb, tp, topk, n_ranges):
    tq = ROW_BLOCK
    nj = tp // tq
    edges = sorted({-(-nj * r // n_ranges) for r in range(n_ranges + 1)})
    kspec = lambda w: pl.BlockSpec((tp, w), lambda b, j: (b, 0))
    outs = ()
    for j0, j1 in zip(edges[:-1], edges[1:]):
        qspec = lambda w, j0=j0: pl.BlockSpec((tq, w), lambda b, j: (b * nj + j0 + j, 0))
        n_in = 12
        outs = pl.pallas_call(
            functools.partial(_attn_prompt_kernel, topk=topk, s_len=j1 * tq),
            grid=(nb, j1 - j0),
            in_specs=[qspec(1024), qspec(128), qspec(1024), qspec(1024), qspec(512), qspec(128),
                      kspec(128), kspec(256), kspec(256), kspec(1024), kspec(1024), _const_spec(ck.shape)]
                     + [pl.BlockSpec(memory_space=pl.ANY)] * len(outs),
            out_specs=[qspec(1024), qspec(1024)],
            out_shape=[jax.ShapeDtypeStruct((nb * tp, 1024), BF16)] * 2,
            input_output_aliases={n_in + o: o for o in range(len(outs))},
            compiler_params=_cparams("parallel", "arbitrary"),
            name="attn_prompt",
        )(qi, wi, q, qn, qr, cq, kikr, k, v, kn, vb, ck, *outs)
    return outs


def _attn_sample_kernel(qi_ref, wi_ref, q_ref, qn_ref, qr_ref, cq_ref, kikr_ref, k_ref, v_ref, kn_ref, vb_ref,
                        pki_ref, pk_ref, pv_ref, pkn_ref, pkr_ref, pvb_ref, ckp_ref, ckn_ref,
                        oa_ref, ob_ref, *, topk):
    cq = cq_ref[:, 0:1]
    ckp, ckn = ckp_ref[...], ckn_ref[...]
    kikr = kikr_ref[...]
    heads = lambda ref: tuple(ref[0, :, c, :].astype(BF16) for c in range(A_KV_HEADS))
    past_a = (pki_ref[0].astype(BF16), heads(pk_ref), heads(pv_ref), ckp)
    new_a = (kikr[:, :64], k_ref[...], v_ref[...], ckn)
    oa_ref[...] = _dsa_core(qi_ref[...], wi_ref[...], q_ref[...], cq, [past_a, new_a], topk).astype(BF16)
    past_b = (pkn_ref[...], pkr_ref[0].astype(BF16), pvb_ref[...], ckp)
    new_b = (kn_ref[...], kikr[:, 64:], vb_ref[...], ckn)
    ob_ref[...] = _mla_core(qn_ref[...], qr_ref[...], cq, [past_b, new_b]).astype(BF16)


def _attn_sample(qi, wi, q, qn, qr, cq, kikr, k, v, kn, vb, pki, pk, pv, pkn, pkr, pvb, ckp, ckn,
                 row0, nb, ts, topk):
    past = pk.shape[1]
    blk0 = row0 // ts
    nspec = lambda w: pl.BlockSpec((ts, w), lambda b: (blk0 + b, 0))
    pspec = lambda w: pl.BlockSpec((1, past, w), lambda b: (b, 0, 0))
    p2spec = lambda w: pl.BlockSpec((past, w), lambda b: (b, 0))
    kvspec = pl.BlockSpec((1, past, A_KV_HEADS, A_HEAD_DIM), lambda b: (b, 0, 0, 0))
    return pl.pallas_call(
        functools.partial(_attn_sample_kernel, topk=topk),
        grid=(nb,),
        in_specs=[nspec(1024), nspec(128), nspec(1024), nspec(1024), nspec(512), nspec(128),
                  nspec(128), nspec(256), nspec(256), nspec(1024), nspec(1024),
                  pspec(64), kvspec, kvspec, p2spec(1024), pspec(64), p2spec(1024),
                  _const_spec(ckp.shape), _const_spec(ckn.shape)],
        out_specs=[pl.BlockSpec((ts, 1024), lambda b: (b, 0))] * 2,
        out_shape=[jax.ShapeDtypeStruct((nb * ts, 1024), BF16)] * 2,
        compiler_params=_cparams("parallel"),
        name="attn_sample",
    )(qi, wi, q, qn, qr, cq, kikr, k, v, kn, vb, pki, pk, pv, pkn, pkr, pvb, ckp, ckn)


def _out_proj_kernel(h_ref, oap_ref, obp_ref, oas_ref, obs_ref, w_ref, o_ref, *, n_p, n_ps):
    half = oap_ref.shape[1]
    i = pl.program_id(0)

    def project(oa_ref, ob_ref):
        y = jnp.dot(oa_ref[...], w_ref[:half, :], preferred_element_type=F32)
        y = y + jnp.dot(ob_ref[...], w_ref[half:, :], preferred_element_type=F32)
        o_ref[...] = h_ref[...] + y

    pl.when(i < n_p)(functools.partial(project, oap_ref, obp_ref))
    pl.when(jnp.logical_and(i >= n_p, i < n_ps))(functools.partial(project, oas_ref, obs_ref))

    @pl.when(i >= n_ps)
    def _():
        o_ref[...] = h_ref[...]


def _out_proj(h, oa_p, ob_p, oa_s, ob_s, w, tm):
    r, d = h.shape
    n_p, n_s = oa_p.shape[0] // tm, oa_s.shape[0] // tm
    assert oa_p.shape[0] % tm == 0 and oa_s.shape[0] % tm == 0
    row = lambda wd: pl.BlockSpec((tm, wd), lambda i: (i, 0))
    pspec = lambda wd: pl.BlockSpec((tm, wd), lambda i: (jnp.minimum(i, n_p - 1), 0))
    sspec = lambda wd: pl.BlockSpec((tm, wd), lambda i: (jnp.clip(i - n_p, 0, n_s - 1), 0))
    return pl.pallas_call(
        functools.partial(_out_proj_kernel, n_p=n_p, n_ps=n_p + n_s),
        grid=(r // tm,),
        in_specs=[row(d), pspec(oa_p.shape[1]), pspec(ob_p.shape[1]), sspec(oa_s.shape[1]), sspec(ob_s.shape[1]),
                  _const_spec(w.shape)],
        out_specs=row(d),
        out_shape=jax.ShapeDtypeStruct((r, d), F32),
        compiler_params=_cparams("parallel"),
        name="out_proj",
    )(h, oa_p, ob_p, oa_s, ob_s, w)


def _mlp_kernel(h_ref, g_ref, wup_ref, wdn_ref, *rest, n_split, n_end):
    *o_refs, xn_ref, acc_ref = rest
    i, f = pl.program_id(0), pl.program_id(1)

    @pl.when(f == 0)
    def _():
        x = h_ref[...]
        xn_ref[...] = _rms(x, g_ref[...]).astype(BF16)
        acc_ref[...] = x

    u = jnp.dot(xn_ref[...], wup_ref[...], preferred_element_type=F32)
    u = jnp.maximum(u, 0.0)
    acc_ref[...] += jnp.dot((u * u).astype(BF16), wdn_ref[...], preferred_element_type=F32)

    @pl.when(f == pl.num_programs(1) - 1)
    def _():
        if len(o_refs) == 1:
            o_refs[0][...] = acc_ref[...]
        else:
            @pl.when(i < n_split)
            def _():
                o_refs[0][...] = acc_ref[...]

            @pl.when(jnp.logical_and(i >= n_split, i < n_end))
            def _():
                o_refs[1][...] = acc_ref[...]


def _mlp(h, g, wup, wdn, layer, tm, tf, split_rows=None):
    r, d = h.shape
    dff = wup.shape[2]
    n_end = r // tm
    if split_rows is None:
        n_split = 0
        out_specs = [pl.BlockSpec((tm, d), lambda i, f: (i, 0))]
        out_shape = [jax.ShapeDtypeStruct((r, d), F32)]
    else:
        a, b = split_rows
        assert a % tm == 0 and b % tm == 0
        n_split, n_b = a // tm, b // tm
        n_end = n_split + n_b
        out_specs = [pl.BlockSpec((tm, d), lambda i, f: (jnp.minimum(i, n_split - 1), 0)),
                     pl.BlockSpec((tm, d), lambda i, f: (jnp.clip(i - n_split, 0, n_b - 1), 0))]
        out_shape = [jax.ShapeDtypeStruct((a, d), F32), jax.ShapeDtypeStruct((b, d), F32)]
    res = pl.pallas_call(
        functools.partial(_mlp_kernel, n_split=n_split, n_end=n_end),
        grid=(r // tm, dff // tf),
        in_specs=[pl.BlockSpec((tm, d), lambda i, f: (i, 0)), pl.BlockSpec((1, d), lambda i, f: (0, 0)),
                  pl.BlockSpec((None, d, tf), lambda i, f: (layer, 0, f)),
                  pl.BlockSpec((None, tf, d), lambda i, f: (layer, f, 0))],
        out_specs=out_specs,
        out_shape=out_shape,
        scratch_shapes=[pltpu.VMEM((tm, d), BF16), pltpu.VMEM((tm, d), F32)],
        compiler_params=_cparams("parallel" if split_rows is None else "arbitrary", "arbitrary"),
        name="mlp",
    )(h, g, wup, wdn)
    return res[0] if split_rows is None else res


def _cd_proj_kernel(h_ref, g_ref, w_ref, z_ref, xn_ref):
    @pl.when(pl.program_id(1) == 0)
    def _():
        xn_ref[...] = _rms(h_ref[...], g_ref[...]).astype(BF16)

    z_ref[...] = jnp.dot(xn_ref[...], w_ref[...], preferred_element_type=F32)


def _cd_proj(h, g, w, tm, tn):
    r, d = h.shape
    n = w.shape[1]
    return pl.pallas_call(
        _cd_proj_kernel,
        grid=(r // tm, n // tn),
        in_specs=[pl.BlockSpec((tm, d), lambda i, j: (i, 0)), pl.BlockSpec((1, d), lambda i, j: (0, 0)),
                  pl.BlockSpec((d, tn), lambda i, j: (0, j))],
        out_specs=pl.BlockSpec((tm, tn), lambda i, j: (i, j)),
        out_shape=jax.ShapeDtypeStruct((r, n), F32),
        scratch_shapes=[pltpu.VMEM((tm, d), BF16)],
        compiler_params=_cparams("parallel", "arbitrary"),
        name="cd_proj",
    )(h, g, w)


def _rglru_kernel(cx_ref, cg_ref, valid_ref, prev0_ref, h0_ref, cw_ref, cb_ref, gw_ref, gab_ref, gxb_ref, lam_ref,
                  o_ref, hl_ref, prev_s, h_s, a_s, b_s):
    @pl.when(pl.program_id(1) == 0)
    def _():
        prev_s[...] = prev0_ref[0]
        h_s[...] = jnp.broadcast_to(h0_ref[0], h_s.shape)

    cx = cx_ref[...]
    tm = cx.shape[0]
    full = jnp.concatenate([prev_s[...], cx], axis=0)
    prev_s[...] = cx[tm - 8:, :]
    xc = cb_ref[...] + cx * cw_ref[CONV_W - 1:CONV_W, :]
    for j in range(CONV_W - 1):
        sh = CONV_W - 1 - j
        xc = xc + full[8 - sh:8 - sh + tm, :] * cw_ref[j:j + 1, :]
    valid = valid_ref[:, 0:1] >= 0
    softplus = jnp.logaddexp(-lam_ref[...], 0.0)
    for n in range(C_BLOCKS):
        sl = slice(n * C_BLOCK, (n + 1) * C_BLOCK)
        xb = xc[:, sl]
        gates = jnp.dot(xb.astype(BF16), gw_ref[n], preferred_element_type=F32)
        r = jax.nn.sigmoid(gates[:, :C_BLOCK] + gab_ref[:, sl])
        ig = jax.nn.sigmoid(gates[:, C_BLOCK:] + gxb_ref[:, sl])
        log_a = -RG_C * r * softplus[:, sl]
        a = jnp.exp(log_a)
        a_s[:, sl] = a
        b = jnp.sqrt(1.0 - a * a) * (ig * xb)
        b_s[:, sl] = jnp.where(valid, b, 0.0)

    rid = lax.broadcasted_iota(jnp.int32, h_s.shape, 0)

    def group(g, h):
        rows = pl.ds(pl.multiple_of(g * 8, 8), 8)
        ca, cb = a_s[rows, :], b_s[rows, :]
        for s in (1, 2, 4):
            m = rid >= s
            cb = jnp.where(m, ca * pltpu.roll(cb, s, 0) + cb, cb)
            ca = jnp.where(m, ca * pltpu.roll(ca, s, 0), ca)
        hh = ca * h + cb
        b_s[rows, :] = hh
        return jnp.broadcast_to(hh[7:8, :], hh.shape)

    h = lax.fori_loop(0, tm // 8, group, h_s[...])
    h_s[...] = h
    hl_ref[0] = h[0:1, :]
    o_ref[...] = (b_s[...] * jax.nn.gelu(cg_ref[...])).astype(BF16)


def _rglru(z, valid, prev0, h0, cw, cb, gw, gab, gxb, lam, row0, nb, t, tm):
    cwid = C_BLOCKS * C_BLOCK
    nj = t // tm
    b0 = row0 // tm
    zspec = lambda c: pl.BlockSpec((tm, cwid), lambda b, j: (b0 + b * nj + j, c))
    return pl.pallas_call(
        _rglru_kernel,
        grid=(nb, nj),
        in_specs=[zspec(0), zspec(1), pl.BlockSpec((tm, LANES), lambda b, j: (b0 + b * nj + j, 0)),
                  pl.BlockSpec((1, 8, cwid), lambda b, j: (b, 0, 0)), pl.BlockSpec((1, 1, cwid), lambda b, j: (b, 0, 0)),
                  _const_spec(cw.shape), _const_spec(cb.shape), _const_spec(gw.shape), _const_spec(gab.shape),
                  _const_spec(gxb.shape), _const_spec(lam.shape)],
        out_specs=[pl.BlockSpec((tm, cwid), lambda b, j: (b * nj + j, 0)),
                   pl.BlockSpec((1, 1, cwid), lambda b, j: (b, 0, 0))],
        out_shape=[jax.ShapeDtypeStruct((nb * t, cwid), BF16), jax.ShapeDtypeStruct((nb, 1, cwid), F32)],
        scratch_shapes=[pltpu.VMEM((8, cwid), F32), pltpu.VMEM((8, cwid), F32), pltpu.VMEM((tm, cwid), F32),
                        pltpu.VMEM((tm, cwid), F32)],
        compiler_params=_cparams("parallel", "arbitrary"),
        name="rglru",
    )(z, z, valid, prev0, h0, cw, cb, gw, gab, gxb, lam)


def _retention_kernel(q_ref, k_ref, v_ref, g_ref, cos_ref, sin_ref, s0_ref, gdn_ref, lg_ref,
                      o_ref, s_ref, *, blk):
    @pl.when(pl.program_id(1) == 0)
    def _():
        s_ref[...] = s0_ref[...]

    cos, sin_signed = cos_ref[...], sin_ref[...]

    def rope(x):
        return x * cos + pltpu.roll(x, D_KDIM // 2, 1) * sin_signed

    ri = lax.broadcasted_iota(jnp.int32, (blk, blk), 0)
    ci = lax.broadcasted_iota(jnp.int32, (blk, blk), 1)
    diff = (ri - ci).astype(F32)
    rows = lax.broadcasted_iota(jnp.int32, (blk, 1), 0).astype(F32)
    for hd in range(D_HEADS):
        lg = lg_ref[hd]
        qsl = slice(hd * D_KDIM, (hd + 1) * D_KDIM)
        vsl = slice(hd * D_VDIM, (hd + 1) * D_VDIM)
        q = rope(q_ref[:, qsl])
        k = rope(k_ref[:, qsl]) * (D_KDIM ** -0.5)
        decay = jnp.where(diff >= 0, jnp.exp(lg * jnp.maximum(diff, 0.0)), 0.0)
        qb, kb, vb = q.astype(BF16), k.astype(BF16), v_ref[:, vsl].astype(BF16)
        att = _dot_t(qb, kb) * decay
        o = jnp.dot(att.astype(BF16), vb, preferred_element_type=F32)
        s_prev = s_ref[0, hd]
        o = o + jnp.dot(qb, s_prev.astype(BF16), preferred_element_type=F32) * jnp.exp(lg * (rows + 1.0))
        kdec = (k * jnp.exp(lg * (blk - 1.0 - rows))).astype(BF16)
        s_ref[0, hd] = jnp.exp(lg * blk) * s_prev + lax.dot_general(kdec, vb, (((0,), (0,)), ((), ())),
                                                                    preferred_element_type=F32)
        od = _rms(o, gdn_ref[...]) * jax.nn.silu(g_ref[:, vsl])
        o_ref[:, vsl] = od.astype(BF16)


def _retention(z, cos, sin_signed, s0, gdn, log_gamma, col0, row0, nb, t, blk):
    nj = t // blk
    b0 = row0 // blk
    wid = D_HEADS * D_KDIM
    c0 = col0 // wid
    zspec = lambda c: pl.BlockSpec((blk, wid), lambda b, j: (b0 + b * nj + j, c0 + c))
    tspec = pl.BlockSpec((blk, D_KDIM), lambda b, j: (b0 + b * nj + j, 0))
    sspec = pl.BlockSpec((1, D_HEADS, D_KDIM, D_VDIM), lambda b, j: (b, 0, 0, 0))
    return pl.pallas_call(
        functools.partial(_retention_kernel, blk=blk),
        grid=(nb, nj),
        in_specs=[zspec(0), zspec(1), zspec(2), zspec(3), tspec, tspec, sspec, _const_spec(gdn.shape),
                  pl.BlockSpec(memory_space=pltpu.SMEM)],
        out_specs=[pl.BlockSpec((blk, wid), lambda b, j: (b * nj + j, 0)), sspec],
        out_shape=[jax.ShapeDtypeStruct((nb * t, wid), BF16), jax.ShapeDtypeStruct(s0.shape, F32)],
        compiler_params=_cparams("parallel", "arbitrary"),
        name="retention",
    )(z, z, z, z, cos, sin_signed, s0, gdn, log_gamma)


def _rope_tables(pos_groups, reps_groups, tail, d, width):
    inv = ROPE_THETA ** (-jnp.arange(0, d, 2, dtype=F32) / d)
    cos_rows, sin_rows = [], []
    for pos, reps in zip(pos_groups, reps_groups):
        ang = pos.astype(F32)[:, None] * inv[None, :]
        cos, sin = lax.optimization_barrier((jnp.cos(ang), jnp.sin(ang)))
        cos_rows.append(jnp.tile(jnp.tile(jnp.concatenate([cos, cos], axis=-1), (1, width // d)), (reps, 1)))
        sin_rows.append(jnp.tile(jnp.tile(jnp.concatenate([-sin, sin], axis=-1), (1, width // d)), (reps, 1)))
    cos_rows.append(jnp.ones((tail, width), F32))
    sin_rows.append(jnp.zeros((tail, width), F32))
    return jnp.concatenate(cos_rows, axis=0), jnp.concatenate(sin_rows, axis=0)


def kernel(x_prompt, x_sample, cache_a_k, cache_a_v, cache_a_kidx, cache_b_latent, cache_b_krope, state_c_conv,
           state_c_h, state_d_s, meta_tokens, norm_mix, norm_mlp, ab_w_in, ab_w_out, a_q_norm, a_k_norm,
           b_q_lat_norm, b_w_uq, b_kv_lat_norm, b_w_ukv, b_qn_norm, b_qr_norm, b_kn_norm, b_kr_norm, cd_w_in,
           cd_w_out, c_conv_w, c_conv_b, c_gate_a_w, c_gate_a_b, c_gate_x_w, c_gate_x_b, c_lambda, d_out_norm,
           mlp_w_up, mlp_w_down):
    nb_p, seq_p, d_model = x_prompt.shape
    nb_s, seq_s, _ = x_sample.shape
    past = cache_a_k.shape[2]
    n_meta = meta_tokens.shape[0]
    depth = norm_mix.shape[0]
    t_real = n_meta + seq_p
    tp = -(-t_real // ROW_BLOCK) * ROW_BLOCK
    pad = tp - t_real
    rp = nb_p * tp
    rs = nb_s * seq_s
    assert seq_s % 8 == 0 and rp % seq_s == 0 and past % 8 == 0
    assert seq_p % ROW_BLOCK == 0 and ROW_BLOCK % CHUNK == 0
    r_real = rp + rs
    tile = 512
    r_tot = -(-r_real // tile) * tile
    tm_out = math.gcd(tile, rp, rs)
    topk_p = min(TOPK_MAX, seq_p // 4)
    topk_s = min(TOPK_MAX, (past + seq_s) // 4)

    pos_p = jnp.concatenate([jnp.zeros((pad,), jnp.int32), jnp.arange(t_real, dtype=jnp.int32)])
    ck_p = jnp.concatenate([jnp.zeros((n_meta,), jnp.int32), 1 + jnp.arange(seq_p, dtype=jnp.int32) // CHUNK])
    cq_p = jnp.concatenate([jnp.full((pad,), -1, jnp.int32), ck_p])
    ckk_p = jnp.concatenate([jnp.full((pad,), 2 ** 30, jnp.int32), ck_p])[None, :]
    pos_s_all = jnp.arange(past + seq_s, dtype=jnp.int32)
    ck_s = pos_s_all // CHUNK
    pos_s = pos_s_all[past:]
    tail = r_tot - r_real
    cq_rows =jnp.concatenate([jnp.tile(cq_p, nb_p), jnp.tile(ck_s[past:], nb_s), jnp.full((tail,), -1, jnp.int32)])
    cq_rows = jnp.broadcast_to(cq_rows[:, None], (r_tot, LANES))
    tabs128 = _rope_tables((pos_p, pos_s), (nb_p, nb_s), tail, 128, 128)
    tabs64 = _rope_tables((pos_p, pos_s), (nb_p, nb_s), tail, 64, 128)
    tabs256 = _rope_tables((pos_p, pos_s), (nb_p, nb_s), tail, 256, 256)

    head = jnp.concatenate([jnp.zeros((pad, d_model), F32), meta_tokens.astype(F32)], axis=0)
    pieces = [p for b in range(nb_p) for p in (head, x_prompt[b])]
    h = jnp.concatenate(pieces + [x_sample.reshape(rs, d_model), jnp.zeros((tail, d_model), F32)], axis=0)
    mlp_up, mlp_down = mlp_w_up.astype(BF16), mlp_w_down.astype(BF16)

    def prompt_rows(x):
        return x[:rp].reshape(nb_p, tp, -1)[:, pad:]

    def sample_rows(x):
        return x[rp:rp + rs].reshape(nb_s, seq_s, -1)

    ab_p, ab_s, cd_p, cd_s = [], [], [], []
    for layer in range(depth):
        i = layer // 2
        gmix = norm_mix[layer][None, :]
        if layer % 2 == 0:
            w = ab_w_in[i]
            offs = np.cumsum([0, 1024, 256, 256, 1024, 64, 16, 512, 256, 64])
            sec = lambda n: w[:, offs[n]:offs[n + 1]]
            win = jnp.concatenate([sec(0), sec(1), sec(2), sec(3), sec(6), sec(7), sec(4), sec(8), sec(5),
                                   jnp.zeros((d_model, AB_IN_PAD - 3472), F32)], axis=1).astype(BF16)
            wuq = b_w_uq[i].reshape(-1, B_HEADS, B_NOPE + B_ROPE)
            wuq = jnp.concatenate([wuq[:, :, :B_NOPE].reshape(-1, B_HEADS * B_NOPE),
                                   wuq[:, :, B_NOPE:].reshape(-1, B_HEADS * B_ROPE)], axis=1).astype(BF16)
            wukv = b_w_ukv[i].reshape(-1, B_HEADS, B_NOPE + B_V)
            wukv = jnp.concatenate([wukv[:, :, :B_NOPE].reshape(-1, B_HEADS * B_NOPE),
                                    wukv[:, :, B_NOPE:].reshape(-1, B_HEADS * B_V)], axis=1).astype(BF16)
            two = lambda g: jnp.concatenate([g, g])[None, :]
            gkr2 = jnp.concatenate([jnp.ones((64,), F32), b_kr_norm[i]])[None, :]
            (q, kf, vf, kb, vb_a, qi, kikr, kikrb, wi, qn, qr, lat) = _ab_proj(
                h, gmix, win, wuq, a_q_norm[i][None, :], a_k_norm[i][None, :], b_q_lat_norm[i][None, :],
                b_kv_lat_norm[i][None, :], b_qn_norm[i][None, :], two(b_qr_norm[i]), gkr2,
                tabs128 + tabs64, 256)
            gkn = b_kn_norm[i][None, :]
            kn, vbb = _ukv(lat, wukv, gkn, 512)
            pkn, pvb = _ukv(cache_b_latent[i].reshape(nb_s * past, -1), wukv, gkn, 512)
            oa_p, ob_p = _attn_prompt(qi, wi, q, qn, qr, cq_rows, kikrb, kb, vb_a, kn, vbb, ckk_p, nb_p, tp, topk_p, 4)
            oa_s, ob_s = _attn_sample(qi, wi, q, qn, qr, cq_rows, kikrb, kb, vb_a, kn, vbb,
                                      cache_a_kidx[i], cache_a_k[i], cache_a_v[i], pkn, cache_b_krope[i], pvb,
                                      ck_s[None, :past], ck_s[None, past:], rp, nb_s, seq_s, topk_s)
            h = _out_proj(h, oa_p, ob_p, oa_s, ob_s, ab_w_out[i].astype(BF16), tm_out)
            kv4 = lambda x: x.reshape(x.shape[0], x.shape[1], A_KV_HEADS, A_HEAD_DIM)
            ab_p.append((kv4(prompt_rows(kf)), kv4(prompt_rows(vf)), prompt_rows(kikr)[..., :64],
                         prompt_rows(lat), prompt_rows(kikr)[..., 64:]))
            ab_s.append((kv4(sample_rows(kf)), kv4(sample_rows(vf)), sample_rows(kikr)[..., :64],
                         sample_rows(lat), sample_rows(kikr)[..., 64:]))
        else:
            z = _cd_proj(h, gmix, cd_w_in[i].astype(BF16), 1024 if r_tot % 1024 == 0 else tile, 1024)
            cwid = C_BLOCKS * C_BLOCK
            gw = jnp.concatenate([c_gate_a_w[i], c_gate_x_w[i]], axis=-1).astype(BF16)
            cargs = (c_conv_w[i], c_conv_b[i][None, :], gw, c_gate_a_b[i][None, :], c_gate_x_b[i][None, :],
                     c_lambda[i][None, :])
            oc_p, hl_p = _rglru(z, cq_rows, jnp.zeros((nb_p, 8, cwid), F32), jnp.zeros((nb_p, 1, cwid), F32),
                                *cargs, 0, nb_p, tp, ROW_BLOCK)
            prev_s = jnp.concatenate([jnp.zeros((nb_s, 8 - (CONV_W - 1), cwid), F32), state_c_conv[i]], axis=1)
            oc_s, hl_s = _rglru(z, cq_rows, prev_s, state_c_h[i][:, None, :], *cargs, rp, nb_s, seq_s, seq_s)
            hl_p, hl_s = hl_p[:, 0], hl_s[:, 0]
            log_gamma = jnp.log(1.0 - 2.0 ** (-5.0 - jnp.arange(D_HEADS, dtype=F32)))
            gdn = d_out_norm[i][None, :]
            od_p, ds_p = _retention(z, *tabs256, jnp.zeros((nb_p, D_HEADS, D_KDIM, D_VDIM), F32), gdn, log_gamma,
                                    2 * cwid, 0, nb_p, tp, ROW_BLOCK)
            od_s, ds_s = _retention(z, *tabs256, state_d_s[i], gdn, log_gamma, 2 * cwid, rp, nb_s, seq_s, seq_s)
            h = _out_proj(h, oc_p, od_p, oc_s, od_s, cd_w_out[i].astype(BF16), tm_out)
            nc = CONV_W - 1
            def seq_tails(row0, nb, t):
                n = min(nc, t)
                rows = row0 + t - n + (jnp.arange(nb, dtype=jnp.int32) * t)[:, None] + jnp.arange(n, dtype=jnp.int32)
                return jnp.take(z, rows.reshape(-1), axis=0)[:, :cwid].reshape(nb, n, cwid)

            cd_p.append((seq_tails(0, nb_p, tp), hl_p, ds_p))
            cd_s.append((jnp.concatenate([state_c_conv[i], seq_tails(rp, nb_s, seq_s)], axis=1)[:, -nc:], hl_s, ds_s))
        last = layer == depth - 1
        h = _mlp(h, norm_mlp[layer][None, :], mlp_up, mlp_down, layer, tm_out if last else tile, 512,
                 split_rows=(rp, rs) if last else None)

    def stack(entries, j):
        return jnp.stack([e[j] for e in entries])

    y_p = h[0].reshape(nb_p, tp, d_model)[:, pad + n_meta:]
    y_s = h[1].reshape(nb_s, seq_s, d_model)
    return (y_p, y_s,
            stack(ab_p, 0), stack(ab_p, 1), stack(ab_p, 2), stack(ab_p, 3), stack(ab_p, 4),
            stack(cd_p, 0), stack(cd_p, 1), stack(cd_p, 2),
            stack(ab_s, 0), stack(ab_s, 1), stack(ab_s, 2), stack(ab_s, 3), stack(ab_s, 4),
            stack(cd_s, 0), stack(cd_s, 1), stack(cd_s, 2))
```

```python
import functools
import math

import jax
import jax.numpy as jnp
import numpy as np
from jax import lax
from jax.experimental import pallas as pl
from jax.experimental.pallas import tpu as pltpu

F32 = jnp.float32
BF16 = jnp.bfloat16

CHUNK = 64
ROPE_THETA = 10000.0
EPS = 1e-6
A_HEADS, A_KV_HEADS, A_HEAD_DIM = 8, 2, 128
IDX_HEADS, IDX_DIM = 16, 64
TOPK_MAX = 256
B_HEADS, B_NOPE, B_ROPE, B_V = 8, 128, 64, 128
C_BLOCKS, C_BLOCK, CONV_W = 8, 128, 4
RG_C = 8.0
D_HEADS, D_KDIM, D_VDIM = 4, 256, 256

LANES = 128
ROW_BLOCK = 128
NEG = -1e30
INT_MIN = -2 ** 31
VMEM_LIMIT = 56 * 1024 * 1024


def _cparams(*sem):
    return pltpu.CompilerParams(dimension_semantics=sem, vmem_limit_bytes=VMEM_LIMIT)


def _const_spec(shape):
    nd = len(shape)
    return pl.BlockSpec(shape, lambda *_: (0,) * nd)


def _rms(x, g):
    ms = jnp.mean(x * x, axis=-1, keepdims=True)
    return x * lax.rsqrt(ms + EPS) * g


def _rms_half(x, g):
    lane = lax.broadcasted_iota(jnp.int32, x.shape, 1)
    lo = lane < 64
    xx = x * x
    s_lo = jnp.sum(jnp.where(lo, xx, 0.0), axis=-1, keepdims=True)
    s_hi = jnp.sum(jnp.where(lo, 0.0, xx), axis=-1, keepdims=True)
    ms = jnp.where(lo, s_lo, s_hi) * (1.0 / 64.0)
    return x * lax.rsqrt(ms + EPS) * g


def _rope128(x, cos, sin_signed):
    return x * cos + pltpu.roll(x, 64, 1) * sin_signed


def _rope64(x, cos, sin_signed):
    lane = lax.broadcasted_iota(jnp.int32, x.shape, 1)
    first = (lane % 64) < 32
    rot = jnp.where(first, pltpu.roll(x, 96, 1), pltpu.roll(x, 32, 1))
    return x * cos + rot * sin_signed


AB_COLS = dict(qa=(0, 1024), ka=(1024, 1280), va=(1280, 1536), qi=(1536, 2560), cq=(2560, 3072),
               ckv=(3072, 3328), kikr=(3328, 3456), wi=(3456, 3584))
AB_IN_PAD = 3584


def _ab_proj_kernel(h_ref, gmix_ref, win_ref, wuq_ref, gaq_ref, gak_ref, gqlat_ref, gkvlat_ref, gqn_ref,
                    gqr_ref, gkr_ref, c128_ref, s128_ref, c64_ref, s64_ref,
                    q_ref, kf_ref, vf_ref, kb_ref, vb_ref, qi_ref, kikr_ref, kikrb_ref, wi_ref, qn_ref, qr_ref,
                    lat_ref):
    xn = _rms(h_ref[...], gmix_ref[...]).astype(BF16)
    c128, s128, c64, s64 = c128_ref[...], s128_ref[...], c64_ref[...], s64_ref[...]

    def proj(name):
        a, b = AB_COLS[name]
        return jnp.dot(xn, win_ref[:, a:b], preferred_element_type=F32)

    z = proj('qa')
    qscale = A_HEAD_DIM ** -0.5
    for h in range(A_HEADS):
        x = _rope128(_rms(z[:, h * 128:(h + 1) * 128], gaq_ref[...]), c128, s128)
        q_ref[:, h * 128:(h + 1) * 128] = (x * qscale).astype(BF16)
    z = proj('ka')
    for h in range(A_KV_HEADS):
        x = _rope128(_rms(z[:, h * 128:(h + 1) * 128], gak_ref[...]), c128, s128)
        kf_ref[:, h * 128:(h + 1) * 128] = x
        kb_ref[:, h * 128:(h + 1) * 128] = x.astype(BF16)
    z = proj('va')
    vf_ref[...] = z
    vb_ref[...] = z.astype(BF16)
    z = proj('qi')
    for p in range(IDX_HEADS // 2):
        x = _rope64(z[:, p * 128:(p + 1) * 128], c64, s64)
        qi_ref[:, p * 128:(p + 1) * 128] = (x * (IDX_DIM ** -0.5)).astype(BF16)
    z = proj('kikr')
    lane = lax.broadcasted_iota(jnp.int32, z.shape, 1)
    x = jnp.where(lane < 64, z, _rms_half(z, gkr_ref[...]))
    x = _rope64(x, c64, s64)
    kikr_ref[...] = x
    kikrb_ref[...] = x.astype(BF16)
    wi_ref[...] = proj('wi') * (IDX_HEADS ** -0.5)
    lat = _rms(proj('ckv'), gkvlat_ref[...])
    lat_ref[...] = lat
    cq = _rms(proj('cq'), gqlat_ref[...]).astype(BF16)
    bscale = (B_NOPE + B_ROPE) ** -0.5
    zq = jnp.dot(cq, wuq_ref[:, :B_HEADS * B_NOPE], preferred_element_type=F32)
    for h in range(B_HEADS):
        x = _rms(zq[:, h * 128:(h + 1) * 128], gqn_ref[...])
        qn_ref[:, h * 128:(h + 1) * 128] = (x * bscale).astype(BF16)
    zq = jnp.dot(cq, wuq_ref[:, B_HEADS * B_NOPE:], preferred_element_type=F32)
    for p in range(B_HEADS // 2):
        x = _rope64(_rms_half(zq[:, p * 128:(p + 1) * 128], gqr_ref[...]), c64, s64)
        qr_ref[:, p * 128:(p + 1) * 128] = (x * bscale).astype(BF16)


def _ab_proj(h, gmix, win, wuq, gaq, gak, gqlat, gkvlat, gqn, gqr2, gkr2, tabs, tm):
    r, d = h.shape
    row = lambda w: pl.BlockSpec((tm, w), lambda i: (i, 0))
    outs = [(1024, BF16), (256, F32), (256, F32), (256, BF16), (256, BF16), (1024, BF16), (128, F32),
            (128, BF16), (128, F32), (1024, BF16), (512, BF16), (256, F32)]
    return pl.pallas_call(
        _ab_proj_kernel,
        grid=(r // tm,),
        in_specs=[row(d), _const_spec(gmix.shape), _const_spec(win.shape), _const_spec(wuq.shape),
                  _const_spec(gaq.shape), _const_spec(gak.shape), _const_spec(gqlat.shape),
                  _const_spec(gkvlat.shape), _const_spec(gqn.shape), _const_spec(gqr2.shape),
                  _const_spec(gkr2.shape), row(128), row(128), row(128), row(128)],
        out_specs=[row(w) for w, _ in outs],
        out_shape=[jax.ShapeDtypeStruct((r, w), dt) for w, dt in outs],
        compiler_params=_cparams("parallel"),
        name="ab_proj",
    )(h, gmix, win, wuq, gaq, gak, gqlat, gkvlat, gqn, gqr2, gkr2, *tabs)


def _ukv_kernel(lat_ref, w_ref, gkn_ref, kn_ref, vb_ref):
    lat = lat_ref[...].astype(BF16)
    z = jnp.dot(lat, w_ref[:, :B_HEADS * B_NOPE], preferred_element_type=F32)
    for h in range(B_HEADS):
        kn_ref[:, h * 128:(h + 1) * 128] = _rms(z[:, h * 128:(h + 1) * 128], gkn_ref[...]).astype(BF16)
    vb_ref[...] = jnp.dot(lat, w_ref[:, B_HEADS * B_NOPE:], preferred_element_type=F32).astype(BF16)


def _ukv(lat, w, gkn, tm):
    n = lat.shape[0]
    row = lambda wd: pl.BlockSpec((tm, wd), lambda i: (i, 0))
    return pl.pallas_call(
        _ukv_kernel,
        grid=(n // tm,),
        in_specs=[row(lat.shape[1]), _const_spec(w.shape), _const_spec(gkn.shape)],
        out_specs=[row(1024), row(1024)],
        out_shape=[jax.ShapeDtypeStruct((n, 1024), BF16)] * 2,
        compiler_params=_cparams("parallel"),
        name="mla_ukv",
    )(lat, w, gkn)


def _dot_t(a, b):
    return lax.dot_general(a, b, (((1,), (1,)), ((), ())), preferred_element_type=F32)


def _float_key(x):
    bits = pltpu.bitcast(x, jnp.int32)
    return bits ^ ((bits >> 31) & jnp.int32(0x7FFFFFFF))


def _kth_largest(keys, k):
    rows = keys[0].shape[0]

    def body(i, t):
        bit = jnp.int32(31) - i
        cand = t + jnp.left_shift(jnp.int32(1), bit)
        cnt = jnp.zeros((rows, 1), F32)
        for key in keys:
            cnt = cnt + jnp.sum(jnp.where(key >= cand, 1.0, 0.0), axis=-1, keepdims=True)
        return jnp.where(cnt >= float(k), cand, t)

    return lax.fori_loop(0, 32, body, jnp.full((rows, 1), INT_MIN, jnp.int32))


def _kv_head(x, c):
    return x[c] if isinstance(x, tuple) else x[:, c * 128:(c + 1) * 128]


def _tie_break(keys, thr, topk, bias_refs):
    nq = keys[0].shape[0]
    n_gt = functools.reduce(jnp.add, [jnp.sum(jnp.where(key > thr, 1.0, 0.0), axis=-1, keepdims=True)
                                      for key in keys])
    need = float(topk) - n_gt
    r_i = lax.broadcasted_iota(jnp.int32, (LANES, LANES), 0)
    c_i = lax.broadcasted_iota(jnp.int32, (LANES, LANES), 1)
    before = jnp.where(r_i < c_i, 1.0, 0.0).astype(BF16)
    seen = jnp.zeros((nq, 1), F32)
    for key, ref in zip(keys, bias_refs):
        for s0 in range(0, key.shape[1], LANES):
            blk = key[:, s0:s0 + LANES]
            w = blk.shape[1]
            eq = jnp.where(blk == thr, 1.0, 0.0)
            rank = seen + jnp.dot(eq.astype(BF16), before[:w, :w], preferred_element_type=F32)
            sel = jnp.logical_or(blk > thr, jnp.logical_and(eq > 0.0, rank < need))
            ref[:, s0:s0 + w] = jnp.where(jnp.logical_and(sel, blk > INT_MIN), 0.0, NEG)
            seen = seen + jnp.sum(eq, axis=-1, keepdims=True)


def _dsa_core(qi, wi, q, cq, pieces, topk, bias_refs):
    nq = qi.shape[0]
    keys = []
    for ki, _, _, ck in pieces:
        score = jnp.zeros((nq, ki.shape[0]), F32)
        for h in range(IDX_HEADS):
            s_h = _dot_t(qi[:, h * 64:(h + 1) * 64], ki)
            score = score + jnp.maximum(s_h, 0.0) * wi[:, h:h + 1]
        keys.append(jnp.where(ck <= cq, _float_key(score), INT_MIN))
    thr = _kth_largest(keys, topk)
    n_ge = jnp.zeros((nq, 1), F32)
    for key, ref in zip(keys, bias_refs):
        ge = key >= thr
        ref[...] = jnp.where(jnp.logical_and(ge, key > INT_MIN), 0.0, NEG)
        n_ge = n_ge + jnp.sum(jnp.where(ge, 1.0, 0.0), axis=-1, keepdims=True)
    tied = jnp.max(jnp.where(jnp.logical_and(n_ge > float(topk), thr > INT_MIN), 1.0, 0.0))
    pl.when(tied > 0.0)(functools.partial(_tie_break, keys, thr, topk, bias_refs))
    group = A_HEADS // A_KV_HEADS
    biases = [jnp.concatenate([ref[...]] * group, axis=0) for ref in bias_refs]
    outs = []
    for c in range(A_KV_HEADS):
        qg = jnp.concatenate([q[:, (c * group + g) * 128:(c * group + g + 1) * 128] for g in range(group)], axis=0)
        ss = [_dot_t(qg, _kv_head(k, c)) + bias for (_, k, _, _), bias in zip(pieces, biases)]
        m = functools.reduce(jnp.maximum, [jnp.max(s, axis=-1, keepdims=True) for s in ss])
        ps = [jnp.exp(s - m) for s in ss]
        den = functools.reduce(jnp.add, [jnp.sum(p, axis=-1, keepdims=True) for p in ps])
        o = functools.reduce(jnp.add, [jnp.dot(p.astype(BF16), _kv_head(v, c), preferred_element_type=F32)
                                       for p, (_, _, v, _) in zip(ps, pieces)])
        o = o / den
        outs.extend([o[g * nq:(g + 1) * nq] for g in range(group)])
    return jnp.concatenate(outs, axis=-1)


def _mla_core(qn, qr, cq, pieces):
    outs = []
    biases = [jnp.where(ck <= cq, 0.0, NEG) for _, _, _, ck in pieces]
    for h in range(B_HEADS):
        ss = []
        for (kn, kr, _, _), bias in zip(pieces, biases):
            s = _dot_t(qn[:, h * 128:(h + 1) * 128], kn[:, h * 128:(h + 1) * 128])
            s = s + _dot_t(qr[:, h * 64:(h + 1) * 64], kr)
            ss.append(s + bias)
        m = functools.reduce(jnp.maximum, [jnp.max(s, axis=-1, keepdims=True) for s in ss])
        ps = [jnp.exp(s - m) for s in ss]
        den = functools.reduce(jnp.add, [jnp.sum(p, axis=-1, keepdims=True) for p in ps])
        o = functools.reduce(jnp.add, [jnp.dot(p.astype(BF16), vb[:, h * 128:(h + 1) * 128],
                                               preferred_element_type=F32)
                                       for p, (_, _, vb, _) in zip(ps, pieces)])
        outs.append(o / den)
    return jnp.concatenate(outs, axis=-1)


def _attn_prompt_kernel(qi_ref, wi_ref, q_ref, qn_ref, qr_ref, cq_ref, kikr_ref, k_ref, v_ref, kn_ref, vb_ref,
                        ck_ref, *rest, topk, s_len):
    oa_ref, ob_ref, bias_ref = rest[-3:]
    cq = cq_ref[:, 0:1]
    valid = cq >= 0
    ck = ck_ref[:, :s_len]
    kikr = kikr_ref[:s_len, :]
    oa = _dsa_core(qi_ref[...], wi_ref[...], q_ref[...], cq,
                   [(kikr[:, :64], k_ref[:s_len, :], v_ref[:s_len, :], ck)], topk, [bias_ref])
    oa_ref[...] = jnp.where(valid, oa, 0.0).astype(BF16)
    ob = _mla_core(qn_ref[...], qr_ref[...], cq, [(kn_ref[:s_len, :], kikr[:, 64:], vb_ref[:s_len, :], ck)])
    ob_ref[...] = jnp.where(valid, ob, 0.0).astype(BF16)


def _attn_prompt(qi, wi, q, qn, qr, cq, kikr, k, v, kn, vb, ck, nb, tp, topk, n_ranges):
    tq = ROW_BLOCK
    nj = tp // tq
    edges = sorted({-(-nj * r // n_ranges) for r in range(n_ranges + 1)})
    kspec = lambda w: pl.BlockSpec((tp, w), lambda b, j: (b, 0))
    outs = ()
    for j0, j1 in zip(edges[:-1], edges[1:]):
        qspec = lambda w, j0=j0: pl.BlockSpec((tq, w), lambda b, j: (b * nj + j0 + j, 0))
        n_in = 12
        outs = pl.pallas_call(
            functools.partial(_attn_prompt_kernel, topk=topk, s_len=j1 * tq),
            grid=(nb, j1 - j0),
            in_specs=[qspec(1024), qspec(128), qspec(1024), qspec(1024), qspec(512), qspec(128),
                      kspec(128), kspec(256), kspec(256), kspec(1024), kspec(1024), _const_spec(ck.shape)]
                     + [pl.BlockSpec(memory_space=pl.ANY)] * len(outs),
            out_specs=[qspec(1024), qspec(1024)],
            out_shape=[jax.ShapeDtypeStruct((nb * tp, 1024), BF16)] * 2,
            scratch_shapes=[pltpu.VMEM((tq, j1 * tq), F32)],
            input_output_aliases={n_in + o: o for o in range(len(outs))},
            compiler_params=_cparams("parallel", "arbitrary"),
            name="attn_prompt",
        )(qi, wi, q, qn, qr, cq, kikr, k, v, kn, vb, ck, *outs)
    return outs


def _attn_sample_kernel(qi_ref, wi_ref, q_ref, qn_ref, qr_ref, cq_ref, kikr_ref, k_ref, v_ref, kn_ref, vb_ref,
                        pki_ref, pk_ref, pv_ref, pkn_ref, pkr_ref, pvb_ref, ckp_ref, ckn_ref,
                        oa_ref, ob_ref, biasp_ref, biasn_ref, *, topk):
    cq = cq_ref[:, 0:1]
    ckp, ckn = ckp_ref[...], ckn_ref[...]
    kikr = kikr_ref[...]
    past = pki_ref.shape[1]
    heads = lambda ref: tuple(ref[pl.ds(c, past, stride=A_KV_HEADS), :].astype(BF16) for c in range(A_KV_HEADS))
    past_a = (pki_ref[0].astype(BF16), heads(pk_ref), heads(pv_ref), ckp)
    new_a = (kikr[:, :64], k_ref[...], v_ref[...], ckn)
    oa_ref[...] = _dsa_core(qi_ref[...], wi_ref[...], q_ref[...], cq, [past_a, new_a], topk,
                            [biasp_ref, biasn_ref]).astype(BF16)
    past_b = (pkn_ref[...], pkr_ref[0].astype(BF16), pvb_ref[...], ckp)
    new_b = (kn_ref[...], kikr[:, 64:], vb_ref[...], ckn)
    ob_ref[...] = _mla_core(qn_ref[...], qr_ref[...], cq, [past_b, new_b]).astype(BF16)


def _attn_sample(qi, wi, q, qn, qr, cq, kikr, k, v, kn, vb, pki, pk, pv, pkn, pkr, pvb, ckp, ckn,
                 row0, nb, ts, topk):
    past = pk.shape[1]
    blk0 = row0 // ts
    nspec = lambda w: pl.BlockSpec((ts, w), lambda b: (blk0 + b, 0))
    pspec = lambda w: pl.BlockSpec((1, past, w), lambda b: (b, 0, 0))
    p2spec = lambda w: pl.BlockSpec((past, w), lambda b: (b, 0))
    kvspec = pl.BlockSpec((None, past * A_KV_HEADS, A_HEAD_DIM), lambda b: (b, 0, 0))
    pk = pk.reshape(nb, past * A_KV_HEADS, A_HEAD_DIM)
    pv = pv.reshape(nb, past * A_KV_HEADS, A_HEAD_DIM)
    return pl.pallas_call(
        functools.partial(_attn_sample_kernel, topk=topk),
        grid=(nb,),
        in_specs=[nspec(1024), nspec(128), nspec(1024), nspec(1024), nspec(512), nspec(128),
                  nspec(128), nspec(256), nspec(256), nspec(1024), nspec(1024),
                  pspec(64), kvspec, kvspec, p2spec(1024), pspec(64), p2spec(1024),
                  _const_spec(ckp.shape), _const_spec(ckn.shape)],
        out_specs=[pl.BlockSpec((ts, 1024), lambda b: (b, 0))] * 2,
        out_shape=[jax.ShapeDtypeStruct((nb * ts, 1024), BF16)] * 2,
        scratch_shapes=[pltpu.VMEM((ts, past), F32), pltpu.VMEM((ts, ts), F32)],
        compiler_params=_cparams("parallel"),
        name="attn_sample",
    )(qi, wi, q, qn, qr, cq, kikr, k, v, kn, vb, pki, pk, pv, pkn, pkr, pvb, ckp, ckn)


def _out_proj_kernel(h_ref, oap_ref, obp_ref, oas_ref, obs_ref, w_ref, o_ref, *, n_p, n_ps):
    half = oap_ref.shape[1]
    i = pl.program_id(0)

    def project(oa_ref, ob_ref):
        y = jnp.dot(oa_ref[...], w_ref[:half, :], preferred_element_type=F32)
        y = y + jnp.dot(ob_ref[...], w_ref[half:, :], preferred_element_type=F32)
        o_ref[...] = h_ref[...] + y

    pl.when(i < n_p)(functools.partial(project, oap_ref, obp_ref))
    pl.when(jnp.logical_and(i >= n_p, i < n_ps))(functools.partial(project, oas_ref, obs_ref))

    @pl.when(i >= n_ps)
    def _():
        o_ref[...] = h_ref[...]


def _out_proj(h, oa_p, ob_p, oa_s, ob_s, w, tm):
    r, d = h.shape
    n_p, n_s = oa_p.shape[0] // tm, oa_s.shape[0] // tm
    assert oa_p.shape[0] % tm == 0 and oa_s.shape[0] % tm == 0
    row = lambda wd: pl.BlockSpec((tm, wd), lambda i: (i, 0))
    pspec = lambda wd: pl.BlockSpec((tm, wd), lambda i: (jnp.minimum(i, n_p - 1), 0))
    sspec = lambda wd: pl.BlockSpec((tm, wd), lambda i: (jnp.clip(i - n_p, 0, n_s - 1), 0))
    return pl.pallas_call(
        functools.partial(_out_proj_kernel, n_p=n_p, n_ps=n_p + n_s),
        grid=(r // tm,),
        in_specs=[row(d), pspec(oa_p.shape[1]), pspec(ob_p.shape[1]), sspec(oa_s.shape[1]), sspec(ob_s.shape[1]),
                  _const_spec(w.shape)],
        out_specs=row(d),
        out_shape=jax.ShapeDtypeStruct((r, d), F32),
        compiler_params=_cparams("parallel"),
        name="out_proj",
    )(h, oa_p, ob_p, oa_s, ob_s, w)


def _mlp_kernel(h_ref, g_ref, wup_ref, wdn_ref, *rest, n_split, n_end):
    *o_refs, xn_ref, acc_ref = rest
    i, f = pl.program_id(0), pl.program_id(1)

    @pl.when(f == 0)
    def _():
        x = h_ref[...]
        xn_ref[...] = _rms(x, g_ref[...]).astype(BF16)
        acc_ref[...] = x

    u = jnp.dot(xn_ref[...], wup_ref[...], preferred_element_type=F32)
    u = jnp.maximum(u, 0.0)
    acc_ref[...] += jnp.dot((u * u).astype(BF16), wdn_ref[...], preferred_element_type=F32)

    @pl.when(f == pl.num_programs(1) - 1)
    def _():
        if len(o_refs) == 1:
            o_refs[0][...] = acc_ref[...]
        else:
            @pl.when(i < n_split)
            def _():
                o_refs[0][...] = acc_ref[...]

            @pl.when(jnp.logical_and(i >= n_split, i < n_end))
            def _():
                o_refs[1][...] = acc_ref[...]


def _mlp(h, g, wup, wdn, layer, tm, tf, split_rows=None):
    r, d = h.shape
    dff = wup.shape[2]
    n_end = r // tm
    if split_rows is None:
        n_split = 0
        out_specs = [pl.BlockSpec((tm, d), lambda i, f: (i, 0))]
        out_shape = [jax.ShapeDtypeStruct((r, d), F32)]
    else:
        a, b = split_rows
        assert a % tm == 0 and b % tm == 0
        n_split, n_b = a // tm, b // tm
        n_end = n_split + n_b
        out_specs = [pl.BlockSpec((tm, d), lambda i, f: (jnp.minimum(i, n_split - 1), 0)),
                     pl.BlockSpec((tm, d), lambda i, f: (jnp.clip(i - n_split, 0, n_b - 1), 0))]
        out_shape = [jax.ShapeDtypeStruct((a, d), F32), jax.ShapeDtypeStruct((b, d), F32)]
    res = pl.pallas_call(
        functools.partial(_mlp_kernel, n_split=n_split, n_end=n_end),
        grid=(r // tm, dff // tf),
        in_specs=[pl.BlockSpec((tm, d), lambda i, f: (i, 0)), pl.BlockSpec((1, d), lambda i, f: (0, 0)),
                  pl.BlockSpec((None, d, tf), lambda i, f: (layer, 0, f)),
                  pl.BlockSpec((None, tf, d), lambda i, f: (layer, f, 0))],
        out_specs=out_specs,
        out_shape=out_shape,
        scratch_shapes=[pltpu.VMEM((tm, d), BF16), pltpu.VMEM((tm, d), F32)],
        compiler_params=_cparams("parallel" if split_rows is None else "arbitrary", "arbitrary"),
        name="mlp",
    )(h, g, wup, wdn)
    return res[0] if split_rows is None else res


def _cd_proj_kernel(h_ref, g_ref, w_ref, z_ref, xn_ref):
    @pl.when(pl.program_id(1) == 0)
    def _():
        xn_ref[...] = _rms(h_ref[...], g_ref[...]).astype(BF16)

    z_ref[...] = jnp.dot(xn_ref[...], w_ref[...], preferred_element_type=F32)


def _cd_proj(h, g, w, tm, tn):
    r, d = h.shape
    n = w.shape[1]
    return pl.pallas_call(
        _cd_proj_kernel,
        grid=(r // tm, n // tn),
        in_specs=[pl.BlockSpec((tm, d), lambda i, j: (i, 0)), pl.BlockSpec((1, d), lambda i, j: (0, 0)),
                  pl.BlockSpec((d, tn), lambda i, j: (0, j))],
        out_specs=pl.BlockSpec((tm, tn), lambda i, j: (i, j)),
        out_shape=jax.ShapeDtypeStruct((r, n), F32),
        scratch_shapes=[pltpu.VMEM((tm, d), BF16)],
        compiler_params=_cparams("parallel", "arbitrary"),
        name="cd_proj",
    )(h, g, w)


def _rglru_kernel(cx_ref, cg_ref, valid_ref, prev0_ref, h0_ref, cw_ref, cb_ref, gw_ref, gab_ref, gxb_ref, lam_ref,
                  o_ref, hl_ref, prev_s, h_s, a_s, b_s):
    @pl.when(pl.program_id(1) == 0)
    def _():
        prev_s[...] = prev0_ref[0]
        h_s[...] = jnp.broadcast_to(h0_ref[0], h_s.shape)

    cx = cx_ref[...]
    tm = cx.shape[0]
    full = jnp.concatenate([prev_s[...], cx], axis=0)
    prev_s[...] = cx[tm - 8:, :]
    xc = cb_ref[...] + cx * cw_ref[CONV_W - 1:CONV_W, :]
    for j in range(CONV_W - 1):
        sh = CONV_W - 1 - j
        xc = xc + full[8 - sh:8 - sh + tm, :] * cw_ref[j:j + 1, :]
    valid = valid_ref[:, 0:1] >= 0
    softplus = jnp.logaddexp(-lam_ref[...], 0.0)
    for n in range(C_BLOCKS):
        sl = slice(n * C_BLOCK, (n + 1) * C_BLOCK)
        xb = xc[:, sl]
        gates = jnp.dot(xb.astype(BF16), gw_ref[n], preferred_element_type=F32)
        r = jax.nn.sigmoid(gates[:, :C_BLOCK] + gab_ref[:, sl])
        ig = jax.nn.sigmoid(gates[:, C_BLOCK:] + gxb_ref[:, sl])
        log_a = -RG_C * r * softplus[:, sl]
        a = jnp.exp(log_a)
        a_s[:, sl] = a
        b = jnp.sqrt(1.0 - a * a) * (ig * xb)
        b_s[:, sl] = jnp.where(valid, b, 0.0)

    rid = lax.broadcasted_iota(jnp.int32, h_s.shape, 0)

    def group(g, h):
        rows = pl.ds(pl.multiple_of(g * 8, 8), 8)
        ca, cb = a_s[rows, :], b_s[rows, :]
        for s in (1, 2, 4):
            m = rid >= s
            cb = jnp.where(m, ca * pltpu.roll(cb, s, 0) + cb, cb)
            ca = jnp.where(m, ca * pltpu.roll(ca, s, 0), ca)
        hh = ca * h + cb
        b_s[rows, :] = hh
        return jnp.broadcast_to(hh[7:8, :], hh.shape)

    h = lax.fori_loop(0, tm // 8, group, h_s[...])
    h_s[...] = h
    hl_ref[0] = h[0:1, :]
    o_ref[...] = (b_s[...] * jax.nn.gelu(cg_ref[...])).astype(BF16)


def _rglru(z, valid, prev0, h0, cw, cb, gw, gab, gxb, lam, row0, nb, t, tm):
    cwid = C_BLOCKS * C_BLOCK
    nj = t // tm
    b0 = row0 // tm
    zspec = lambda c: pl.BlockSpec((tm, cwid), lambda b, j: (b0 + b * nj + j, c))
    return pl.pallas_call(
        _rglru_kernel,
        grid=(nb, nj),
        in_specs=[zspec(0), zspec(1), pl.BlockSpec((tm, LANES), lambda b, j: (b0 + b * nj + j, 0)),
                  pl.BlockSpec((1, 8, cwid), lambda b, j: (b, 0, 0)), pl.BlockSpec((1, 1, cwid), lambda b, j: (b, 0, 0)),
                  _const_spec(cw.shape), _const_spec(cb.shape), _const_spec(gw.shape), _const_spec(gab.shape),
                  _const_spec(gxb.shape), _const_spec(lam.shape)],
        out_specs=[pl.BlockSpec((tm, cwid), lambda b, j: (b * nj + j, 0)),
                   pl.BlockSpec((1, 1, cwid), lambda b, j: (b, 0, 0))],
        out_shape=[jax.ShapeDtypeStruct((nb * t, cwid), BF16), jax.ShapeDtypeStruct((nb, 1, cwid), F32)],
        scratch_shapes=[pltpu.VMEM((8, cwid), F32), pltpu.VMEM((8, cwid), F32), pltpu.VMEM((tm, cwid), F32),
                        pltpu.VMEM((tm, cwid), F32)],
        compiler_params=_cparams("parallel", "arbitrary"),
        name="rglru",
    )(z, z, valid, prev0, h0, cw, cb, gw, gab, gxb, lam)


def _retention_kernel(q_ref, k_ref, v_ref, g_ref, cos_ref, sin_ref, s0_ref, gdn_ref, lg_ref,
                      o_ref, s_ref, *, blk):
    @pl.when(pl.program_id(1) == 0)
    def _():
        s_ref[...] = s0_ref[...]

    cos, sin_signed = cos_ref[...], sin_ref[...]

    def rope(x):
        return x * cos + pltpu.roll(x, D_KDIM // 2, 1) * sin_signed

    ri = lax.broadcasted_iota(jnp.int32, (blk, blk), 0)
    ci = lax.broadcasted_iota(jnp.int32, (blk, blk), 1)
    diff = (ri - ci).astype(F32)
    rows = lax.broadcasted_iota(jnp.int32, (blk, 1), 0).astype(F32)
    for hd in range(D_HEADS):
        lg = lg_ref[hd]
        qsl = slice(hd * D_KDIM, (hd + 1) * D_KDIM)
        vsl = slice(hd * D_VDIM, (hd + 1) * D_VDIM)
        q = rope(q_ref[:, qsl])
        k = rope(k_ref[:, qsl]) * (D_KDIM ** -0.5)
        decay = jnp.where(diff >= 0, jnp.exp(lg * jnp.maximum(diff, 0.0)), 0.0)
        qb, kb, vb = q.astype(BF16), k.astype(BF16), v_ref[:, vsl].astype(BF16)
        att = _dot_t(qb, kb) * decay
        o = jnp.dot(att.astype(BF16), vb, preferred_element_type=F32)
        s_prev = s_ref[0, hd]
        o = o + jnp.dot(qb, s_prev.astype(BF16), preferred_element_type=F32) * jnp.exp(lg * (rows + 1.0))
        kdec = (k * jnp.exp(lg * (blk - 1.0 - rows))).astype(BF16)
        s_ref[0, hd] = jnp.exp(lg * blk) * s_prev + lax.dot_general(kdec, vb, (((0,), (0,)), ((), ())),
                                                                    preferred_element_type=F32)
        od = _rms(o, gdn_ref[...]) * jax.nn.silu(g_ref[:, vsl])
        o_ref[:, vsl] = od.astype(BF16)


def _retention(z, cos, sin_signed, s0, gdn, log_gamma, col0, row0, nb, t, blk):
    nj = t // blk
    b0 = row0 // blk
    wid = D_HEADS * D_KDIM
    c0 = col0 // wid
    zspec = lambda c: pl.BlockSpec((blk, wid), lambda b, j: (b0 + b * nj + j, c0 + c))
    tspec = pl.BlockSpec((blk, D_KDIM), lambda b, j: (b0 + b * nj + j, 0))
    sspec = pl.BlockSpec((1, D_HEADS, D_KDIM, D_VDIM), lambda b, j: (b, 0, 0, 0))
    return pl.pallas_call(
        functools.partial(_retention_kernel, blk=blk),
        grid=(nb, nj),
        in_specs=[zspec(0), zspec(1), zspec(2), zspec(3), tspec, tspec, sspec, _const_spec(gdn.shape),
                  pl.BlockSpec(memory_space=pltpu.SMEM)],
        out_specs=[pl.BlockSpec((blk, wid), lambda b, j: (b * nj + j, 0)), sspec],
        out_shape=[jax.ShapeDtypeStruct((nb * t, wid), BF16), jax.ShapeDtypeStruct(s0.shape, F32)],
        compiler_params=_cparams("parallel", "arbitrary"),
        name="retention",
    )(z, z, z, z, cos, sin_signed, s0, gdn, log_gamma)


def _rope_tables(pos_groups, reps_groups, tail, d, width):
    inv = ROPE_THETA ** (-jnp.arange(0, d, 2, dtype=F32) / d)
    cos_rows, sin_rows = [], []
    for pos, reps in zip(pos_groups, reps_groups):
        ang = pos.astype(F32)[:, None] * inv[None, :]
        cos, sin = lax.optimization_barrier((jnp.cos(ang), jnp.sin(ang)))
        cos_rows.append(jnp.tile(jnp.tile(jnp.concatenate([cos, cos], axis=-1), (1, width // d)), (reps, 1)))
        sin_rows.append(jnp.tile(jnp.tile(jnp.concatenate([-sin, sin], axis=-1), (1, width // d)), (reps, 1)))
    cos_rows.append(jnp.ones((tail, width), F32))
    sin_rows.append(jnp.zeros((tail, width), F32))
    return jnp.concatenate(cos_rows, axis=0), jnp.concatenate(sin_rows, axis=0)


def kernel(x_prompt, x_sample, cache_a_k, cache_a_v, cache_a_kidx, cache_b_latent, cache_b_krope, state_c_conv,
           state_c_h, state_d_s, meta_tokens, norm_mix, norm_mlp, ab_w_in, ab_w_out, a_q_norm, a_k_norm,
           b_q_lat_norm, b_w_uq, b_kv_lat_norm, b_w_ukv, b_qn_norm, b_qr_norm, b_kn_norm, b_kr_norm, cd_w_in,
           cd_w_out, c_conv_w, c_conv_b, c_gate_a_w, c_gate_a_b, c_gate_x_w, c_gate_x_b, c_lambda, d_out_norm,
           mlp_w_up, mlp_w_down):
    nb_p, seq_p, d_model = x_prompt.shape
    nb_s, seq_s, _ = x_sample.shape
    past = cache_a_k.shape[2]
    n_meta = meta_tokens.shape[0]
    depth = norm_mix.shape[0]
    t_real = n_meta + seq_p
    tp = -(-t_real // ROW_BLOCK) * ROW_BLOCK
    pad = tp - t_real
    rp = nb_p * tp
    rs = nb_s * seq_s
    assert seq_s % 8 == 0 and rp % seq_s == 0 and past % 8 == 0
    assert seq_p % ROW_BLOCK == 0 and ROW_BLOCK % CHUNK == 0
    r_real = rp + rs
    tile = 512
    r_tot = -(-r_real // tile) * tile
    tm_out = math.gcd(tile, rp, rs)
    topk_p = min(TOPK_MAX, seq_p // 4)
    topk_s = min(TOPK_MAX, (past + seq_s) // 4)

    pos_p = jnp.concatenate([jnp.zeros((pad,), jnp.int32), jnp.arange(t_real, dtype=jnp.int32)])
    ck_p = jnp.concatenate([jnp.zeros((n_meta,), jnp.int32), 1 + jnp.arange(seq_p, dtype=jnp.int32) // CHUNK])
    cq_p = jnp.concatenate([jnp.full((pad,), -1, jnp.int32), ck_p])
    ckk_p = jnp.concatenate([jnp.full((pad,), 2 ** 30, jnp.int32), ck_p])[None, :]
    pos_s_all = jnp.arange(past + seq_s, dtype=jnp.int32)
    ck_s = pos_s_all // CHUNK
    pos_s = pos_s_all[past:]
    tail = r_tot - r_real
    cq_rows =jnp.concatenate([jnp.tile(cq_p, nb_p), jnp.tile(ck_s[past:], nb_s), jnp.full((tail,), -1, jnp.int32)])
    cq_rows = jnp.broadcast_to(cq_rows[:, None], (r_tot, LANES))
    tabs128 = _rope_tables((pos_p, pos_s), (nb_p, nb_s), tail, 128, 128)
    tabs64 = _rope_tables((pos_p, pos_s), (nb_p, nb_s), tail, 64, 128)
    tabs256 = _rope_tables((pos_p, pos_s), (nb_p, nb_s), tail, 256, 256)

    head = jnp.concatenate([jnp.zeros((pad, d_model), F32), meta_tokens.astype(F32)], axis=0)
    pieces = [p for b in range(nb_p) for p in (head, x_prompt[b])]
    h = jnp.concatenate(pieces + [x_sample.reshape(rs, d_model), jnp.zeros((tail, d_model), F32)], axis=0)
    mlp_up, mlp_down = mlp_w_up.astype(BF16), mlp_w_down.astype(BF16)

    def prompt_rows(x):
        return x[:rp].reshape(nb_p, tp, -1)[:, pad:]

    def sample_rows(x):
        return x[rp:rp + rs].reshape(nb_s, seq_s, -1)

    ab_p, ab_s, cd_p, cd_s = [], [], [], []
    for layer in range(depth):
        i = layer // 2
        gmix = norm_mix[layer][None, :]
        if layer % 2 == 0:
            w = ab_w_in[i]
            offs = np.cumsum([0, 1024, 256, 256, 1024, 64, 16, 512, 256, 64])
            sec = lambda n: w[:, offs[n]:offs[n + 1]]
            win = jnp.concatenate([sec(0), sec(1), sec(2), sec(3), sec(6), sec(7), sec(4), sec(8), sec(5),
                                   jnp.zeros((d_model, AB_IN_PAD - 3472), F32)], axis=1).astype(BF16)
            wuq = b_w_uq[i].reshape(-1, B_HEADS, B_NOPE + B_ROPE)
            wuq = jnp.concatenate([wuq[:, :, :B_NOPE].reshape(-1, B_HEADS * B_NOPE),
                                   wuq[:, :, B_NOPE:].reshape(-1, B_HEADS * B_ROPE)], axis=1).astype(BF16)
            wukv = b_w_ukv[i].reshape(-1, B_HEADS, B_NOPE + B_V)
            wukv = jnp.concatenate([wukv[:, :, :B_NOPE].reshape(-1, B_HEADS * B_NOPE),
                                    wukv[:, :, B_NOPE:].reshape(-1, B_HEADS * B_V)], axis=1).astype(BF16)
            two = lambda g: jnp.concatenate([g, g])[None, :]
            gkr2 = jnp.concatenate([jnp.ones((64,), F32), b_kr_norm[i]])[None, :]
            (q, kf, vf, kb, vb_a, qi, kikr, kikrb, wi, qn, qr, lat) = _ab_proj(
                h, gmix, win, wuq, a_q_norm[i][None, :], a_k_norm[i][None, :], b_q_lat_norm[i][None, :],
                b_kv_lat_norm[i][None, :], b_qn_norm[i][None, :], two(b_qr_norm[i]), gkr2,
                tabs128 + tabs64, tile)
            gkn = b_kn_norm[i][None, :]
            kn, vbb = _ukv(lat, wukv, gkn, 512)
            pkn, pvb = _ukv(cache_b_latent[i].reshape(nb_s * past, -1), wukv, gkn, 512)
            oa_p, ob_p = _attn_prompt(qi, wi, q, qn, qr, cq_rows, kikrb, kb, vb_a, kn, vbb, ckk_p, nb_p, tp, topk_p, 8)
            oa_s, ob_s = _attn_sample(qi, wi, q, qn, qr, cq_rows, kikrb, kb, vb_a, kn, vbb,
                                      cache_a_kidx[i], cache_a_k[i], cache_a_v[i], pkn, cache_b_krope[i], pvb,
                                      ck_s[None, :past], ck_s[None, past:], rp, nb_s, seq_s, topk_s)
            h = _out_proj(h, oa_p, ob_p, oa_s, ob_s, ab_w_out[i].astype(BF16), tm_out)
            kv4 = lambda x: x.reshape(x.shape[0], x.shape[1], A_KV_HEADS, A_HEAD_DIM)
            ab_p.append((kv4(prompt_rows(kf)), kv4(prompt_rows(vf)), prompt_rows(kikr)[..., :64],
                         prompt_rows(lat), prompt_rows(kikr)[..., 64:]))
            ab_s.append((kv4(sample_rows(kf)), kv4(sample_rows(vf)), sample_rows(kikr)[..., :64],
                         sample_rows(lat), sample_rows(kikr)[..., 64:]))
        else:
            z = _cd_proj(h, gmix, cd_w_in[i].astype(BF16), 1024 if r_tot % 1024 == 0 else tile, 1024)
            cwid = C_BLOCKS * C_BLOCK
            gw = jnp.concatenate([c_gate_a_w[i], c_gate_x_w[i]], axis=-1).astype(BF16)
            cargs = (c_conv_w[i], c_conv_b[i][None, :], gw, c_gate_a_b[i][None, :], c_gate_x_b[i][None, :],
                     c_lambda[i][None, :])
            oc_p, hl_p = _rglru(z, cq_rows, jnp.zeros((nb_p, 8, cwid), F32), jnp.zeros((nb_p, 1, cwid), F32),
                                *cargs, 0, nb_p, tp, ROW_BLOCK)
            prev_s = jnp.concatenate([jnp.zeros((nb_s, 8 - (CONV_W - 1), cwid), F32), state_c_conv[i]], axis=1)
            oc_s, hl_s = _rglru(z, cq_rows, prev_s, state_c_h[i][:, None, :], *cargs, rp, nb_s, seq_s, seq_s)
            hl_p, hl_s = hl_p[:, 0], hl_s[:, 0]
            log_gamma = jnp.log(1.0 - 2.0 ** (-5.0 - jnp.arange(D_HEADS, dtype=F32)))
            gdn = d_out_norm[i][None, :]
            od_p, ds_p = _retention(z, *tabs256, jnp.zeros((nb_p, D_HEADS, D_KDIM, D_VDIM), F32), gdn, log_gamma,
                                    2 * cwid, 0, nb_p, tp, ROW_BLOCK)
            od_s, ds_s = _retention(z, *tabs256, state_d_s[i], gdn, log_gamma, 2 * cwid, rp, nb_s, seq_s, seq_s)
            h = _out_proj(h, oc_p, od_p, oc_s, od_s, cd_w_out[i].astype(BF16), tm_out)
            nc = CONV_W - 1
            def seq_tails(row0, nb, t):
                n = min(nc, t)
                rows = row0 + t - n + (jnp.arange(nb, dtype=jnp.int32) * t)[:, None] + jnp.arange(n, dtype=jnp.int32)
                return jnp.take(z, rows.reshape(-1), axis=0)[:, :cwid].reshape(nb, n, cwid)

            cd_p.append((seq_tails(0, nb_p, tp), hl_p, ds_p))
            cd_s.append((jnp.concatenate([state_c_conv[i], seq_tails(rp, nb_s, seq_s)], axis=1)[:, -nc:], hl_s, ds_s))
        last = layer == depth - 1
        h = _mlp(h, norm_mlp[layer][None, :], mlp_up, mlp_down, layer, tm_out if last else tile, 512,
                 split_rows=(rp, rs) if last else None)

    def stack(entries, j):
        return jnp.stack([e[j] for e in entries])

    y_p = h[0].reshape(nb_p, tp, d_model)[:, pad + n_meta:]
    y_s = h[1].reshape(nb_s, seq_s, d_model)
    return (y_p, y_s,
            stack(ab_p, 0), stack(ab_p, 1), stack(ab_p, 2), stack(ab_p, 3), stack(ab_p, 4),
            stack(cd_p, 0), stack(cd_p, 1), stack(cd_p, 2),
            stack(ab_s, 0), stack(ab_s, 1), stack(ab_s, 2), stack(ab_s, 3), stack(ab_s, 4),
            stack(cd_s, 0), stack(cd_s, 1), stack(cd_s, 2))
```

```python
import functools
import math

import jax
import jax.numpy as jnp
import numpy as np
from jax import lax
from jax.experimental import pallas as pl
from jax.experimental.pallas import tpu as pltpu

F32 = jnp.float32
BF16 = jnp.bfloat16

CHUNK = 64
ROPE_THETA = 10000.0
EPS = 1e-6
A_HEADS, A_KV_HEADS, A_HEAD_DIM = 8, 2, 128
IDX_HEADS, IDX_DIM = 16, 64
TOPK_MAX = 256
B_HEADS, B_NOPE, B_ROPE, B_V = 8, 128, 64, 128
C_BLOCKS, C_BLOCK, CONV_W = 8, 128, 4
RG_C = 8.0
D_HEADS, D_KDIM, D_VDIM = 4, 256, 256

LANES = 128
ROW_BLOCK = 128
NEG = -1e30
INT_MIN = -2 ** 31
VMEM_LIMIT = 56 * 1024 * 1024


def _cparams(*sem):
    return pltpu.CompilerParams(dimension_semantics=sem, vmem_limit_bytes=VMEM_LIMIT)


def _const_spec(shape):
    nd = len(shape)
    return pl.BlockSpec(shape, lambda *_: (0,) * nd)


def _rms(x, g):
    ms = jnp.mean(x * x, axis=-1, keepdims=True)
    return x * lax.rsqrt(ms + EPS) * g


def _rms_half(x, g):
    lane = lax.broadcasted_iota(jnp.int32, x.shape, 1)
    lo = lane < 64
    xx = x * x
    s_lo = jnp.sum(jnp.where(lo, xx, 0.0), axis=-1, keepdims=True)
    s_hi = jnp.sum(jnp.where(lo, 0.0, xx), axis=-1, keepdims=True)
    ms = jnp.where(lo, s_lo, s_hi) * (1.0 / 64.0)
    return x * lax.rsqrt(ms + EPS) * g


def _rope128(x, cos, sin_signed):
    return x * cos + pltpu.roll(x, 64, 1) * sin_signed


def _rope64(x, cos, sin_signed):
    lane = lax.broadcasted_iota(jnp.int32, x.shape, 1)
    first = (lane % 64) < 32
    rot = jnp.where(first, pltpu.roll(x, 96, 1), pltpu.roll(x, 32, 1))
    return x * cos + rot * sin_signed


AB_COLS = dict(qa=(0, 1024), ka=(1024, 1280), va=(1280, 1536), qi=(1536, 2560), cq=(2560, 3072),
               ckv=(3072, 3328), kikr=(3328, 3456), wi=(3456, 3584))
AB_IN_PAD = 3584


def _ab_proj_kernel(h_ref, gmix_ref, win_ref, wuq_ref, gaq_ref, gak_ref, gqlat_ref, gkvlat_ref, gqn_ref,
                    gqr_ref, gkr_ref, c128_ref, s128_ref, c64_ref, s64_ref,
                    q_ref, kf_ref, vf_ref, kb_ref, vb_ref, qi_ref, kikr_ref, kikrb_ref, wi_ref, qn_ref, qr_ref,
                    lat_ref):
    xn = _rms(h_ref[...], gmix_ref[...]).astype(BF16)
    c128, s128, c64, s64 = c128_ref[...], s128_ref[...], c64_ref[...], s64_ref[...]

    def proj(name):
        a, b = AB_COLS[name]
        return jnp.dot(xn, win_ref[:, a:b], preferred_element_type=F32)

    z = proj('qa')
    qscale = A_HEAD_DIM ** -0.5
    for h in range(A_HEADS):
        x = _rope128(_rms(z[:, h * 128:(h + 1) * 128], gaq_ref[...]), c128, s128)
        q_ref[:, h * 128:(h + 1) * 128] = (x * qscale).astype(BF16)
    z = proj('ka')
    for h in range(A_KV_HEADS):
        x = _rope128(_rms(z[:, h * 128:(h + 1) * 128], gak_ref[...]), c128, s128)
        kf_ref[:, h * 128:(h + 1) * 128] = x
        kb_ref[:, h * 128:(h + 1) * 128] = x.astype(BF16)
    z = proj('va')
    vf_ref[...] = z
    vb_ref[...] = z.astype(BF16)
    z = proj('qi')
    for p in range(IDX_HEADS // 2):
        x = _rope64(z[:, p * 128:(p + 1) * 128], c64, s64)
        qi_ref[:, p * 128:(p + 1) * 128] = (x * (IDX_DIM ** -0.5)).astype(BF16)
    z = proj('kikr')
    lane = lax.broadcasted_iota(jnp.int32, z.shape, 1)
    x = jnp.where(lane < 64, z, _rms_half(z, gkr_ref[...]))
    x = _rope64(x, c64, s64)
    kikr_ref[...] = x
    kikrb_ref[...] = x.astype(BF16)
    wi_ref[...] = proj('wi') * (IDX_HEADS ** -0.5)
    lat = _rms(proj('ckv'), gkvlat_ref[...])
    lat_ref[...] = lat
    cq = _rms(proj('cq'), gqlat_ref[...]).astype(BF16)
    bscale = (B_NOPE + B_ROPE) ** -0.5
    zq = jnp.dot(cq, wuq_ref[:, :B_HEADS * B_NOPE], preferred_element_type=F32)
    for h in range(B_HEADS):
        x = _rms(zq[:, h * 128:(h + 1) * 128], gqn_ref[...])
        qn_ref[:, h * 128:(h + 1) * 128] = (x * bscale).astype(BF16)
    zq = jnp.dot(cq, wuq_ref[:, B_HEADS * B_NOPE:], preferred_element_type=F32)
    for p in range(B_HEADS // 2):
        x = _rope64(_rms_half(zq[:, p * 128:(p + 1) * 128], gqr_ref[...]), c64, s64)
        qr_ref[:, p * 128:(p + 1) * 128] = (x * bscale).astype(BF16)


def _ab_proj(h, gmix, win, wuq, gaq, gak, gqlat, gkvlat, gqn, gqr2, gkr2, tabs, tm):
    r, d = h.shape
    row = lambda w: pl.BlockSpec((tm, w), lambda i: (i, 0))
    outs = [(1024, BF16), (256, F32), (256, F32), (256, BF16), (256, BF16), (1024, BF16), (128, F32),
            (128, BF16), (128, F32), (1024, BF16), (512, BF16), (256, F32)]
    return pl.pallas_call(
        _ab_proj_kernel,
        grid=(r // tm,),
        in_specs=[row(d), _const_spec(gmix.shape), _const_spec(win.shape), _const_spec(wuq.shape),
                  _const_spec(gaq.shape), _const_spec(gak.shape), _const_spec(gqlat.shape),
                  _const_spec(gkvlat.shape), _const_spec(gqn.shape), _const_spec(gqr2.shape),
                  _const_spec(gkr2.shape), row(128), row(128), row(128), row(128)],
        out_specs=[row(w) for w, _ in outs],
        out_shape=[jax.ShapeDtypeStruct((r, w), dt) for w, dt in outs],
        compiler_params=_cparams("parallel"),
        name="ab_proj",
    )(h, gmix, win, wuq, gaq, gak, gqlat, gkvlat, gqn, gqr2, gkr2, *tabs)


def _ukv_kernel(lat_ref, w_ref, gkn_ref, kn_ref, vb_ref):
    lat = lat_ref[...].astype(BF16)
    z = jnp.dot(lat, w_ref[:, :B_HEADS * B_NOPE], preferred_element_type=F32)
    for h in range(B_HEADS):
        kn_ref[:, h * 128:(h + 1) * 128] = _rms(z[:, h * 128:(h + 1) * 128], gkn_ref[...]).astype(BF16)
    vb_ref[...] = jnp.dot(lat, w_ref[:, B_HEADS * B_NOPE:], preferred_element_type=F32).astype(BF16)


def _ukv(lat, w, gkn, tm):
    n = lat.shape[0]
    row = lambda wd: pl.BlockSpec((tm, wd), lambda i: (i, 0))
    return pl.pallas_call(
        _ukv_kernel,
        grid=(n // tm,),
        in_specs=[row(lat.shape[1]), _const_spec(w.shape), _const_spec(gkn.shape)],
        out_specs=[row(1024), row(1024)],
        out_shape=[jax.ShapeDtypeStruct((n, 1024), BF16)] * 2,
        compiler_params=_cparams("parallel"),
        name="mla_ukv",
    )(lat, w, gkn)


def _dot_t(a, b):
    return lax.dot_general(a, b, (((1,), (1,)), ((), ())), preferred_element_type=F32)


def _float_key(x):
    bits = pltpu.bitcast(x, jnp.int32)
    return bits ^ ((bits >> 31) & jnp.int32(0x7FFFFFFF))


def _counts_ge16(vals, cands):
    rows = cands[0].shape[0]
    c16 = [jnp.broadcast_to(c, (rows, LANES)).astype(jnp.int16) for c in cands]
    one, zero = jnp.int16(1), jnp.int16(0)
    accs = [jnp.zeros((rows, LANES), jnp.int16) for _ in cands]
    rest = [jnp.zeros((rows, 1), F32) for _ in cands]
    for v in vals:
        n_full = v.shape[1] // LANES * LANES
        for s0 in range(0, n_full, LANES):
            blk = v[:, s0:s0 + LANES]
            accs = [a + jnp.where(blk >= c, one, zero) for a, c in zip(accs, c16)]
        if n_full < v.shape[1]:
            blk = v[:, n_full:]
            w = blk.shape[1]
            rest = [r + jnp.sum(jnp.where(blk >= c[:, :w], one, zero).astype(F32), axis=-1, keepdims=True)
                    for r, c in zip(rest, c16)]
    return [jnp.sum(a.astype(F32), axis=-1, keepdims=True) + r for a, r in zip(accs, rest)]


def _search16(vals, k):
    rows = vals[0].shape[0]

    def body(i, t):
        step = jnp.left_shift(jnp.int32(1), jnp.int32(14) - 2 * i)
        cnts = _counts_ge16(vals, [t + j * step for j in (1, 2, 3)])
        digit = functools.reduce(jnp.add, [jnp.where(c >= float(k), 1, 0) for c in cnts])
        return t + digit * step

    return lax.fori_loop(0, 8, body, jnp.full((rows, 1), -32768, jnp.int32))


def _kth_largest(keys, k):
    his = [(key >> 16).astype(jnp.int16) for key in keys]
    los = [((key & 0xFFFF) - 32768).astype(jnp.int16) for key in keys]
    h = _search16(his, k)
    top, bottom = jnp.int16(32767), jnp.int16(-32768)
    los2 = []
    for hi, lo in zip(his, los):
        h16 = jnp.broadcast_to(h, hi.shape).astype(jnp.int16)
        los2.append(jnp.where(hi == h16, lo, jnp.where(hi > h16, top, bottom)))
    l = _search16(los2, k)
    return (h << 16) | ((l + 32768) & 0xFFFF)


def _kv_head(x, c):
    return x[c] if isinstance(x, tuple) else x[:, c * 128:(c + 1) * 128]


def _tie_break(keys, thr, topk, bias_refs):
    nq = keys[0].shape[0]
    n_gt = functools.reduce(jnp.add, [jnp.sum(jnp.where(key > thr, 1.0, 0.0), axis=-1, keepdims=True)
                                      for key in keys])
    need = float(topk) - n_gt
    r_i = lax.broadcasted_iota(jnp.int32, (LANES, LANES), 0)
    c_i = lax.broadcasted_iota(jnp.int32, (LANES, LANES), 1)
    before = jnp.where(r_i < c_i, 1.0, 0.0).astype(BF16)
    seen = jnp.zeros((nq, 1), F32)
    for key, ref in zip(keys, bias_refs):
        for s0 in range(0, key.shape[1], LANES):
            blk = key[:, s0:s0 + LANES]
            w = blk.shape[1]
            eq = jnp.where(blk == thr, 1.0, 0.0)
            rank = seen + jnp.dot(eq.astype(BF16), before[:w, :w], preferred_element_type=F32)
            sel = jnp.logical_or(blk > thr, jnp.logical_and(eq > 0.0, rank < need))
            ref[:, s0:s0 + w] = jnp.where(jnp.logical_and(sel, blk > INT_MIN), 0.0, NEG)
            seen = seen + jnp.sum(eq, axis=-1, keepdims=True)


def _dsa_core(qi, wi, q, cq, pieces, topk, bias_refs):
    nq = qi.shape[0]
    keys = []
    for ki, _, _, ck in pieces:
        score = jnp.zeros((nq, ki.shape[0]), F32)
        for h in range(IDX_HEADS):
            s_h = _dot_t(qi[:, h * 64:(h + 1) * 64], ki)
            score = score + jnp.maximum(s_h, 0.0) * wi[:, h:h + 1]
        keys.append(jnp.where(ck <= cq, _float_key(score), INT_MIN))
    thr = _kth_largest(keys, topk)
    n_ge = jnp.zeros((nq, 1), F32)
    for key, ref in zip(keys, bias_refs):
        ge = key >= thr
        ref[...] = jnp.where(jnp.logical_and(ge, key > INT_MIN), 0.0, NEG)
        n_ge = n_ge + jnp.sum(jnp.where(ge, 1.0, 0.0), axis=-1, keepdims=True)
    tied = jnp.max(jnp.where(jnp.logical_and(n_ge > float(topk), thr > INT_MIN), 1.0, 0.0))
    pl.when(tied > 0.0)(functools.partial(_tie_break, keys, thr, topk, bias_refs))
    group = A_HEADS // A_KV_HEADS
    biases = [jnp.concatenate([ref[...]] * group, axis=0) for ref in bias_refs]
    outs = []
    for c in range(A_KV_HEADS):
        qg = jnp.concatenate([q[:, (c * group + g) * 128:(c * group + g + 1) * 128] for g in range(group)], axis=0)
        ss = [_dot_t(qg, _kv_head(k, c)) + bias for (_, k, _, _), bias in zip(pieces, biases)]
        m = functools.reduce(jnp.maximum, [jnp.max(s, axis=-1, keepdims=True) for s in ss])
        ps = [jnp.exp(s - m) for s in ss]
        den = functools.reduce(jnp.add, [jnp.sum(p, axis=-1, keepdims=True) for p in ps])
        o = functools.reduce(jnp.add, [jnp.dot(p.astype(BF16), _kv_head(v, c), preferred_element_type=F32)
                                       for p, (_, _, v, _) in zip(ps, pieces)])
        o = o / den
        outs.extend([o[g * nq:(g + 1) * nq] for g in range(group)])
    return jnp.concatenate(outs, axis=-1)


def _mla_core(qn, qr, cq, pieces):
    outs = []
    biases = [jnp.where(ck <= cq, 0.0, NEG) for _, _, _, ck in pieces]
    for h in range(B_HEADS):
        ss = []
        for (kn, kr, _, _), bias in zip(pieces, biases):
            s = _dot_t(qn[:, h * 128:(h + 1) * 128], kn[:, h * 128:(h + 1) * 128])
            s = s + _dot_t(qr[:, h * 64:(h + 1) * 64], kr)
            ss.append(s + bias)
        m = functools.reduce(jnp.maximum, [jnp.max(s, axis=-1, keepdims=True) for s in ss])
        ps = [jnp.exp(s - m) for s in ss]
        den = functools.reduce(jnp.add, [jnp.sum(p, axis=-1, keepdims=True) for p in ps])
        o = functools.reduce(jnp.add, [jnp.dot(p.astype(BF16), vb[:, h * 128:(h + 1) * 128],
                                               preferred_element_type=F32)
                                       for p, (_, _, vb, _) in zip(ps, pieces)])
        outs.append(o / den)
    return jnp.concatenate(outs, axis=-1)


def _attn_prompt_kernel(qi_ref, wi_ref, q_ref, qn_ref, qr_ref, cq_ref, kikr_ref, k_ref, v_ref, kn_ref, vb_ref,
                        ck_ref, *rest, topk, s_len):
    oa_ref, ob_ref, bias_ref = rest[-3:]
    cq = cq_ref[:, 0:1]
    valid = cq >= 0
    ck = ck_ref[:, :s_len]
    kikr = kikr_ref[:s_len, :]
    oa = _dsa_core(qi_ref[...], wi_ref[...], q_ref[...], cq,
                   [(kikr[:, :64], k_ref[:s_len, :], v_ref[:s_len, :], ck)], topk, [bias_ref])
    oa_ref[...] = jnp.where(valid, oa, 0.0).astype(BF16)
    ob = _mla_core(qn_ref[...], qr_ref[...], cq, [(kn_ref[:s_len, :], kikr[:, 64:], vb_ref[:s_len, :], ck)])
    ob_ref[...] = jnp.where(valid, ob, 0.0).astype(BF16)


def _attn_prompt(qi, wi, q, qn, qr, cq, kikr, k, v, kn, vb, ck, nb, tp, topk, n_ranges):
    tq = ROW_BLOCK
    nj = tp // tq
    edges = sorted({-(-nj * r // n_ranges) for r in range(n_ranges + 1)})
    kspec = lambda w: pl.BlockSpec((tp, w), lambda b, j: (b, 0))
    outs = ()
    for j0, j1 in zip(edges[:-1], edges[1:]):
        qspec = lambda w, j0=j0: pl.BlockSpec((tq, w), lambda b, j: (b * nj + j0 + j, 0))
        n_in = 12
        outs = pl.pallas_call(
            functools.partial(_attn_prompt_kernel, topk=topk, s_len=j1 * tq),
            grid=(nb, j1 - j0),
            in_specs=[qspec(1024), qspec(128), qspec(1024), qspec(1024), qspec(512), qspec(128),
                      kspec(128), kspec(256), kspec(256), kspec(1024), kspec(1024), _const_spec(ck.shape)]
                     + [pl.BlockSpec(memory_space=pl.ANY)] * len(outs),
            out_specs=[qspec(1024), qspec(1024)],
            out_shape=[jax.ShapeDtypeStruct((nb * tp, 1024), BF16)] * 2,
            scratch_shapes=[pltpu.VMEM((tq, j1 * tq), F32)],
            input_output_aliases={n_in + o: o for o in range(len(outs))},
            compiler_params=_cparams("parallel", "arbitrary"),
            name="attn_prompt",
        )(qi, wi, q, qn, qr, cq, kikr, k, v, kn, vb, ck, *outs)
    return outs


def _attn_sample_kernel(qi_ref, wi_ref, q_ref, qn_ref, qr_ref, cq_ref, kikr_ref, k_ref, v_ref, kn_ref, vb_ref,
                        pki_ref, pk_ref, pv_ref, pkn_ref, pkr_ref, pvb_ref, ckp_ref, ckn_ref,
                        oa_ref, ob_ref, biasp_ref, biasn_ref, *, topk):
    cq = cq_ref[:, 0:1]
    ckp, ckn = ckp_ref[...], ckn_ref[...]
    kikr = kikr_ref[...]
    past = pki_ref.shape[1]
    heads = lambda ref: tuple(ref[pl.ds(c, past, stride=A_KV_HEADS), :].astype(BF16) for c in range(A_KV_HEADS))
    past_a = (pki_ref[0].astype(BF16), heads(pk_ref), heads(pv_ref), ckp)
    new_a = (kikr[:, :64], k_ref[...], v_ref[...], ckn)
    oa_ref[...] = _dsa_core(qi_ref[...], wi_ref[...], q_ref[...], cq, [past_a, new_a], topk,
                            [biasp_ref, biasn_ref]).astype(BF16)
    past_b = (pkn_ref[...], pkr_ref[0].astype(BF16), pvb_ref[...], ckp)
    new_b = (kn_ref[...], kikr[:, 64:], vb_ref[...], ckn)
    ob_ref[...] = _mla_core(qn_ref[...], qr_ref[...], cq, [past_b, new_b]).astype(BF16)


def _attn_sample(qi, wi, q, qn, qr, cq, kikr, k, v, kn, vb, pki, pk, pv, pkn, pkr, pvb, ckp, ckn,
                 row0, nb, ts, topk):
    past = pk.shape[1]
    blk0 = row0 // ts
    nspec = lambda w: pl.BlockSpec((ts, w), lambda b: (blk0 + b, 0))
    pspec = lambda w: pl.BlockSpec((1, past, w), lambda b: (b, 0, 0))
    p2spec = lambda w: pl.BlockSpec((past, w), lambda b: (b, 0))
    kvspec = pl.BlockSpec((None, past * A_KV_HEADS, A_HEAD_DIM), lambda b: (b, 0, 0))
    pk = pk.reshape(nb, past * A_KV_HEADS, A_HEAD_DIM)
    pv = pv.reshape(nb, past * A_KV_HEADS, A_HEAD_DIM)
    return pl.pallas_call(
        functools.partial(_attn_sample_kernel, topk=topk),
        grid=(nb,),
        in_specs=[nspec(1024), nspec(128), nspec(1024), nspec(1024), nspec(512), nspec(128),
                  nspec(128), nspec(256), nspec(256), nspec(1024), nspec(1024),
                  pspec(64), kvspec, kvspec, p2spec(1024), pspec(64), p2spec(1024),
                  _const_spec(ckp.shape), _const_spec(ckn.shape)],
        out_specs=[pl.BlockSpec((ts, 1024), lambda b: (b, 0))] * 2,
        out_shape=[jax.ShapeDtypeStruct((nb * ts, 1024), BF16)] * 2,
        scratch_shapes=[pltpu.VMEM((ts, past), F32), pltpu.VMEM((ts, ts), F32)],
        compiler_params=_cparams("parallel"),
        name="attn_sample",
    )(qi, wi, q, qn, qr, cq, kikr, k, v, kn, vb, pki, pk, pv, pkn, pkr, pvb, ckp, ckn)


def _out_proj_kernel(h_ref, oap_ref, obp_ref, oas_ref, obs_ref, w_ref, o_ref, *, n_p, n_ps):
    half = oap_ref.shape[1]
    i = pl.program_id(0)

    def project(oa_ref, ob_ref):
        y = jnp.dot(oa_ref[...], w_ref[:half, :], preferred_element_type=F32)
        y = y + jnp.dot(ob_ref[...], w_ref[half:, :], preferred_element_type=F32)
        o_ref[...] = h_ref[...] + y

    pl.when(i < n_p)(functools.partial(project, oap_ref, obp_ref))
    pl.when(jnp.logical_and(i >= n_p, i < n_ps))(functools.partial(project, oas_ref, obs_ref))

    @pl.when(i >= n_ps)
    def _():
        o_ref[...] = h_ref[...]


def _out_proj(h, oa_p, ob_p, oa_s, ob_s, w, tm):
    r, d = h.shape
    n_p, n_s = oa_p.shape[0] // tm, oa_s.shape[0] // tm
    assert oa_p.shape[0] % tm == 0 and oa_s.shape[0] % tm == 0
    row = lambda wd: pl.BlockSpec((tm, wd), lambda i: (i, 0))
    pspec = lambda wd: pl.BlockSpec((tm, wd), lambda i: (jnp.minimum(i, n_p - 1), 0))
    sspec = lambda wd: pl.BlockSpec((tm, wd), lambda i: (jnp.clip(i - n_p, 0, n_s - 1), 0))
    return pl.pallas_call(
        functools.partial(_out_proj_kernel, n_p=n_p, n_ps=n_p + n_s),
        grid=(r // tm,),
        in_specs=[row(d), pspec(oa_p.shape[1]), pspec(ob_p.shape[1]), sspec(oa_s.shape[1]), sspec(ob_s.shape[1]),
                  _const_spec(w.shape)],
        out_specs=row(d),
        out_shape=jax.ShapeDtypeStruct((r, d), F32),
        compiler_params=_cparams("parallel"),
        name="out_proj",
    )(h, oa_p, ob_p, oa_s, ob_s, w)


def _mlp_kernel(h_ref, g_ref, wup_ref, wdn_ref, *rest, n_split, n_end):
    *o_refs, xn_ref, acc_ref = rest
    i, f = pl.program_id(0), pl.program_id(1)

    @pl.when(f == 0)
    def _():
        x = h_ref[...]
        xn_ref[...] = _rms(x, g_ref[...]).astype(BF16)
        acc_ref[...] = x

    u = jnp.dot(xn_ref[...], wup_ref[...], preferred_element_type=F32)
    u = jnp.maximum(u, 0.0)
    acc_ref[...] += jnp.dot((u * u).astype(BF16), wdn_ref[...], preferred_element_type=F32)

    @pl.when(f == pl.num_programs(1) - 1)
    def _():
        if len(o_refs) == 1:
            o_refs[0][...] = acc_ref[...]
        else:
            @pl.when(i < n_split)
            def _():
                o_refs[0][...] = acc_ref[...]

            @pl.when(jnp.logical_and(i >= n_split, i < n_end))
            def _():
                o_refs[1][...] = acc_ref[...]


def _mlp(h, g, wup, wdn, layer, tm, tf, split_rows=None):
    r, d = h.shape
    dff = wup.shape[2]
    n_end = r // tm
    if split_rows is None:
        n_split = 0
        out_specs = [pl.BlockSpec((tm, d), lambda i, f: (i, 0))]
        out_shape = [jax.ShapeDtypeStruct((r, d), F32)]
    else:
        a, b = split_rows
        assert a % tm == 0 and b % tm == 0
        n_split, n_b = a // tm, b // tm
        n_end = n_split + n_b
        out_specs = [pl.BlockSpec((tm, d), lambda i, f: (jnp.minimum(i, n_split - 1), 0)),
                     pl.BlockSpec((tm, d), lambda i, f: (jnp.clip(i - n_split, 0, n_b - 1), 0))]
        out_shape = [jax.ShapeDtypeStruct((a, d), F32), jax.ShapeDtypeStruct((b, d), F32)]
    res = pl.pallas_call(
        functools.partial(_mlp_kernel, n_split=n_split, n_end=n_end),
        grid=(r // tm, dff // tf),
        in_specs=[pl.BlockSpec((tm, d), lambda i, f: (i, 0)), pl.BlockSpec((1, d), lambda i, f: (0, 0)),
                  pl.BlockSpec((None, d, tf), lambda i, f: (layer, 0, f)),
                  pl.BlockSpec((None, tf, d), lambda i, f: (layer, f, 0))],
        out_specs=out_specs,
        out_shape=out_shape,
        scratch_shapes=[pltpu.VMEM((tm, d), BF16), pltpu.VMEM((tm, d), F32)],
        compiler_params=_cparams("parallel" if split_rows is None else "arbitrary", "arbitrary"),
        name="mlp",
    )(h, g, wup, wdn)
    return res[0] if split_rows is None else res


def _cd_proj_kernel(h_ref, g_ref, w_ref, z_ref, xn_ref):
    @pl.when(pl.program_id(1) == 0)
    def _():
        xn_ref[...] = _rms(h_ref[...], g_ref[...]).astype(BF16)

    z_ref[...] = jnp.dot(xn_ref[...], w_ref[...], preferred_element_type=F32)


def _cd_proj(h, g, w, tm, tn):
    r, d = h.shape
    n = w.shape[1]
    return pl.pallas_call(
        _cd_proj_kernel,
        grid=(r // tm, n // tn),
        in_specs=[pl.BlockSpec((tm, d), lambda i, j: (i, 0)), pl.BlockSpec((1, d), lambda i, j: (0, 0)),
                  pl.BlockSpec((d, tn), lambda i, j: (0, j))],
        out_specs=pl.BlockSpec((tm, tn), lambda i, j: (i, j)),
        out_shape=jax.ShapeDtypeStruct((r, n), F32),
        scratch_shapes=[pltpu.VMEM((tm, d), BF16)],
        compiler_params=_cparams("parallel", "arbitrary"),
        name="cd_proj",
    )(h, g, w)


def _rglru_kernel(cx_ref, cg_ref, valid_ref, prev0_ref, h0_ref, cw_ref, cb_ref, gw_ref, gab_ref, gxb_ref, lam_ref,
                  o_ref, hl_ref, prev_s, h_s, a_s, b_s):
    @pl.when(pl.program_id(1) == 0)
    def _():
        prev_s[...] = prev0_ref[0]
        h_s[...] = jnp.broadcast_to(h0_ref[0], h_s.shape)

    cx = cx_ref[...]
    tm = cx.shape[0]
    full = jnp.concatenate([prev_s[...], cx], axis=0)
    prev_s[...] = cx[tm - 8:, :]
    xc = cb_ref[...] + cx * cw_ref[CONV_W - 1:CONV_W, :]
    for j in range(CONV_W - 1):
        sh = CONV_W - 1 - j
        xc = xc + full[8 - sh:8 - sh + tm, :] * cw_ref[j:j + 1, :]
    valid = valid_ref[:, 0:1] >= 0
    softplus = jnp.logaddexp(-lam_ref[...], 0.0)
    for n in range(C_BLOCKS):
        sl = slice(n * C_BLOCK, (n + 1) * C_BLOCK)
        xb = xc[:, sl]
        gates = jnp.dot(xb.astype(BF16), gw_ref[n], preferred_element_type=F32)
        r = jax.nn.sigmoid(gates[:, :C_BLOCK] + gab_ref[:, sl])
        ig = jax.nn.sigmoid(gates[:, C_BLOCK:] + gxb_ref[:, sl])
        log_a = -RG_C * r * softplus[:, sl]
        a = jnp.exp(log_a)
        a_s[:, sl] = a
        b = jnp.sqrt(1.0 - a * a) * (ig * xb)
        b_s[:, sl] = jnp.where(valid, b, 0.0)

    rid = lax.broadcasted_iota(jnp.int32, h_s.shape, 0)

    def group(g, h):
        rows = pl.ds(pl.multiple_of(g * 8, 8), 8)
        ca, cb = a_s[rows, :], b_s[rows, :]
        for s in (1, 2, 4):
            m = rid >= s
            cb = jnp.where(m, ca * pltpu.roll(cb, s, 0) + cb, cb)
            ca = jnp.where(m, ca * pltpu.roll(ca, s, 0), ca)
        hh = ca * h + cb
        b_s[rows, :] = hh
        return jnp.broadcast_to(hh[7:8, :], hh.shape)

    h = lax.fori_loop(0, tm // 8, group, h_s[...])
    h_s[...] = h
    hl_ref[0] = h[0:1, :]
    o_ref[...] = (b_s[...] * jax.nn.gelu(cg_ref[...])).astype(BF16)


def _rglru(z, valid, prev0, h0, cw, cb, gw, gab, gxb, lam, row0, nb, t, tm):
    cwid = C_BLOCKS * C_BLOCK
    nj = t // tm
    b0 = row0 // tm
    zspec = lambda c: pl.BlockSpec((tm, cwid), lambda b, j: (b0 + b * nj + j, c))
    return pl.pallas_call(
        _rglru_kernel,
        grid=(nb, nj),
        in_specs=[zspec(0), zspec(1), pl.BlockSpec((tm, LANES), lambda b, j: (b0 + b * nj + j, 0)),
                  pl.BlockSpec((1, 8, cwid), lambda b, j: (b, 0, 0)), pl.BlockSpec((1, 1, cwid), lambda b, j: (b, 0, 0)),
                  _const_spec(cw.shape), _const_spec(cb.shape), _const_spec(gw.shape), _const_spec(gab.shape),
                  _const_spec(gxb.shape), _const_spec(lam.shape)],
        out_specs=[pl.BlockSpec((tm, cwid), lambda b, j: (b * nj + j, 0)),
                   pl.BlockSpec((1, 1, cwid), lambda b, j: (b, 0, 0))],
        out_shape=[jax.ShapeDtypeStruct((nb * t, cwid), BF16), jax.ShapeDtypeStruct((nb, 1, cwid), F32)],
        scratch_shapes=[pltpu.VMEM((8, cwid), F32), pltpu.VMEM((8, cwid), F32), pltpu.VMEM((tm, cwid), F32),
                        pltpu.VMEM((tm, cwid), F32)],
        compiler_params=_cparams("parallel", "arbitrary"),
        name="rglru",
    )(z, z, valid, prev0, h0, cw, cb, gw, gab, gxb, lam)


def _retention_kernel(q_ref, k_ref, v_ref, g_ref, cos_ref, sin_ref, s0_ref, gdn_ref, lg_ref,
                      o_ref, s_ref, *, blk):
    @pl.when(pl.program_id(1) == 0)
    def _():
        s_ref[...] = s0_ref[...]

    cos, sin_signed = cos_ref[...], sin_ref[...]

    def rope(x):
        return x * cos + pltpu.roll(x, D_KDIM // 2, 1) * sin_signed

    ri = lax.broadcasted_iota(jnp.int32, (blk, blk), 0)
    ci = lax.broadcasted_iota(jnp.int32, (blk, blk), 1)
    diff = (ri - ci).astype(F32)
    rows = lax.broadcasted_iota(jnp.int32, (blk, 1), 0).astype(F32)
    for hd in range(D_HEADS):
        lg = lg_ref[hd]
        qsl = slice(hd * D_KDIM, (hd + 1) * D_KDIM)
        vsl = slice(hd * D_VDIM, (hd + 1) * D_VDIM)
        q = rope(q_ref[:, qsl])
        k = rope(k_ref[:, qsl]) * (D_KDIM ** -0.5)
        decay = jnp.where(diff >= 0, jnp.exp(lg * jnp.maximum(diff, 0.0)), 0.0)
        qb, kb, vb = q.astype(BF16), k.astype(BF16), v_ref[:, vsl].astype(BF16)
        att = _dot_t(qb, kb) * decay
        o = jnp.dot(att.astype(BF16), vb, preferred_element_type=F32)
        s_prev = s_ref[0, hd]
        o = o + jnp.dot(qb, s_prev.astype(BF16), preferred_element_type=F32) * jnp.exp(lg * (rows + 1.0))
        kdec = (k * jnp.exp(lg * (blk - 1.0 - rows))).astype(BF16)
        s_ref[0, hd] = jnp.exp(lg * blk) * s_prev + lax.dot_general(kdec, vb, (((0,), (0,)), ((), ())),
                                                                    preferred_element_type=F32)
        od = _rms(o, gdn_ref[...]) * jax.nn.silu(g_ref[:, vsl])
        o_ref[:, vsl] = od.astype(BF16)


def _retention(z, cos, sin_signed, s0, gdn, log_gamma, col0, row0, nb, t, blk):
    nj = t // blk
    b0 = row0 // blk
    wid = D_HEADS * D_KDIM
    c0 = col0 // wid
    zspec = lambda c: pl.BlockSpec((blk, wid), lambda b, j: (b0 + b * nj + j, c0 + c))
    tspec = pl.BlockSpec((blk, D_KDIM), lambda b, j: (b0 + b * nj + j, 0))
    sspec = pl.BlockSpec((1, D_HEADS, D_KDIM, D_VDIM), lambda b, j: (b, 0, 0, 0))
    return pl.pallas_call(
        functools.partial(_retention_kernel, blk=blk),
        grid=(nb, nj),
        in_specs=[zspec(0), zspec(1), zspec(2), zspec(3), tspec, tspec, sspec, _const_spec(gdn.shape),
                  pl.BlockSpec(memory_space=pltpu.SMEM)],
        out_specs=[pl.BlockSpec((blk, wid), lambda b, j: (b * nj + j, 0)), sspec],
        out_shape=[jax.ShapeDtypeStruct((nb * t, wid), BF16), jax.ShapeDtypeStruct(s0.shape, F32)],
        compiler_params=_cparams("parallel", "arbitrary"),
        name="retention",
    )(z, z, z, z, cos, sin_signed, s0, gdn, log_gamma)


def _rope_tables(pos_groups, reps_groups, tail, d, width):
    inv = ROPE_THETA ** (-jnp.arange(0, d, 2, dtype=F32) / d)
    cos_rows, sin_rows = [], []
    for pos, reps in zip(pos_groups, reps_groups):
        ang = pos.astype(F32)[:, None] * inv[None, :]
        cos, sin = lax.optimization_barrier((jnp.cos(ang), jnp.sin(ang)))
        cos_rows.append(jnp.tile(jnp.tile(jnp.concatenate([cos, cos], axis=-1), (1, width // d)), (reps, 1)))
        sin_rows.append(jnp.tile(jnp.tile(jnp.concatenate([-sin, sin], axis=-1), (1, width // d)), (reps, 1)))
    cos_rows.append(jnp.ones((tail, width), F32))
    sin_rows.append(jnp.zeros((tail, width), F32))
    return jnp.concatenate(cos_rows, axis=0), jnp.concatenate(sin_rows, axis=0)


def kernel(x_prompt, x_sample, cache_a_k, cache_a_v, cache_a_kidx, cache_b_latent, cache_b_krope, state_c_conv,
           state_c_h, state_d_s, meta_tokens, norm_mix, norm_mlp, ab_w_in, ab_w_out, a_q_norm, a_k_norm,
           b_q_lat_norm, b_w_uq, b_kv_lat_norm, b_w_ukv, b_qn_norm, b_qr_norm, b_kn_norm, b_kr_norm, cd_w_in,
           cd_w_out, c_conv_w, c_conv_b, c_gate_a_w, c_gate_a_b, c_gate_x_w, c_gate_x_b, c_lambda, d_out_norm,
           mlp_w_up, mlp_w_down):
    nb_p, seq_p, d_model = x_prompt.shape
    nb_s, seq_s, _ = x_sample.shape
    past = cache_a_k.shape[2]
    n_meta = meta_tokens.shape[0]
    depth = norm_mix.shape[0]
    t_real = n_meta + seq_p
    tp = -(-t_real // ROW_BLOCK) * ROW_BLOCK
    pad = tp - t_real
    rp = nb_p * tp
    rs = nb_s * seq_s
    assert seq_s % 8 == 0 and rp % seq_s == 0 and past % 8 == 0
    assert seq_p % ROW_BLOCK == 0 and ROW_BLOCK % CHUNK == 0
    r_real = rp + rs
    tile = 512
    r_tot = -(-r_real // tile) * tile
    tm_out = math.gcd(tile, rp, rs)
    topk_p = min(TOPK_MAX, seq_p // 4)
    topk_s = min(TOPK_MAX, (past + seq_s) // 4)

    pos_p = jnp.concatenate([jnp.zeros((pad,), jnp.int32), jnp.arange(t_real, dtype=jnp.int32)])
    ck_p = jnp.concatenate([jnp.zeros((n_meta,), jnp.int32), 1 + jnp.arange(seq_p, dtype=jnp.int32) // CHUNK])
    cq_p = jnp.concatenate([jnp.full((pad,), -1, jnp.int32), ck_p])
    ckk_p = jnp.concatenate([jnp.full((pad,), 2 ** 30, jnp.int32), ck_p])[None, :]
    pos_s_all = jnp.arange(past + seq_s, dtype=jnp.int32)
    ck_s = pos_s_all // CHUNK
    pos_s = pos_s_all[past:]
    tail = r_tot - r_real
    cq_rows =jnp.concatenate([jnp.tile(cq_p, nb_p), jnp.tile(ck_s[past:], nb_s), jnp.full((tail,), -1, jnp.int32)])
    cq_rows = jnp.broadcast_to(cq_rows[:, None], (r_tot, LANES))
    tabs128 = _rope_tables((pos_p, pos_s), (nb_p, nb_s), tail, 128, 128)
    tabs64 = _rope_tables((pos_p, pos_s), (nb_p, nb_s), tail, 64, 128)
    tabs256 = _rope_tables((pos_p, pos_s), (nb_p, nb_s), tail, 256, 256)

    head = jnp.concatenate([jnp.zeros((pad, d_model), F32), meta_tokens.astype(F32)], axis=0)
    pieces = [p for b in range(nb_p) for p in (head, x_prompt[b])]
    h = jnp.concatenate(pieces + [x_sample.reshape(rs, d_model), jnp.zeros((tail, d_model), F32)], axis=0)
    mlp_up, mlp_down = mlp_w_up.astype(BF16), mlp_w_down.astype(BF16)

    def prompt_rows(x):
        return x[:rp].reshape(nb_p, tp, -1)[:, pad:]

    def sample_rows(x):
        return x[rp:rp + rs].reshape(nb_s, seq_s, -1)

    ab_p, ab_s, cd_p, cd_s = [], [], [], []
    for layer in range(depth):
        i = layer // 2
        gmix = norm_mix[layer][None, :]
        if layer % 2 == 0:
            w = ab_w_in[i]
            offs = np.cumsum([0, 1024, 256, 256, 1024, 64, 16, 512, 256, 64])
            sec = lambda n: w[:, offs[n]:offs[n + 1]]
            win = jnp.concatenate([sec(0), sec(1), sec(2), sec(3), sec(6), sec(7), sec(4), sec(8), sec(5),
                                   jnp.zeros((d_model, AB_IN_PAD - 3472), F32)], axis=1).astype(BF16)
            wuq = b_w_uq[i].reshape(-1, B_HEADS, B_NOPE + B_ROPE)
            wuq = jnp.concatenate([wuq[:, :, :B_NOPE].reshape(-1, B_HEADS * B_NOPE),
                                   wuq[:, :, B_NOPE:].reshape(-1, B_HEADS * B_ROPE)], axis=1).astype(BF16)
            wukv = b_w_ukv[i].reshape(-1, B_HEADS, B_NOPE + B_V)
            wukv = jnp.concatenate([wukv[:, :, :B_NOPE].reshape(-1, B_HEADS * B_NOPE),
                                    wukv[:, :, B_NOPE:].reshape(-1, B_HEADS * B_V)], axis=1).astype(BF16)
            two = lambda g: jnp.concatenate([g, g])[None, :]
            gkr2 = jnp.concatenate([jnp.ones((64,), F32), b_kr_norm[i]])[None, :]
            (q, kf, vf, kb, vb_a, qi, kikr, kikrb, wi, qn, qr, lat) = _ab_proj(
                h, gmix, win, wuq, a_q_norm[i][None, :], a_k_norm[i][None, :], b_q_lat_norm[i][None, :],
                b_kv_lat_norm[i][None, :], b_qn_norm[i][None, :], two(b_qr_norm[i]), gkr2,
                tabs128 + tabs64, tile)
            gkn = b_kn_norm[i][None, :]
            kn, vbb = _ukv(lat, wukv, gkn, 512)
            pkn, pvb = _ukv(cache_b_latent[i].reshape(nb_s * past, -1), wukv, gkn, 512)
            oa_p, ob_p = _attn_prompt(qi, wi, q, qn, qr, cq_rows, kikrb, kb, vb_a, kn, vbb, ckk_p, nb_p, tp, topk_p, 8)
            oa_s, ob_s = _attn_sample(qi, wi, q, qn, qr, cq_rows, kikrb, kb, vb_a, kn, vbb,
                                      cache_a_kidx[i], cache_a_k[i], cache_a_v[i], pkn, cache_b_krope[i], pvb,
                                      ck_s[None, :past], ck_s[None, past:], rp, nb_s, seq_s, topk_s)
            h = _out_proj(h, oa_p, ob_p, oa_s, ob_s, ab_w_out[i].astype(BF16), tm_out)
            kv4 = lambda x: x.reshape(x.shape[0], x.shape[1], A_KV_HEADS, A_HEAD_DIM)
            ab_p.append((kv4(prompt_rows(kf)), kv4(prompt_rows(vf)), prompt_rows(kikr)[..., :64],
                         prompt_rows(lat), prompt_rows(kikr)[..., 64:]))
            ab_s.append((kv4(sample_rows(kf)), kv4(sample_rows(vf)), sample_rows(kikr)[..., :64],
                         sample_rows(lat), sample_rows(kikr)[..., 64:]))
        else:
            z = _cd_proj(h, gmix, cd_w_in[i].astype(BF16), 1024 if r_tot % 1024 == 0 else tile, 1024)
            cwid = C_BLOCKS * C_BLOCK
            gw = jnp.concatenate([c_gate_a_w[i], c_gate_x_w[i]], axis=-1).astype(BF16)
            cargs = (c_conv_w[i], c_conv_b[i][None, :], gw, c_gate_a_b[i][None, :], c_gate_x_b[i][None, :],
                     c_lambda[i][None, :])
            oc_p, hl_p = _rglru(z, cq_rows, jnp.zeros((nb_p, 8, cwid), F32), jnp.zeros((nb_p, 1, cwid), F32),
                                *cargs, 0, nb_p, tp, ROW_BLOCK)
            prev_s = jnp.concatenate([jnp.zeros((nb_s, 8 - (CONV_W - 1), cwid), F32), state_c_conv[i]], axis=1)
            oc_s, hl_s = _rglru(z, cq_rows, prev_s, state_c_h[i][:, None, :], *cargs, rp, nb_s, seq_s, seq_s)
            hl_p, hl_s = hl_p[:, 0], hl_s[:, 0]
            log_gamma = jnp.log(1.0 - 2.0 ** (-5.0 - jnp.arange(D_HEADS, dtype=F32)))
            gdn = d_out_norm[i][None, :]
            od_p, ds_p = _retention(z, *tabs256, jnp.zeros((nb_p, D_HEADS, D_KDIM, D_VDIM), F32), gdn, log_gamma,
                                    2 * cwid, 0, nb_p, tp, ROW_BLOCK)
            od_s, ds_s = _retention(z, *tabs256, state_d_s[i], gdn, log_gamma, 2 * cwid, rp, nb_s, seq_s, seq_s)
            h = _out_proj(h, oc_p, od_p, oc_s, od_s, cd_w_out[i].astype(BF16), tm_out)
            nc = CONV_W - 1
            def seq_tails(row0, nb, t):
                n = min(nc, t)
                rows = row0 + t - n + (jnp.arange(nb, dtype=jnp.int32) * t)[:, None] + jnp.arange(n, dtype=jnp.int32)
                return jnp.take(z, rows.reshape(-1), axis=0)[:, :cwid].reshape(nb, n, cwid)

            cd_p.append((seq_tails(0, nb_p, tp), hl_p, ds_p))
            cd_s.append((jnp.concatenate([state_c_conv[i], seq_tails(rp, nb_s, seq_s)], axis=1)[:, -nc:], hl_s, ds_s))
        last = layer == depth - 1
        h = _mlp(h, norm_mlp[layer][None, :], mlp_up, mlp_down, layer, tm_out if last else tile, 512,
                 split_rows=(rp, rs) if last else None)

    def stack(entries, j):
        return jnp.stack([e[j] for e in entries])

    y_p = h[0].reshape(nb_p, tp, d_model)[:, pad + n_meta:]
    y_s = h[1].reshape(nb_s, seq_s, d_model)
    return (y_p, y_s,
            stack(ab_p, 0), stack(ab_p, 1), stack(ab_p, 2), stack(ab_p, 3), stack(ab_p, 4),
            stack(cd_p, 0), stack(cd_p, 1), stack(cd_p, 2),
            stack(ab_s, 0), stack(ab_s, 1), stack(ab_s, 2), stack(ab_s, 3), stack(ab_s, 4),
            stack(cd_s, 0), stack(cd_s, 1), stack(cd_s, 2))
```

```python
import functools
import math

import jax
import jax.numpy as jnp
import numpy as np
from jax import lax
from jax.experimental import pallas as pl
from jax.experimental.pallas import tpu as pltpu

F32 = jnp.float32
BF16 = jnp.bfloat16

CHUNK = 64
ROPE_THETA = 10000.0
EPS = 1e-6
A_HEADS, A_KV_HEADS, A_HEAD_DIM = 8, 2, 128
IDX_HEADS, IDX_DIM = 16, 64
TOPK_MAX = 256
B_HEADS, B_NOPE, B_ROPE, B_V = 8, 128, 64, 128
C_BLOCKS, C_BLOCK, CONV_W = 8, 128, 4
RG_C = 8.0
D_HEADS, D_KDIM, D_VDIM = 4, 256, 256

LANES = 128
ROW_BLOCK = 128
NEG = -1e30
INT_MIN = -2 ** 31
VMEM_LIMIT = 56 * 1024 * 1024


def _cparams(*sem):
    return pltpu.CompilerParams(dimension_semantics=sem, vmem_limit_bytes=VMEM_LIMIT)


def _const_spec(shape):
    nd = len(shape)
    return pl.BlockSpec(shape, lambda *_: (0,) * nd)


def _rms(x, g):
    ms = jnp.mean(x * x, axis=-1, keepdims=True)
    return x * lax.rsqrt(ms + EPS) * g


def _rms_half(x, g):
    lane = lax.broadcasted_iota(jnp.int32, x.shape, 1)
    lo = lane < 64
    xx = x * x
    s_lo = jnp.sum(jnp.where(lo, xx, 0.0), axis=-1, keepdims=True)
    s_hi = jnp.sum(jnp.where(lo, 0.0, xx), axis=-1, keepdims=True)
    ms = jnp.where(lo, s_lo, s_hi) * (1.0 / 64.0)
    return x * lax.rsqrt(ms + EPS) * g


def _rope128(x, cos, sin_signed):
    return x * cos + pltpu.roll(x, 64, 1) * sin_signed


def _rope64(x, cos, sin_signed):
    lane = lax.broadcasted_iota(jnp.int32, x.shape, 1)
    first = (lane % 64) < 32
    rot = jnp.where(first, pltpu.roll(x, 96, 1), pltpu.roll(x, 32, 1))
    return x * cos + rot * sin_signed


AB_COLS = dict(qa=(0, 1024), ka=(1024, 1280), va=(1280, 1536), qi=(1536, 2560), cq=(2560, 3072),
               ckv=(3072, 3328), kikr=(3328, 3456), wi=(3456, 3584))
AB_IN_PAD = 3584


def _ab_proj_kernel(h_ref, gmix_ref, win_ref, wuq_ref, gaq_ref, gak_ref, gqlat_ref, gkvlat_ref, gqn_ref,
                    gqr_ref, gkr_ref, c128_ref, s128_ref, c64_ref, s64_ref,
                    q_ref, kf_ref, vf_ref, kb_ref, vb_ref, qi_ref, kikr_ref, kikrb_ref, wi_ref, qn_ref, qr_ref,
                    lat_ref):
    xn = _rms(h_ref[...], gmix_ref[...]).astype(BF16)
    c128, s128, c64, s64 = c128_ref[...], s128_ref[...], c64_ref[...], s64_ref[...]

    def proj(name):
        a, b = AB_COLS[name]
        return jnp.dot(xn, win_ref[:, a:b], preferred_element_type=F32)

    z = proj('qa')
    qscale = A_HEAD_DIM ** -0.5
    for h in range(A_HEADS):
        x = _rope128(_rms(z[:, h * 128:(h + 1) * 128], gaq_ref[...]), c128, s128)
        q_ref[:, h * 128:(h + 1) * 128] = (x * qscale).astype(BF16)
    z = proj('ka')
    for h in range(A_KV_HEADS):
        x = _rope128(_rms(z[:, h * 128:(h + 1) * 128], gak_ref[...]), c128, s128)
        kf_ref[:, h * 128:(h + 1) * 128] = x
        kb_ref[:, h * 128:(h + 1) * 128] = x.astype(BF16)
    z = proj('va')
    vf_ref[...] = z
    vb_ref[...] = z.astype(BF16)
    z = proj('qi')
    for p in range(IDX_HEADS // 2):
        x = _rope64(z[:, p * 128:(p + 1) * 128], c64, s64)
        qi_ref[:, p * 128:(p + 1) * 128] = (x * (IDX_DIM ** -0.5)).astype(BF16)
    z = proj('kikr')
    lane = lax.broadcasted_iota(jnp.int32, z.shape, 1)
    x = jnp.where(lane < 64, z, _rms_half(z, gkr_ref[...]))
    x = _rope64(x, c64, s64)
    kikr_ref[...] = x
    kikrb_ref[...] = x.astype(BF16)
    wi_ref[...] = proj('wi') * (IDX_HEADS ** -0.5)
    lat = _rms(proj('ckv'), gkvlat_ref[...])
    lat_ref[...] = lat
    cq = _rms(proj('cq'), gqlat_ref[...]).astype(BF16)
    bscale = (B_NOPE + B_ROPE) ** -0.5
    zq = jnp.dot(cq, wuq_ref[:, :B_HEADS * B_NOPE], preferred_element_type=F32)
    for h in range(B_HEADS):
        x = _rms(zq[:, h * 128:(h + 1) * 128], gqn_ref[...])
        qn_ref[:, h * 128:(h + 1) * 128] = (x * bscale).astype(BF16)
    zq = jnp.dot(cq, wuq_ref[:, B_HEADS * B_NOPE:], preferred_element_type=F32)
    for p in range(B_HEADS // 2):
        x = _rope64(_rms_half(zq[:, p * 128:(p + 1) * 128], gqr_ref[...]), c64, s64)
        qr_ref[:, p * 128:(p + 1) * 128] = (x * bscale).astype(BF16)


def _ab_proj(h, gmix, win, wuq, gaq, gak, gqlat, gkvlat, gqn, gqr2, gkr2, tabs, tm):
    r, d = h.shape
    row = lambda w: pl.BlockSpec((tm, w), lambda i: (i, 0))
    outs = [(1024, BF16), (256, F32), (256, F32), (256, BF16), (256, BF16), (1024, BF16), (128, F32),
            (128, BF16), (128, F32), (1024, BF16), (512, BF16), (256, F32)]
    return pl.pallas_call(
        _ab_proj_kernel,
        grid=(r // tm,),
        in_specs=[row(d), _const_spec(gmix.shape), _const_spec(win.shape), _const_spec(wuq.shape),
                  _const_spec(gaq.shape), _const_spec(gak.shape), _const_spec(gqlat.shape),
                  _const_spec(gkvlat.shape), _const_spec(gqn.shape), _const_spec(gqr2.shape),
                  _const_spec(gkr2.shape), row(128), row(128), row(128), row(128)],
        out_specs=[row(w) for w, _ in outs],
        out_shape=[jax.ShapeDtypeStruct((r, w), dt) for w, dt in outs],
        compiler_params=_cparams("parallel"),
        name="ab_proj",
    )(h, gmix, win, wuq, gaq, gak, gqlat, gkvlat, gqn, gqr2, gkr2, *tabs)


def _ukv_kernel(lat_ref, w_ref, gkn_ref, kn_ref, vb_ref):
    lat = lat_ref[...].astype(BF16)
    z = jnp.dot(lat, w_ref[:, :B_HEADS * B_NOPE], preferred_element_type=F32)
    for h in range(B_HEADS):
        kn_ref[:, h * 128:(h + 1) * 128] = _rms(z[:, h * 128:(h + 1) * 128], gkn_ref[...]).astype(BF16)
    vb_ref[...] = jnp.dot(lat, w_ref[:, B_HEADS * B_NOPE:], preferred_element_type=F32).astype(BF16)


def _ukv(lat, w, gkn, tm):
    n = lat.shape[0]
    row = lambda wd: pl.BlockSpec((tm, wd), lambda i: (i, 0))
    return pl.pallas_call(
        _ukv_kernel,
        grid=(n // tm,),
        in_specs=[row(lat.shape[1]), _const_spec(w.shape), _const_spec(gkn.shape)],
        out_specs=[row(1024), row(1024)],
        out_shape=[jax.ShapeDtypeStruct((n, 1024), BF16)] * 2,
        compiler_params=_cparams("parallel"),
        name="mla_ukv",
    )(lat, w, gkn)


def _dot_t(a, b):
    return lax.dot_general(a, b, (((1,), (1,)), ((), ())), preferred_element_type=F32)


def _float_key(x):
    bits = pltpu.bitcast(x, jnp.int32)
    return bits ^ ((bits >> 31) & jnp.int32(0x7FFFFFFF))


def _counts_ge16(vals, cands):
    rows = cands[0].shape[0]
    c16 = [jnp.broadcast_to(c, (rows, LANES)).astype(jnp.int16) for c in cands]
    one, zero = jnp.int16(1), jnp.int16(0)
    accs = [jnp.zeros((rows, LANES), jnp.int16) for _ in cands]
    rest = [jnp.zeros((rows, 1), F32) for _ in cands]
    for v in vals:
        n_full = v.shape[1] // LANES * LANES
        for s0 in range(0, n_full, LANES):
            blk = v[:, s0:s0 + LANES]
            accs = [a + jnp.where(blk >= c, one, zero) for a, c in zip(accs, c16)]
        if n_full < v.shape[1]:
            blk = v[:, n_full:]
            w = blk.shape[1]
            rest = [r + jnp.sum(jnp.where(blk >= c[:, :w], one, zero).astype(F32), axis=-1, keepdims=True)
                    for r, c in zip(rest, c16)]
    return [jnp.sum(a.astype(F32), axis=-1, keepdims=True) + r for a, r in zip(accs, rest)]


SEARCH_STEPS = 8
UNROLL_SEARCH_MAX_KEYS = 1152


def _search16(vals, k, side_jobs):
    def advance(t, step):
        cnts = _counts_ge16(vals, [t + j * step for j in (1, 2, 3)])
        digit = functools.reduce(jnp.add, [jnp.where(c >= float(k), 1, 0) for c in cnts])
        return t + digit * step

    t = jnp.full((vals[0].shape[0], 1), -32768, jnp.int32)
    if sum(v.shape[1] for v in vals) > UNROLL_SEARCH_MAX_KEYS:
        return lax.fori_loop(0, SEARCH_STEPS, lambda i, t: advance(t, jnp.left_shift(jnp.int32(1), 14 - 2 * i)), t)
    for i in range(SEARCH_STEPS):
        t = advance(t, 1 << (14 - 2 * i))
        if i % 2 == 1 and side_jobs:
            side_jobs.pop(0)()
    return t


def _kth_largest(keys, k, side_jobs):
    his = [(key >> 16).astype(jnp.int16) for key in keys]
    los = [((key & 0xFFFF) - 32768).astype(jnp.int16) for key in keys]
    h = _search16(his, k, side_jobs)
    top, bottom = jnp.int16(32767), jnp.int16(-32768)
    los2 = []
    for hi, lo in zip(his, los):
        h16 = jnp.broadcast_to(h, hi.shape).astype(jnp.int16)
        los2.append(jnp.where(hi == h16, lo, jnp.where(hi > h16, top, bottom)))
    l = _search16(los2, k, side_jobs)
    return (h << 16) | ((l + 32768) & 0xFFFF)


def _kv_head(x, c):
    if callable(x):
        return x(c)
    return x[c] if isinstance(x, tuple) else x[:, c * 128:(c + 1) * 128]


def _tie_break(keys, thr, topk, bias_refs):
    nq = keys[0].shape[0]
    n_gt = functools.reduce(jnp.add, [jnp.sum(jnp.where(key > thr, 1.0, 0.0), axis=-1, keepdims=True)
                                      for key in keys])
    need = float(topk) - n_gt
    r_i = lax.broadcasted_iota(jnp.int32, (LANES, LANES), 0)
    c_i = lax.broadcasted_iota(jnp.int32, (LANES, LANES), 1)
    before = jnp.where(r_i < c_i, 1.0, 0.0).astype(BF16)
    seen = jnp.zeros((nq, 1), F32)
    for key, ref in zip(keys, bias_refs):
        for s0 in range(0, key.shape[1], LANES):
            blk = key[:, s0:s0 + LANES]
            w = blk.shape[1]
            eq = jnp.where(blk == thr, 1.0, 0.0)
            rank = seen + jnp.dot(eq.astype(BF16), before[:w, :w], preferred_element_type=F32)
            sel = jnp.logical_or(blk > thr, jnp.logical_and(eq > 0.0, rank < need))
            ref[:, s0:s0 + w] = jnp.where(jnp.logical_and(sel, blk > INT_MIN), 0.0, NEG)
            seen = seen + jnp.sum(eq, axis=-1, keepdims=True)


def _dsa_core(qi, wi, q, cq, pieces, topk, bias_refs, side_jobs):
    nq = qi.shape[0]
    keys = []
    for ki, _, _, ck in pieces:
        score = jnp.zeros((nq, ki.shape[0]), F32)
        for h in range(IDX_HEADS):
            s_h = _dot_t(qi[:, h * 64:(h + 1) * 64], ki)
            score = score + jnp.maximum(s_h, 0.0) * wi[:, h:h + 1]
        keys.append(jnp.where(ck <= cq, _float_key(score), INT_MIN))
    thr = _kth_largest(keys, topk, side_jobs)
    n_ge = jnp.zeros((nq, 1), F32)
    for key, ref in zip(keys, bias_refs):
        ge = key >= thr
        ref[...] = jnp.where(jnp.logical_and(ge, key > INT_MIN), 0.0, NEG)
        n_ge = n_ge + jnp.sum(jnp.where(ge, 1.0, 0.0), axis=-1, keepdims=True)
    tied = jnp.max(jnp.where(jnp.logical_and(n_ge > float(topk), thr > INT_MIN), 1.0, 0.0))
    pl.when(tied > 0.0)(functools.partial(_tie_break, keys, thr, topk, bias_refs))
    group = A_HEADS // A_KV_HEADS
    biases = [jnp.concatenate([ref[...]] * group, axis=0) for ref in bias_refs]
    outs = []
    for c in range(A_KV_HEADS):
        qg = jnp.concatenate([q[:, (c * group + g) * 128:(c * group + g + 1) * 128] for g in range(group)], axis=0)
        ss = [_dot_t(qg, _kv_head(k, c)) + bias for (_, k, _, _), bias in zip(pieces, biases)]
        m = functools.reduce(jnp.maximum, [jnp.max(s, axis=-1, keepdims=True) for s in ss])
        ps = [jnp.exp(s - m) for s in ss]
        den = functools.reduce(jnp.add, [jnp.sum(p, axis=-1, keepdims=True) for p in ps])
        o = functools.reduce(jnp.add, [jnp.dot(p.astype(BF16), _kv_head(v, c), preferred_element_type=F32)
                                       for p, (_, _, v, _) in zip(ps, pieces)])
        o = o / den
        outs.extend([o[g * nq:(g + 1) * nq] for g in range(group)])
    return jnp.concatenate(outs, axis=-1)


def _mla_jobs(qn, qr, cq, pieces, store):
    biases = [jnp.where(ck <= cq, 0.0, NEG) for _, _, _, ck in pieces]

    def head(h):
        ss = []
        for (kn, kr, _, _), bias in zip(pieces, biases):
            s = _dot_t(qn[:, h * B_NOPE:(h + 1) * B_NOPE], kn(h))
            s = s + _dot_t(qr[:, h * B_ROPE:(h + 1) * B_ROPE], kr)
            ss.append(s + bias)
        m = functools.reduce(jnp.maximum, [jnp.max(s, axis=-1, keepdims=True) for s in ss])
        ps = [jnp.exp(s - m) for s in ss]
        den = functools.reduce(jnp.add, [jnp.sum(p, axis=-1, keepdims=True) for p in ps])
        o = functools.reduce(jnp.add, [jnp.dot(p.astype(BF16), vb(h), preferred_element_type=F32)
                                       for p, (_, _, vb, _) in zip(ps, pieces)])
        store(h, o / den)

    return [functools.partial(head, h) for h in range(B_HEADS)]


def _attn_prompt_kernel(qi_ref, wi_ref, q_ref, qn_ref, qr_ref, cq_ref, kikr_ref, k_ref, v_ref, kn_ref, vb_ref,
                        ck_ref, *rest, topk, s_len):
    oa_ref, ob_ref, bias_ref = rest[-3:]
    cq = cq_ref[:, 0:1]
    valid = cq >= 0
    ck = ck_ref[:, :s_len]
    kikr = kikr_ref[:s_len, :]

    def store_b(h, o):
        ob_ref[:, h * B_V:(h + 1) * B_V] = jnp.where(valid, o, 0.0).astype(BF16)

    head_of = lambda ref: lambda h: ref[:s_len, h * B_V:(h + 1) * B_V]
    jobs = _mla_jobs(qn_ref, qr_ref, cq, [(head_of(kn_ref), kikr[:, 64:], head_of(vb_ref), ck)], store_b)
    kv_of = lambda ref: lambda c: ref[:s_len, c * A_HEAD_DIM:(c + 1) * A_HEAD_DIM]
    oa = _dsa_core(qi_ref[...], wi_ref[...], q_ref[...], cq,
                   [(kikr[:, :64], kv_of(k_ref), kv_of(v_ref), ck)], topk, [bias_ref], jobs)
    oa_ref[...] = jnp.where(valid, oa, 0.0).astype(BF16)
    for job in jobs:
        job()


def _attn_prompt(qi, wi, q, qn, qr, cq, kikr, k, v, kn, vb, ck, nb, tp, topk, n_ranges):
    tq = ROW_BLOCK
    nj = tp // tq
    edges = sorted({-(-nj * r // n_ranges) for r in range(n_ranges + 1)})
    kspec = lambda w: pl.BlockSpec((tp, w), lambda b, j: (b, 0))
    outs = (jnp.zeros((nb * tp, 1024), BF16), jnp.zeros((nb * tp, 1024), BF16))
    for j0, j1 in zip(edges[:-1], edges[1:]):
        qspec = lambda w, j0=j0: pl.BlockSpec((tq, w), lambda b, j: (b * nj + j0 + j, 0))
        n_in = 12
        outs = pl.pallas_call(
            functools.partial(_attn_prompt_kernel, topk=topk, s_len=j1 * tq),
            grid=(nb, j1 - j0),
            in_specs=[qspec(1024), qspec(128), qspec(1024), qspec(1024), qspec(512), qspec(128),
                      kspec(128), kspec(256), kspec(256), kspec(1024), kspec(1024), _const_spec(ck.shape)]
                     + [pl.BlockSpec(memory_space=pl.ANY)] * len(outs),
            out_specs=[qspec(1024), qspec(1024)],
            out_shape=[jax.ShapeDtypeStruct((nb * tp, 1024), BF16)] * 2,
            scratch_shapes=[pltpu.VMEM((tq, j1 * tq), F32)],
            input_output_aliases={n_in + o: o for o in range(len(outs))},
            compiler_params=_cparams("parallel", "arbitrary"),
            name="attn_prompt",
        )(qi, wi, q, qn, qr, cq, kikr, k, v, kn, vb, ck, *outs)
    return outs


def _attn_sample_kernel(qi_ref, wi_ref, q_ref, qn_ref, qr_ref, cq_ref, kikr_ref, k_ref, v_ref, kn_ref, vb_ref,
                        pki_ref, pk_ref, pv_ref, pkn_ref, pkr_ref, pvb_ref, ckp_ref, ckn_ref,
                        oa_ref, ob_ref, biasp_ref, biasn_ref, *, topk):
    cq = cq_ref[:, 0:1]
    ckp, ckn = ckp_ref[...], ckn_ref[...]
    kikr = kikr_ref[...]
    past = pki_ref.shape[1]
    heads = lambda ref: lambda c: ref[pl.ds(c, past, stride=A_KV_HEADS), :].astype(BF16)
    kv_of = lambda ref: lambda c: ref[:, c * A_HEAD_DIM:(c + 1) * A_HEAD_DIM]
    past_a = (pki_ref[0].astype(BF16), heads(pk_ref), heads(pv_ref), ckp)
    new_a = (kikr[:, :64], kv_of(k_ref), kv_of(v_ref), ckn)
    head_of = lambda ref: lambda h: ref[:, h * B_V:(h + 1) * B_V]
    past_b = (head_of(pkn_ref), pkr_ref[0].astype(BF16), head_of(pvb_ref), ckp)
    new_b = (head_of(kn_ref), kikr[:, 64:], head_of(vb_ref), ckn)

    def store_b(h, o):
        ob_ref[:, h * B_V:(h + 1) * B_V] = o.astype(BF16)

    jobs = _mla_jobs(qn_ref, qr_ref, cq, [past_b, new_b], store_b)
    oa_ref[...] = _dsa_core(qi_ref[...], wi_ref[...], q_ref[...], cq, [past_a, new_a], topk,
                            [biasp_ref, biasn_ref], jobs).astype(BF16)
    for job in jobs:
        job()


def _attn_sample(qi, wi, q, qn, qr, cq, kikr, k, v, kn, vb, pki, pk, pv, pkn, pkr, pvb, ckp, ckn,
                 row0, nb, ts, topk):
    past = pk.shape[1]
    blk0 = row0 // ts
    nspec = lambda w: pl.BlockSpec((ts, w), lambda b: (blk0 + b, 0))
    pspec = lambda w: pl.BlockSpec((1, past, w), lambda b: (b, 0, 0))
    p2spec = lambda w: pl.BlockSpec((past, w), lambda b: (b, 0))
    kvspec = pl.BlockSpec((None, past * A_KV_HEADS, A_HEAD_DIM), lambda b: (b, 0, 0))
    pk = pk.reshape(nb, past * A_KV_HEADS, A_HEAD_DIM)
    pv = pv.reshape(nb, past * A_KV_HEADS, A_HEAD_DIM)
    return pl.pallas_call(
        functools.partial(_attn_sample_kernel, topk=topk),
        grid=(nb,),
        in_specs=[nspec(1024), nspec(128), nspec(1024), nspec(1024), nspec(512), nspec(128),
                  nspec(128), nspec(256), nspec(256), nspec(1024), nspec(1024),
                  pspec(64), kvspec, kvspec, p2spec(1024), pspec(64), p2spec(1024),
                  _const_spec(ckp.shape), _const_spec(ckn.shape)],
        out_specs=[pl.BlockSpec((ts, 1024), lambda b: (b, 0))] * 2,
        out_shape=[jax.ShapeDtypeStruct((nb * ts, 1024), BF16)] * 2,
        scratch_shapes=[pltpu.VMEM((ts, past), F32), pltpu.VMEM((ts, ts), F32)],
        compiler_params=_cparams("parallel"),
        name="attn_sample",
    )(qi, wi, q, qn, qr, cq, kikr, k, v, kn, vb, pki, pk, pv, pkn, pkr, pvb, ckp, ckn)


def _out_proj_kernel(h_ref, oap_ref, obp_ref, oas_ref, obs_ref, w_ref, o_ref, *, n_p, n_ps):
    half = oap_ref.shape[1]
    i = pl.program_id(0)

    def project(oa_ref, ob_ref):
        y = jnp.dot(oa_ref[...], w_ref[:half, :], preferred_element_type=F32)
        y = y + jnp.dot(ob_ref[...], w_ref[half:, :], preferred_element_type=F32)
        o_ref[...] = h_ref[...] + y

    pl.when(i < n_p)(functools.partial(project, oap_ref, obp_ref))
    pl.when(jnp.logical_and(i >= n_p, i < n_ps))(functools.partial(project, oas_ref, obs_ref))

    @pl.when(i >= n_ps)
    def _():
        o_ref[...] = h_ref[...]


def _out_proj(h, oa_p, ob_p, oa_s, ob_s, w, tm):
    r, d = h.shape
    n_p, n_s = oa_p.shape[0] // tm, oa_s.shape[0] // tm
    assert oa_p.shape[0] % tm == 0 and oa_s.shape[0] % tm == 0
    row = lambda wd: pl.BlockSpec((tm, wd), lambda i: (i, 0))
    pspec = lambda wd: pl.BlockSpec((tm, wd), lambda i: (jnp.minimum(i, n_p - 1), 0))
    sspec = lambda wd: pl.BlockSpec((tm, wd), lambda i: (jnp.clip(i - n_p, 0, n_s - 1), 0))
    return pl.pallas_call(
        functools.partial(_out_proj_kernel, n_p=n_p, n_ps=n_p + n_s),
        grid=(r // tm,),
        in_specs=[row(d), pspec(oa_p.shape[1]), pspec(ob_p.shape[1]), sspec(oa_s.shape[1]), sspec(ob_s.shape[1]),
                  _const_spec(w.shape)],
        out_specs=row(d),
        out_shape=jax.ShapeDtypeStruct((r, d), F32),
        compiler_params=_cparams("parallel"),
        name="out_proj",
    )(h, oa_p, ob_p, oa_s, ob_s, w)


def _mlp_kernel(h_ref, g_ref, wup_ref, wdn_ref, *rest, n_split, n_end):
    *o_refs, xn_ref, acc_ref = rest
    i, f = pl.program_id(0), pl.program_id(1)

    @pl.when(f == 0)
    def _():
        x = h_ref[...]
        xn_ref[...] = _rms(x, g_ref[...]).astype(BF16)
        acc_ref[...] = x

    u = jnp.dot(xn_ref[...], wup_ref[...], preferred_element_type=F32)
    u = jnp.maximum(u, 0.0)
    acc_ref[...] += jnp.dot((u * u).astype(BF16), wdn_ref[...], preferred_element_type=F32)

    @pl.when(f == pl.num_programs(1) - 1)
    def _():
        if len(o_refs) == 1:
            o_refs[0][...] = acc_ref[...]
        else:
            @pl.when(i < n_split)
            def _():
                o_refs[0][...] = acc_ref[...]

            @pl.when(jnp.logical_and(i >= n_split, i < n_end))
            def _():
                o_refs[1][...] = acc_ref[...]


def _mlp(h, g, wup, wdn, layer, tm, tf, split_rows=None):
    r, d = h.shape
    dff = wup.shape[2]
    n_end = r // tm
    if split_rows is None:
        n_split = 0
        out_specs = [pl.BlockSpec((tm, d), lambda i, f: (i, 0))]
        out_shape = [jax.ShapeDtypeStruct((r, d), F32)]
    else:
        a, b = split_rows
        assert a % tm == 0 and b % tm == 0
        n_split, n_b = a // tm, b // tm
        n_end = n_split + n_b
        out_specs = [pl.BlockSpec((tm, d), lambda i, f: (jnp.minimum(i, n_split - 1), 0)),
                     pl.BlockSpec((tm, d), lambda i, f: (jnp.clip(i - n_split, 0, n_b - 1), 0))]
        out_shape = [jax.ShapeDtypeStruct((a, d), F32), jax.ShapeDtypeStruct((b, d), F32)]
    res = pl.pallas_call(
        functools.partial(_mlp_kernel, n_split=n_split, n_end=n_end),
        grid=(r // tm, dff // tf),
        in_specs=[pl.BlockSpec((tm, d), lambda i, f: (i, 0)), pl.BlockSpec((1, d), lambda i, f: (0, 0)),
                  pl.BlockSpec((None, d, tf), lambda i, f: (layer, 0, f)),
                  pl.BlockSpec((None, tf, d), lambda i, f: (layer, f, 0))],
        out_specs=out_specs,
        out_shape=out_shape,
        scratch_shapes=[pltpu.VMEM((tm, d), BF16), pltpu.VMEM((tm, d), F32)],
        compiler_params=_cparams("parallel" if split_rows is None else "arbitrary", "arbitrary"),
        name="mlp",
    )(h, g, wup, wdn)
    return res[0] if split_rows is None else res


def _cd_proj_kernel(h_ref, g_ref, w_ref, z_ref, xn_ref):
    @pl.when(pl.program_id(1) == 0)
    def _():
        xn_ref[...] = _rms(h_ref[...], g_ref[...]).astype(BF16)

    z_ref[...] = jnp.dot(xn_ref[...], w_ref[...], preferred_element_type=F32)


def _cd_proj(h, g, w, tm, tn):
    r, d = h.shape
    n = w.shape[1]
    return pl.pallas_call(
        _cd_proj_kernel,
        grid=(r // tm, n // tn),
        in_specs=[pl.BlockSpec((tm, d), lambda i, j: (i, 0)), pl.BlockSpec((1, d), lambda i, j: (0, 0)),
                  pl.BlockSpec((d, tn), lambda i, j: (0, j))],
        out_specs=pl.BlockSpec((tm, tn), lambda i, j: (i, j)),
        out_shape=jax.ShapeDtypeStruct((r, n), F32),
        scratch_shapes=[pltpu.VMEM((tm, d), BF16)],
        compiler_params=_cparams("parallel", "arbitrary"),
        name="cd_proj",
    )(h, g, w)


def _rglru_kernel(cx_ref, cg_ref, valid_ref, prev0_ref, h0_ref, cw_ref, cb_ref, gw_ref, gab_ref, gxb_ref, lam_ref,
                  o_ref, hl_ref, prev_s, h_s, a_s, b_s):
    @pl.when(pl.program_id(1) == 0)
    def _():
        prev_s[...] = prev0_ref[0]
        h_s[...] = jnp.broadcast_to(h0_ref[0], h_s.shape)

    cx = cx_ref[...]
    tm = cx.shape[0]
    full = jnp.concatenate([prev_s[...], cx], axis=0)
    prev_s[...] = cx[tm - 8:, :]
    xc = cb_ref[...] + cx * cw_ref[CONV_W - 1:CONV_W, :]
    for j in range(CONV_W - 1):
        sh = CONV_W - 1 - j
        xc = xc + full[8 - sh:8 - sh + tm, :] * cw_ref[j:j + 1, :]
    valid = valid_ref[:, 0:1] >= 0
    softplus = jnp.logaddexp(-lam_ref[...], 0.0)
    for n in range(C_BLOCKS):
        sl = slice(n * C_BLOCK, (n + 1) * C_BLOCK)
        xb = xc[:, sl]
        gates = jnp.dot(xb.astype(BF16), gw_ref[n], preferred_element_type=F32)
        r = jax.nn.sigmoid(gates[:, :C_BLOCK] + gab_ref[:, sl])
        ig = jax.nn.sigmoid(gates[:, C_BLOCK:] + gxb_ref[:, sl])
        log_a = -RG_C * r * softplus[:, sl]
        a = jnp.exp(log_a)
        a_s[:, sl] = a
        b = jnp.sqrt(1.0 - a * a) * (ig * xb)
        b_s[:, sl] = jnp.where(valid, b, 0.0)

    rid = lax.broadcasted_iota(jnp.int32, h_s.shape, 0)

    def group(g, h):
        rows = pl.ds(pl.multiple_of(g * 8, 8), 8)
        ca, cb = a_s[rows, :], b_s[rows, :]
        for s in (1, 2, 4):
            m = rid >= s
            cb = jnp.where(m, ca * pltpu.roll(cb, s, 0) + cb, cb)
            ca = jnp.where(m, ca * pltpu.roll(ca, s, 0), ca)
        hh = ca * h + cb
        b_s[rows, :] = hh
        return jnp.broadcast_to(hh[7:8, :], hh.shape)

    h = lax.fori_loop(0, tm // 8, group, h_s[...])
    h_s[...] = h
    hl_ref[0] = h[0:1, :]
    o_ref[...] = (b_s[...] * jax.nn.gelu(cg_ref[...])).astype(BF16)


def _rglru(z, valid, prev0, h0, cw, cb, gw, gab, gxb, lam, row0, nb, t, tm):
    cwid = C_BLOCKS * C_BLOCK
    nj = t // tm
    b0 = row0 // tm
    zspec = lambda c: pl.BlockSpec((tm, cwid), lambda b, j: (b0 + b * nj + j, c))
    return pl.pallas_call(
        _rglru_kernel,
        grid=(nb, nj),
        in_specs=[zspec(0), zspec(1), pl.BlockSpec((tm, LANES), lambda b, j: (b0 + b * nj + j, 0)),
                  pl.BlockSpec((1, 8, cwid), lambda b, j: (b, 0, 0)), pl.BlockSpec((1, 1, cwid), lambda b, j: (b, 0, 0)),
                  _const_spec(cw.shape), _const_spec(cb.shape), _const_spec(gw.shape), _const_spec(gab.shape),
                  _const_spec(gxb.shape), _const_spec(lam.shape)],
        out_specs=[pl.BlockSpec((tm, cwid), lambda b, j: (b * nj + j, 0)),
                   pl.BlockSpec((1, 1, cwid), lambda b, j: (b, 0, 0))],
        out_shape=[jax.ShapeDtypeStruct((nb * t, cwid), BF16), jax.ShapeDtypeStruct((nb, 1, cwid), F32)],
        scratch_shapes=[pltpu.VMEM((8, cwid), F32), pltpu.VMEM((8, cwid), F32), pltpu.VMEM((tm, cwid), F32),
                        pltpu.VMEM((tm, cwid), F32)],
        compiler_params=_cparams("parallel", "arbitrary"),
        name="rglru",
    )(z, z, valid, prev0, h0, cw, cb, gw, gab, gxb, lam)


def _retention_kernel(q_ref, k_ref, v_ref, g_ref, cos_ref, sin_ref, s0_ref, gdn_ref, lg_ref,
                      o_ref, s_ref, *, blk):
    @pl.when(pl.program_id(1) == 0)
    def _():
        s_ref[...] = s0_ref[...]

    cos, sin_signed = cos_ref[...], sin_ref[...]

    def rope(x):
        return x * cos + pltpu.roll(x, D_KDIM // 2, 1) * sin_signed

    ri = lax.broadcasted_iota(jnp.int32, (blk, blk), 0)
    ci = lax.broadcasted_iota(jnp.int32, (blk, blk), 1)
    diff = (ri - ci).astype(F32)
    rows = lax.broadcasted_iota(jnp.int32, (blk, 1), 0).astype(F32)
    for hd in range(D_HEADS):
        lg = lg_ref[hd]
        qsl = slice(hd * D_KDIM, (hd + 1) * D_KDIM)
        vsl = slice(hd * D_VDIM, (hd + 1) * D_VDIM)
        q = rope(q_ref[:, qsl])
        k = rope(k_ref[:, qsl]) * (D_KDIM ** -0.5)
        decay = jnp.where(diff >= 0, jnp.exp(lg * jnp.maximum(diff, 0.0)), 0.0)
        qb, kb, vb = q.astype(BF16), k.astype(BF16), v_ref[:, vsl].astype(BF16)
        att = _dot_t(qb, kb) * decay
        o = jnp.dot(att.astype(BF16), vb, preferred_element_type=F32)
        s_prev = s_ref[0, hd]
        o = o + jnp.dot(qb, s_prev.astype(BF16), preferred_element_type=F32) * jnp.exp(lg * (rows + 1.0))
        kdec = (k * jnp.exp(lg * (blk - 1.0 - rows))).astype(BF16)
        s_ref[0, hd] = jnp.exp(lg * blk) * s_prev + lax.dot_general(kdec, vb, (((0,), (0,)), ((), ())),
                                                                    preferred_element_type=F32)
        od = _rms(o, gdn_ref[...]) * jax.nn.silu(g_ref[:, vsl])
        o_ref[:, vsl] = od.astype(BF16)


def _retention(z, cos, sin_signed, s0, gdn, log_gamma, col0, row0, nb, t, blk):
    nj = t // blk
    b0 = row0 // blk
    wid = D_HEADS * D_KDIM
    c0 = col0 // wid
    zspec = lambda c: pl.BlockSpec((blk, wid), lambda b, j: (b0 + b * nj + j, c0 + c))
    tspec = pl.BlockSpec((blk, D_KDIM), lambda b, j: (b0 + b * nj + j, 0))
    sspec = pl.BlockSpec((1, D_HEADS, D_KDIM, D_VDIM), lambda b, j: (b, 0, 0, 0))
    return pl.pallas_call(
        functools.partial(_retention_kernel, blk=blk),
        grid=(nb, nj),
        in_specs=[zspec(0), zspec(1), zspec(2), zspec(3), tspec, tspec, sspec, _const_spec(gdn.shape),
                  pl.BlockSpec(memory_space=pltpu.SMEM)],
        out_specs=[pl.BlockSpec((blk, wid), lambda b, j: (b * nj + j, 0)), sspec],
        out_shape=[jax.ShapeDtypeStruct((nb * t, wid), BF16), jax.ShapeDtypeStruct(s0.shape, F32)],
        compiler_params=_cparams("parallel", "arbitrary"),
        name="retention",
    )(z, z, z, z, cos, sin_signed, s0, gdn, log_gamma)


def _rope_tables(pos_groups, reps_groups, tail, d, width):
    inv = ROPE_THETA ** (-jnp.arange(0, d, 2, dtype=F32) / d)
    cos_rows, sin_rows = [], []
    for pos, reps in zip(pos_groups, reps_groups):
        ang = pos.astype(F32)[:, None] * inv[None, :]
        cos, sin = lax.optimization_barrier((jnp.cos(ang), jnp.sin(ang)))
        cos_rows.append(jnp.tile(jnp.tile(jnp.concatenate([cos, cos], axis=-1), (1, width // d)), (reps, 1)))
        sin_rows.append(jnp.tile(jnp.tile(jnp.concatenate([-sin, sin], axis=-1), (1, width // d)), (reps, 1)))
    cos_rows.append(jnp.ones((tail, width), F32))
    sin_rows.append(jnp.zeros((tail, width), F32))
    return jnp.concatenate(cos_rows, axis=0), jnp.concatenate(sin_rows, axis=0)


def kernel(x_prompt, x_sample, cache_a_k, cache_a_v, cache_a_kidx, cache_b_latent, cache_b_krope, state_c_conv,
           state_c_h, state_d_s, meta_tokens, norm_mix, norm_mlp, ab_w_in, ab_w_out, a_q_norm, a_k_norm,
           b_q_lat_norm, b_w_uq, b_kv_lat_norm, b_w_ukv, b_qn_norm, b_qr_norm, b_kn_norm, b_kr_norm, cd_w_in,
           cd_w_out, c_conv_w, c_conv_b, c_gate_a_w, c_gate_a_b, c_gate_x_w, c_gate_x_b, c_lambda, d_out_norm,
           mlp_w_up, mlp_w_down):
    nb_p, seq_p, d_model = x_prompt.shape
    nb_s, seq_s, _ = x_sample.shape
    past = cache_a_k.shape[2]
    n_meta = meta_tokens.shape[0]
    depth = norm_mix.shape[0]
    t_real = n_meta + seq_p
    tp = -(-t_real // ROW_BLOCK) * ROW_BLOCK
    pad = tp - t_real
    rp = nb_p * tp
    rs = nb_s * seq_s
    assert seq_s % 8 == 0 and rp % seq_s == 0 and past % 8 == 0
    assert seq_p % ROW_BLOCK == 0 and ROW_BLOCK % CHUNK == 0
    r_real = rp + rs
    tile = 512
    r_tot = -(-r_real // tile) * tile
    tm_out = math.gcd(tile, rp, rs)
    topk_p = min(TOPK_MAX, seq_p // 4)
    topk_s = min(TOPK_MAX, (past + seq_s) // 4)

    pos_p = jnp.concatenate([jnp.zeros((pad,), jnp.int32), jnp.arange(t_real, dtype=jnp.int32)])
    ck_p = jnp.concatenate([jnp.zeros((n_meta,), jnp.int32), 1 + jnp.arange(seq_p, dtype=jnp.int32) // CHUNK])
    cq_p = jnp.concatenate([jnp.full((pad,), -1, jnp.int32), ck_p])
    ckk_p = jnp.concatenate([jnp.full((pad,), 2 ** 30, jnp.int32), ck_p])[None, :]
    pos_s_all = jnp.arange(past + seq_s, dtype=jnp.int32)
    ck_s = pos_s_all // CHUNK
    pos_s = pos_s_all[past:]
    tail = r_tot - r_real
    cq_rows =jnp.concatenate([jnp.tile(cq_p, nb_p), jnp.tile(ck_s[past:], nb_s), jnp.full((tail,), -1, jnp.int32)])
    cq_rows = jnp.broadcast_to(cq_rows[:, None], (r_tot, LANES))
    tabs128 = _rope_tables((pos_p, pos_s), (nb_p, nb_s), tail, 128, 128)
    tabs64 = _rope_tables((pos_p, pos_s), (nb_p, nb_s), tail, 64, 128)
    tabs256 = _rope_tables((pos_p, pos_s), (nb_p, nb_s), tail, 256, 256)

    head = jnp.concatenate([jnp.zeros((pad, d_model), F32), meta_tokens.astype(F32)], axis=0)
    pieces = [p for b in range(nb_p) for p in (head, x_prompt[b])]
    h = jnp.concatenate(pieces + [x_sample.reshape(rs, d_model), jnp.zeros((tail, d_model), F32)], axis=0)
    mlp_up, mlp_down = mlp_w_up.astype(BF16), mlp_w_down.astype(BF16)

    def prompt_rows(x):
        return x[:rp].reshape(nb_p, tp, -1)[:, pad:]

    def sample_rows(x):
        return x[rp:rp + rs].reshape(nb_s, seq_s, -1)

    ab_p, ab_s, cd_p, cd_s = [], [], [], []
    for layer in range(depth):
        i = layer // 2
        gmix = norm_mix[layer][None, :]
        if layer % 2 == 0:
            w = ab_w_in[i]
            offs = np.cumsum([0, 1024, 256, 256, 1024, 64, 16, 512, 256, 64])
            sec = lambda n: w[:, offs[n]:offs[n + 1]]
            win = jnp.concatenate([sec(0), sec(1), sec(2), sec(3), sec(6), sec(7), sec(4), sec(8), sec(5),
                                   jnp.zeros((d_model, AB_IN_PAD - 3472), F32)], axis=1).astype(BF16)
            wuq = b_w_uq[i].reshape(-1, B_HEADS, B_NOPE + B_ROPE)
            wuq = jnp.concatenate([wuq[:, :, :B_NOPE].reshape(-1, B_HEADS * B_NOPE),
                                   wuq[:, :, B_NOPE:].reshape(-1, B_HEADS * B_ROPE)], axis=1).astype(BF16)
            wukv = b_w_ukv[i].reshape(-1, B_HEADS, B_NOPE + B_V)
            wukv = jnp.concatenate([wukv[:, :, :B_NOPE].reshape(-1, B_HEADS * B_NOPE),
                                    wukv[:, :, B_NOPE:].reshape(-1, B_HEADS * B_V)], axis=1).astype(BF16)
            two = lambda g: jnp.concatenate([g, g])[None, :]
            gkr2 = jnp.concatenate([jnp.ones((64,), F32), b_kr_norm[i]])[None, :]
            (q, kf, vf, kb, vb_a, qi, kikr, kikrb, wi, qn, qr, lat) = _ab_proj(
                h, gmix, win, wuq, a_q_norm[i][None, :], a_k_norm[i][None, :], b_q_lat_norm[i][None, :],
                b_kv_lat_norm[i][None, :], b_qn_norm[i][None, :], two(b_qr_norm[i]), gkr2,
                tabs128 + tabs64, tile)
            gkn = b_kn_norm[i][None, :]
            kn, vbb = _ukv(lat, wukv, gkn, 512)
            pkn, pvb = _ukv(cache_b_latent[i].reshape(nb_s * past, -1), wukv, gkn, 512)
            oa_p, ob_p = _attn_prompt(qi, wi, q, qn, qr, cq_rows, kikrb, kb, vb_a, kn, vbb, ckk_p, nb_p, tp, topk_p, 8)
            oa_s, ob_s = _attn_sample(qi, wi, q, qn, qr, cq_rows, kikrb, kb, vb_a, kn, vbb,
                                      cache_a_kidx[i], cache_a_k[i], cache_a_v[i], pkn, cache_b_krope[i], pvb,
                                      ck_s[None, :past], ck_s[None, past:], rp, nb_s, seq_s, topk_s)
            h = _out_proj(h, oa_p, ob_p, oa_s, ob_s, ab_w_out[i].astype(BF16), tm_out)
            kv4 = lambda x: x.reshape(x.shape[0], x.shape[1], A_KV_HEADS, A_HEAD_DIM)
            ab_p.append((kv4(prompt_rows(kf)), kv4(prompt_rows(vf)), prompt_rows(kikr)[..., :64],
                         prompt_rows(lat), prompt_rows(kikr)[..., 64:]))
            ab_s.append((kv4(sample_rows(kf)), kv4(sample_rows(vf)), sample_rows(kikr)[..., :64],
                         sample_rows(lat), sample_rows(kikr)[..., 64:]))
        else:
            z = _cd_proj(h, gmix, cd_w_in[i].astype(BF16), 1024 if r_tot % 1024 == 0 else tile, 1024)
            cwid = C_BLOCKS * C_BLOCK
            gw = jnp.concatenate([c_gate_a_w[i], c_gate_x_w[i]], axis=-1).astype(BF16)
            cargs = (c_conv_w[i], c_conv_b[i][None, :], gw, c_gate_a_b[i][None, :], c_gate_x_b[i][None, :],
                     c_lambda[i][None, :])
            oc_p, hl_p = _rglru(z, cq_rows, jnp.zeros((nb_p, 8, cwid), F32), jnp.zeros((nb_p, 1, cwid), F32),
                                *cargs, 0, nb_p, tp, ROW_BLOCK)
            prev_s = jnp.concatenate([jnp.zeros((nb_s, 8 - (CONV_W - 1), cwid), F32), state_c_conv[i]], axis=1)
            oc_s, hl_s = _rglru(z, cq_rows, prev_s, state_c_h[i][:, None, :], *cargs, rp, nb_s, seq_s, seq_s)
            hl_p, hl_s = hl_p[:, 0], hl_s[:, 0]
            log_gamma = jnp.log(1.0 - 2.0 ** (-5.0 - jnp.arange(D_HEADS, dtype=F32)))
            gdn = d_out_norm[i][None, :]
            od_p, ds_p = _retention(z, *tabs256, jnp.zeros((nb_p, D_HEADS, D_KDIM, D_VDIM), F32), gdn, log_gamma,
                                    2 * cwid, 0, nb_p, tp, ROW_BLOCK)
            od_s, ds_s = _retention(z, *tabs256, state_d_s[i], gdn, log_gamma, 2 * cwid, rp, nb_s, seq_s, seq_s)
            h = _out_proj(h, oc_p, od_p, oc_s, od_s, cd_w_out[i].astype(BF16), tm_out)
            nc = CONV_W - 1
            def seq_tails(row0, nb, t):
                n = min(nc, t)
                rows = row0 + t - n + (jnp.arange(nb, dtype=jnp.int32) * t)[:, None] + jnp.arange(n, dtype=jnp.int32)
                return jnp.take(z, rows.reshape(-1), axis=0)[:, :cwid].reshape(nb, n, cwid)

            cd_p.append((seq_tails(0, nb_p, tp), hl_p, ds_p))
            cd_s.append((jnp.concatenate([state_c_conv[i], seq_tails(rp, nb_s, seq_s)], axis=1)[:, -nc:], hl_s, ds_s))
        last = layer == depth - 1
        h = _mlp(h, norm_mlp[layer][None, :], mlp_up, mlp_down, layer, tm_out if last else tile, 1024,
                 split_rows=(rp, rs) if last else None)

    def stack(entries, j):
        return jnp.stack([e[j] for e in entries])

    y_p = h[0].reshape(nb_p, tp, d_model)[:, pad + n_meta:]
    y_s = h[1].reshape(nb_s, seq_s, d_model)
    return (y_p, y_s,
            stack(ab_p, 0), stack(ab_p, 1), stack(ab_p, 2), stack(ab_p, 3), stack(ab_p, 4),
            stack(cd_p, 0), stack(cd_p, 1), stack(cd_p, 2),
            stack(ab_s, 0), stack(ab_s, 1), stack(ab_s, 2), stack(ab_s, 3), stack(ab_s, 4),
            stack(cd_s, 0), stack(cd_s, 1), stack(cd_s, 2))
```

```python
import functools
import math

import jax
import jax.numpy as jnp
import numpy as np
from jax import lax
from jax.experimental import pallas as pl
from jax.experimental.pallas import tpu as pltpu

F32 = jnp.float32
BF16 = jnp.bfloat16

CHUNK = 64
ROPE_THETA = 10000.0
EPS = 1e-6
A_HEADS, A_KV_HEADS, A_HEAD_DIM = 8, 2, 128
IDX_HEADS, IDX_DIM = 16, 64
TOPK_MAX = 256
B_HEADS, B_NOPE, B_ROPE, B_V = 8, 128, 64, 128
C_BLOCKS, C_BLOCK, CONV_W = 8, 128, 4
RG_C = 8.0
D_HEADS, D_KDIM, D_VDIM = 4, 256, 256

LANES = 128
ROW_BLOCK = 128
NEG = -1e30
INT_MIN = -2 ** 31
VMEM_LIMIT = 56 * 1024 * 1024


def _cparams(*sem):
    return pltpu.CompilerParams(dimension_semantics=sem, vmem_limit_bytes=VMEM_LIMIT)


def _const_spec(shape):
    nd = len(shape)
    return pl.BlockSpec(shape, lambda *_: (0,) * nd)


def _rms(x, g):
    ms = jnp.mean(x * x, axis=-1, keepdims=True)
    return x * lax.rsqrt(ms + EPS) * g


def _rms_half(x, g):
    lane = lax.broadcasted_iota(jnp.int32, x.shape, 1)
    lo = lane < 64
    xx = x * x
    s_lo = jnp.sum(jnp.where(lo, xx, 0.0), axis=-1, keepdims=True)
    s_hi = jnp.sum(jnp.where(lo, 0.0, xx), axis=-1, keepdims=True)
    ms = jnp.where(lo, s_lo, s_hi) * (1.0 / 64.0)
    return x * lax.rsqrt(ms + EPS) * g


def _rope128(x, cos, sin_signed):
    return x * cos + pltpu.roll(x, 64, 1) * sin_signed


def _rope64(x, cos, sin_signed):
    lane = lax.broadcasted_iota(jnp.int32, x.shape, 1)
    first = (lane % 64) < 32
    rot = jnp.where(first, pltpu.roll(x, 96, 1), pltpu.roll(x, 32, 1))
    return x * cos + rot * sin_signed


AB_COLS = dict(qa=(0, 1024), ka=(1024, 1280), va=(1280, 1536), qi=(1536, 2560), cq=(2560, 3072),
               ckv=(3072, 3328), kikr=(3328, 3456), wi=(3456, 3584))
AB_IN_PAD = 3584


def _ab_proj_kernel(h_ref, gmix_ref, win_ref, wuq_ref, gaq_ref, gak_ref, gqlat_ref, gkvlat_ref, gqn_ref,
                    gqr_ref, gkr_ref, c128_ref, s128_ref, c64_ref, s64_ref,
                    q_ref, kf_ref, vf_ref, kb_ref, vb_ref, qi_ref, kikr_ref, kikrb_ref, wi_ref, qn_ref, qr_ref,
                    lat_ref):
    xn = _rms(h_ref[...], gmix_ref[...]).astype(BF16)
    c128, s128, c64, s64 = c128_ref[...], s128_ref[...], c64_ref[...], s64_ref[...]

    def proj(name):
        a, b = AB_COLS[name]
        return jnp.dot(xn, win_ref[:, a:b], preferred_element_type=F32)

    z = proj('qa')
    qscale = A_HEAD_DIM ** -0.5
    for h in range(A_HEADS):
        x = _rope128(_rms(z[:, h * 128:(h + 1) * 128], gaq_ref[...]), c128, s128)
        q_ref[:, h * 128:(h + 1) * 128] = (x * qscale).astype(BF16)
    z = proj('ka')
    for h in range(A_KV_HEADS):
        x = _rope128(_rms(z[:, h * 128:(h + 1) * 128], gak_ref[...]), c128, s128)
        kf_ref[:, h * 128:(h + 1) * 128] = x
        kb_ref[:, h * 128:(h + 1) * 128] = x.astype(BF16)
    z = proj('va')
    vf_ref[...] = z
    vb_ref[...] = z.astype(BF16)
    z = proj('qi')
    for p in range(IDX_HEADS // 2):
        x = _rope64(z[:, p * 128:(p + 1) * 128], c64, s64)
        qi_ref[:, p * 128:(p + 1) * 128] = (x * (IDX_DIM ** -0.5)).astype(BF16)
    z = proj('kikr')
    lane = lax.broadcasted_iota(jnp.int32, z.shape, 1)
    x = jnp.where(lane < 64, z, _rms_half(z, gkr_ref[...]))
    x = _rope64(x, c64, s64)
    kikr_ref[...] = x
    kikrb_ref[...] = x.astype(BF16)
    wi_ref[...] = proj('wi') * (IDX_HEADS ** -0.5)
    lat = _rms(proj('ckv'), gkvlat_ref[...])
    lat_ref[...] = lat
    cq = _rms(proj('cq'), gqlat_ref[...]).astype(BF16)
    bscale = (B_NOPE + B_ROPE) ** -0.5
    zq = jnp.dot(cq, wuq_ref[:, :B_HEADS * B_NOPE], preferred_element_type=F32)
    for h in range(B_HEADS):
        x = _rms(zq[:, h * 128:(h + 1) * 128], gqn_ref[...])
        qn_ref[:, h * 128:(h + 1) * 128] = (x * bscale).astype(BF16)
    zq = jnp.dot(cq, wuq_ref[:, B_HEADS * B_NOPE:], preferred_element_type=F32)
    for p in range(B_HEADS // 2):
        x = _rope64(_rms_half(zq[:, p * 128:(p + 1) * 128], gqr_ref[...]), c64, s64)
        qr_ref[:, p * 128:(p + 1) * 128] = (x * bscale).astype(BF16)


def _ab_proj(h, gmix, win, wuq, gaq, gak, gqlat, gkvlat, gqn, gqr2, gkr2, tabs, tm):
    r, d = h.shape
    row = lambda w: pl.BlockSpec((tm, w), lambda i: (i, 0))
    outs = [(1024, BF16), (256, F32), (256, F32), (256, BF16), (256, BF16), (1024, BF16), (128, F32),
            (128, BF16), (128, F32), (1024, BF16), (512, BF16), (256, F32)]
    return pl.pallas_call(
        _ab_proj_kernel,
        grid=(r // tm,),
        in_specs=[row(d), _const_spec(gmix.shape), _const_spec(win.shape), _const_spec(wuq.shape),
                  _const_spec(gaq.shape), _const_spec(gak.shape), _const_spec(gqlat.shape),
                  _const_spec(gkvlat.shape), _const_spec(gqn.shape), _const_spec(gqr2.shape),
                  _const_spec(gkr2.shape), row(128), row(128), row(128), row(128)],
        out_specs=[row(w) for w, _ in outs],
        out_shape=[jax.ShapeDtypeStruct((r, w), dt) for w, dt in outs],
        compiler_params=_cparams("parallel"),
        name="ab_proj",
    )(h, gmix, win, wuq, gaq, gak, gqlat, gkvlat, gqn, gqr2, gkr2, *tabs)


def _ukv_kernel(lat_ref, w_ref, gkn_ref, kn_ref, vb_ref):
    lat = lat_ref[...].astype(BF16)
    z = jnp.dot(lat, w_ref[:, :B_HEADS * B_NOPE], preferred_element_type=F32)
    for h in range(B_HEADS):
        kn_ref[:, h * 128:(h + 1) * 128] = _rms(z[:, h * 128:(h + 1) * 128], gkn_ref[...]).astype(BF16)
    vb_ref[...] = jnp.dot(lat, w_ref[:, B_HEADS * B_NOPE:], preferred_element_type=F32).astype(BF16)


def _ukv(lat, w, gkn, tm):
    n = lat.shape[0]
    row = lambda wd: pl.BlockSpec((tm, wd), lambda i: (i, 0))
    return pl.pallas_call(
        _ukv_kernel,
        grid=(n // tm,),
        in_specs=[row(lat.shape[1]), _const_spec(w.shape), _const_spec(gkn.shape)],
        out_specs=[row(1024), row(1024)],
        out_shape=[jax.ShapeDtypeStruct((n, 1024), BF16)] * 2,
        compiler_params=_cparams("parallel"),
        name="mla_ukv",
    )(lat, w, gkn)


def _dot_t(a, b):
    return lax.dot_general(a, b, (((1,), (1,)), ((), ())), preferred_element_type=F32)


def _float_key(x):
    bits = pltpu.bitcast(x, jnp.int32)
    return bits ^ ((bits >> 31) & jnp.int32(0x7FFFFFFF))


def _counts_ge16(vals, cands):
    rows = cands[0].shape[0]
    c16 = [jnp.broadcast_to(c, (rows, LANES)).astype(jnp.int16) for c in cands]
    one, zero = jnp.int16(1), jnp.int16(0)
    accs = [jnp.zeros((rows, LANES), jnp.int16) for _ in cands]
    rest = [jnp.zeros((rows, 1), F32) for _ in cands]
    for v in vals:
        n_full = v.shape[1] // LANES * LANES
        for s0 in range(0, n_full, LANES):
            blk = v[:, s0:s0 + LANES]
            accs = [a + jnp.where(blk >= c, one, zero) for a, c in zip(accs, c16)]
        if n_full < v.shape[1]:
            blk = v[:, n_full:]
            w = blk.shape[1]
            rest = [r + jnp.sum(jnp.where(blk >= c[:, :w], one, zero).astype(F32), axis=-1, keepdims=True)
                    for r, c in zip(rest, c16)]
    return [jnp.sum(a.astype(F32), axis=-1, keepdims=True) + r for a, r in zip(accs, rest)]


SEARCH_STEPS = 8
UNROLL_SEARCH_MAX_KEYS = 1152


def _search16(vals, k, side_jobs):
    def advance(t, step):
        cnts = _counts_ge16(vals, [t + j * step for j in (1, 2, 3)])
        digit = functools.reduce(jnp.add, [jnp.where(c >= float(k), 1, 0) for c in cnts])
        return t + digit * step

    t = jnp.full((vals[0].shape[0], 1), -32768, jnp.int32)
    if sum(v.shape[1] for v in vals) > UNROLL_SEARCH_MAX_KEYS:
        return lax.fori_loop(0, SEARCH_STEPS, lambda i, t: advance(t, jnp.left_shift(jnp.int32(1), 14 - 2 * i)), t)
    for i in range(SEARCH_STEPS):
        t = advance(t, 1 << (14 - 2 * i))
        if i % 2 == 1 and side_jobs:
            side_jobs.pop(0)()
    return t


def _kth_largest(keys, k, side_jobs):
    his = [(key >> 16).astype(jnp.int16) for key in keys]
    los = [((key & 0xFFFF) - 32768).astype(jnp.int16) for key in keys]
    h = _search16(his, k, side_jobs)
    top, bottom = jnp.int16(32767), jnp.int16(-32768)
    los2 = []
    for hi, lo in zip(his, los):
        h16 = jnp.broadcast_to(h, hi.shape).astype(jnp.int16)
        los2.append(jnp.where(hi == h16, lo, jnp.where(hi > h16, top, bottom)))
    l = _search16(los2, k, side_jobs)
    return (h << 16) | ((l + 32768) & 0xFFFF)


def _softmax_pv(ss, vs):
    m = functools.reduce(jnp.maximum, [jnp.max(s, axis=-1, keepdims=True) for s in ss])
    acc = functools.reduce(jnp.add, [
        jnp.dot(jnp.exp(s - m).astype(BF16), jnp.concatenate([v, jnp.ones_like(v)], axis=1),
                preferred_element_type=F32) for s, v in zip(ss, vs)])
    d = vs[0].shape[1]
    return acc[:, :d] / acc[:, d:]


def _kv_head(x, c):
    if callable(x):
        return x(c)
    return x[c] if isinstance(x, tuple) else x[:, c * 128:(c + 1) * 128]


def _tie_break(keys, thr, topk, bias_refs):
    nq = keys[0].shape[0]
    n_gt = functools.reduce(jnp.add, [jnp.sum(jnp.where(key > thr, 1.0, 0.0), axis=-1, keepdims=True)
                                      for key in keys])
    need = float(topk) - n_gt
    r_i = lax.broadcasted_iota(jnp.int32, (LANES, LANES), 0)
    c_i = lax.broadcasted_iota(jnp.int32, (LANES, LANES), 1)
    before = jnp.where(r_i < c_i, 1.0, 0.0).astype(BF16)
    seen = jnp.zeros((nq, 1), F32)
    for key, ref in zip(keys, bias_refs):
        for s0 in range(0, key.shape[1], LANES):
            blk = key[:, s0:s0 + LANES]
            w = blk.shape[1]
            eq = jnp.where(blk == thr, 1.0, 0.0)
            rank = seen + jnp.dot(eq.astype(BF16), before[:w, :w], preferred_element_type=F32)
            sel = jnp.logical_or(blk > thr, jnp.logical_and(eq > 0.0, rank < need))
            ref[:, s0:s0 + w] = jnp.where(jnp.logical_and(sel, blk > INT_MIN), 0.0, NEG)
            seen = seen + jnp.sum(eq, axis=-1, keepdims=True)


def _dsa_core(qi, wi, q, cq, pieces, topk, bias_refs, side_jobs):
    nq = qi.shape[0]
    keys = []
    for ki, _, _, ck in pieces:
        score = jnp.zeros((nq, ki.shape[0]), F32)
        for h in range(IDX_HEADS):
            s_h = _dot_t(qi[:, h * 64:(h + 1) * 64], ki)
            score = score + jnp.maximum(s_h, 0.0) * wi[:, h:h + 1]
        keys.append(jnp.where(ck <= cq, _float_key(score), INT_MIN))
    thr = _kth_largest(keys, topk, side_jobs)
    n_ge = jnp.zeros((nq, 1), F32)
    for key, ref in zip(keys, bias_refs):
        ge = key >= thr
        ref[...] = jnp.where(jnp.logical_and(ge, key > INT_MIN), 0.0, NEG)
        n_ge = n_ge + jnp.sum(jnp.where(ge, 1.0, 0.0), axis=-1, keepdims=True)
    tied = jnp.max(jnp.where(jnp.logical_and(n_ge > float(topk), thr > INT_MIN), 1.0, 0.0))
    pl.when(tied > 0.0)(functools.partial(_tie_break, keys, thr, topk, bias_refs))
    group = A_HEADS // A_KV_HEADS
    biases = [jnp.concatenate([ref[...]] * group, axis=0) for ref in bias_refs]
    outs = []
    for c in range(A_KV_HEADS):
        qg = jnp.concatenate([q[:, (c * group + g) * 128:(c * group + g + 1) * 128] for g in range(group)], axis=0)
        ss = [_dot_t(qg, _kv_head(k, c)) + bias for (_, k, _, _), bias in zip(pieces, biases)]
        o = _softmax_pv(ss, [_kv_head(v, c) for _, _, v, _ in pieces])
        outs.extend([o[g * nq:(g + 1) * nq] for g in range(group)])
    return jnp.concatenate(outs, axis=-1)


def _mla_jobs(qn, qr, cq, pieces, store):
    biases = [jnp.where(ck <= cq, 0.0, NEG) for _, _, _, ck in pieces]

    def head(h):
        ss = []
        for (kn, kr, _, _), bias in zip(pieces, biases):
            s = _dot_t(qn[:, h * B_NOPE:(h + 1) * B_NOPE], kn(h))
            s = s + _dot_t(qr[:, h * B_ROPE:(h + 1) * B_ROPE], kr)
            ss.append(s + bias)
        store(h, _softmax_pv(ss, [vb(h) for _, _, vb, _ in pieces]))

    return [functools.partial(head, h) for h in range(B_HEADS)]


def _attn_prompt_kernel(qi_ref, wi_ref, q_ref, qn_ref, qr_ref, cq_ref, kikr_ref, k_ref, v_ref, kn_ref, vb_ref,
                        ck_ref, *rest, topk, s_len):
    oa_ref, ob_ref, bias_ref = rest[-3:]
    cq = cq_ref[:, 0:1]
    valid = cq >= 0
    ck = ck_ref[:, :s_len]
    kikr = kikr_ref[:s_len, :]

    def store_b(h, o):
        ob_ref[:, h * B_V:(h + 1) * B_V] = jnp.where(valid, o, 0.0).astype(BF16)

    head_of = lambda ref: lambda h: ref[:s_len, h * B_V:(h + 1) * B_V]
    jobs = _mla_jobs(qn_ref, qr_ref, cq, [(head_of(kn_ref), kikr[:, 64:], head_of(vb_ref), ck)], store_b)
    kv_of = lambda ref: lambda c: ref[:s_len, c * A_HEAD_DIM:(c + 1) * A_HEAD_DIM]
    oa = _dsa_core(qi_ref[...], wi_ref[...], q_ref[...], cq,
                   [(kikr[:, :64], kv_of(k_ref), kv_of(v_ref), ck)], topk, [bias_ref], jobs)
    oa_ref[...] = jnp.where(valid, oa, 0.0).astype(BF16)
    for job in jobs:
        job()


def _attn_prompt(qi, wi, q, qn, qr, cq, kikr, k, v, kn, vb, ck, nb, tp, topk, n_ranges):
    tq = ROW_BLOCK
    nj = tp // tq
    edges = sorted({-(-nj * r // n_ranges) for r in range(n_ranges + 1)})
    kspec = lambda w: pl.BlockSpec((tp, w), lambda b, j: (b, 0))
    outs = (jnp.zeros((nb * tp, 1024), BF16), jnp.zeros((nb * tp, 1024), BF16))
    for j0, j1 in zip(edges[:-1], edges[1:]):
        qspec = lambda w, j0=j0: pl.BlockSpec((tq, w), lambda b, j: (b * nj + j0 + j, 0))
        n_in = 12
        outs = pl.pallas_call(
            functools.partial(_attn_prompt_kernel, topk=topk, s_len=j1 * tq),
            grid=(nb, j1 - j0),
            in_specs=[qspec(1024), qspec(128), qspec(1024), qspec(1024), qspec(512), qspec(128),
                      kspec(128), kspec(256), kspec(256), kspec(1024), kspec(1024), _const_spec(ck.shape)]
                     + [pl.BlockSpec(memory_space=pl.ANY)] * len(outs),
            out_specs=[qspec(1024), qspec(1024)],
            out_shape=[jax.ShapeDtypeStruct((nb * tp, 1024), BF16)] * 2,
            scratch_shapes=[pltpu.VMEM((tq, j1 * tq), F32)],
            input_output_aliases={n_in + o: o for o in range(len(outs))},
            compiler_params=_cparams("parallel", "arbitrary"),
            name="attn_prompt",
        )(qi, wi, q, qn, qr, cq, kikr, k, v, kn, vb, ck, *outs)
    return outs


def _attn_sample_kernel(qi_ref, wi_ref, q_ref, qn_ref, qr_ref, cq_ref, kikr_ref, k_ref, v_ref, kn_ref, vb_ref,
                        pki_ref, pk_ref, pv_ref, pkn_ref, pkr_ref, pvb_ref, ckp_ref, ckn_ref,
                        oa_ref, ob_ref, biasp_ref, biasn_ref, *, topk):
    cq = cq_ref[:, 0:1]
    ckp, ckn = ckp_ref[...], ckn_ref[...]
    kikr = kikr_ref[...]
    past = pki_ref.shape[1]
    heads = lambda ref: lambda c: ref[pl.ds(c, past, stride=A_KV_HEADS), :].astype(BF16)
    kv_of = lambda ref: lambda c: ref[:, c * A_HEAD_DIM:(c + 1) * A_HEAD_DIM]
    past_a = (pki_ref[0].astype(BF16), heads(pk_ref), heads(pv_ref), ckp)
    new_a = (kikr[:, :64], kv_of(k_ref), kv_of(v_ref), ckn)
    head_of = lambda ref: lambda h: ref[:, h * B_V:(h + 1) * B_V]
    past_b = (head_of(pkn_ref), pkr_ref[0].astype(BF16), head_of(pvb_ref), ckp)
    new_b = (head_of(kn_ref), kikr[:, 64:], head_of(vb_ref), ckn)

    def store_b(h, o):
        ob_ref[:, h * B_V:(h + 1) * B_V] = o.astype(BF16)

    jobs = _mla_jobs(qn_ref, qr_ref, cq, [past_b, new_b], store_b)
    oa_ref[...] = _dsa_core(qi_ref[...], wi_ref[...], q_ref[...], cq, [past_a, new_a], topk,
                            [biasp_ref, biasn_ref], jobs).astype(BF16)
    for job in jobs:
        job()


def _attn_sample(qi, wi, q, qn, qr, cq, kikr, k, v, kn, vb, pki, pk, pv, pkn, pkr, pvb, ckp, ckn,
                 row0, nb, ts, topk):
    past = pk.shape[1]
    blk0 = row0 // ts
    nspec = lambda w: pl.BlockSpec((ts, w), lambda b: (blk0 + b, 0))
    pspec = lambda w: pl.BlockSpec((1, past, w), lambda b: (b, 0, 0))
    p2spec = lambda w: pl.BlockSpec((past, w), lambda b: (b, 0))
    kvspec = pl.BlockSpec((None, past * A_KV_HEADS, A_HEAD_DIM), lambda b: (b, 0, 0))
    pk = pk.reshape(nb, past * A_KV_HEADS, A_HEAD_DIM)
    pv = pv.reshape(nb, past * A_KV_HEADS, A_HEAD_DIM)
    return pl.pallas_call(
        functools.partial(_attn_sample_kernel, topk=topk),
        grid=(nb,),
        in_specs=[nspec(1024), nspec(128), nspec(1024), nspec(1024), nspec(512), nspec(128),
                  nspec(128), nspec(256), nspec(256), nspec(1024), nspec(1024),
                  pspec(64), kvspec, kvspec, p2spec(1024), pspec(64), p2spec(1024),
                  _const_spec(ckp.shape), _const_spec(ckn.shape)],
        out_specs=[pl.BlockSpec((ts, 1024), lambda b: (b, 0))] * 2,
        out_shape=[jax.ShapeDtypeStruct((nb * ts, 1024), BF16)] * 2,
        scratch_shapes=[pltpu.VMEM((ts, past), F32), pltpu.VMEM((ts, ts), F32)],
        compiler_params=_cparams("parallel"),
        name="attn_sample",
    )(qi, wi, q, qn, qr, cq, kikr, k, v, kn, vb, pki, pk, pv, pkn, pkr, pvb, ckp, ckn)


def _out_proj_kernel(h_ref, oap_ref, obp_ref, oas_ref, obs_ref, w_ref, o_ref, *, n_p, n_ps):
    half = oap_ref.shape[1]
    i = pl.program_id(0)

    def project(oa_ref, ob_ref):
        y = jnp.dot(oa_ref[...], w_ref[:half, :], preferred_element_type=F32)
        y = y + jnp.dot(ob_ref[...], w_ref[half:, :], preferred_element_type=F32)
        o_ref[...] = h_ref[...] + y

    pl.when(i < n_p)(functools.partial(project, oap_ref, obp_ref))
    pl.when(jnp.logical_and(i >= n_p, i < n_ps))(functools.partial(project, oas_ref, obs_ref))

    @pl.when(i >= n_ps)
    def _():
        o_ref[...] = h_ref[...]


def _out_proj(h, oa_p, ob_p, oa_s, ob_s, w, tm):
    r, d = h.shape
    n_p, n_s = oa_p.shape[0] // tm, oa_s.shape[0] // tm
    assert oa_p.shape[0] % tm == 0 and oa_s.shape[0] % tm == 0
    row = lambda wd: pl.BlockSpec((tm, wd), lambda i: (i, 0))
    pspec = lambda wd: pl.BlockSpec((tm, wd), lambda i: (jnp.minimum(i, n_p - 1), 0))
    sspec = lambda wd: pl.BlockSpec((tm, wd), lambda i: (jnp.clip(i - n_p, 0, n_s - 1), 0))
    return pl.pallas_call(
        functools.partial(_out_proj_kernel, n_p=n_p, n_ps=n_p + n_s),
        grid=(r // tm,),
        in_specs=[row(d), pspec(oa_p.shape[1]), pspec(ob_p.shape[1]), sspec(oa_s.shape[1]), sspec(ob_s.shape[1]),
                  _const_spec(w.shape)],
        out_specs=row(d),
        out_shape=jax.ShapeDtypeStruct((r, d), F32),
        compiler_params=_cparams("parallel"),
        name="out_proj",
    )(h, oa_p, ob_p, oa_s, ob_s, w)


def _mlp_kernel(h_ref, g_ref, wup_ref, wdn_ref, *rest, n_split, n_end):
    *o_refs, xn_ref, acc_ref = rest
    i, f = pl.program_id(0), pl.program_id(1)

    @pl.when(f == 0)
    def _():
        x = h_ref[...]
        xn_ref[...] = _rms(x, g_ref[...]).astype(BF16)
        acc_ref[...] = x

    u = jnp.dot(xn_ref[...], wup_ref[...], preferred_element_type=F32)
    u = jnp.maximum(u, 0.0)
    acc_ref[...] += jnp.dot((u * u).astype(BF16), wdn_ref[...], preferred_element_type=F32)

    @pl.when(f == pl.num_programs(1) - 1)
    def _():
        if len(o_refs) == 1:
            o_refs[0][...] = acc_ref[...]
        else:
            @pl.when(i < n_split)
            def _():
                o_refs[0][...] = acc_ref[...]

            @pl.when(jnp.logical_and(i >= n_split, i < n_end))
            def _():
                o_refs[1][...] = acc_ref[...]


def _mlp(h, g, wup, wdn, layer, tm, tf, split_rows=None):
    r, d = h.shape
    dff = wup.shape[2]
    n_end = r // tm
    if split_rows is None:
        n_split = 0
        out_specs = [pl.BlockSpec((tm, d), lambda i, f: (i, 0))]
        out_shape = [jax.ShapeDtypeStruct((r, d), F32)]
    else:
        a, b = split_rows
        assert a % tm == 0 and b % tm == 0
        n_split, n_b = a // tm, b // tm
        n_end = n_split + n_b
        out_specs = [pl.BlockSpec((tm, d), lambda i, f: (jnp.minimum(i, n_split - 1), 0)),
                     pl.BlockSpec((tm, d), lambda i, f: (jnp.clip(i - n_split, 0, n_b - 1), 0))]
        out_shape = [jax.ShapeDtypeStruct((a, d), F32), jax.ShapeDtypeStruct((b, d), F32)]
    res = pl.pallas_call(
        functools.partial(_mlp_kernel, n_split=n_split, n_end=n_end),
        grid=(r // tm, dff // tf),
        in_specs=[pl.BlockSpec((tm, d), lambda i, f: (i, 0)), pl.BlockSpec((1, d), lambda i, f: (0, 0)),
                  pl.BlockSpec((None, d, tf), lambda i, f: (layer, 0, f)),
                  pl.BlockSpec((None, tf, d), lambda i, f: (layer, f, 0))],
        out_specs=out_specs,
        out_shape=out_shape,
        scratch_shapes=[pltpu.VMEM((tm, d), BF16), pltpu.VMEM((tm, d), F32)],
        compiler_params=_cparams("parallel" if split_rows is None else "arbitrary", "arbitrary"),
        name="mlp",
    )(h, g, wup, wdn)
    return res[0] if split_rows is None else res


def _cd_proj_kernel(h_ref, g_ref, w_ref, z_ref, xn_ref):
    @pl.when(pl.program_id(1) == 0)
    def _():
        xn_ref[...] = _rms(h_ref[...], g_ref[...]).astype(BF16)

    z_ref[...] = jnp.dot(xn_ref[...], w_ref[...], preferred_element_type=F32)


def _cd_proj(h, g, w, tm, tn):
    r, d = h.shape
    n = w.shape[1]
    return pl.pallas_call(
        _cd_proj_kernel,
        grid=(r // tm, n // tn),
        in_specs=[pl.BlockSpec((tm, d), lambda i, j: (i, 0)), pl.BlockSpec((1, d), lambda i, j: (0, 0)),
                  pl.BlockSpec((d, tn), lambda i, j: (0, j))],
        out_specs=pl.BlockSpec((tm, tn), lambda i, j: (i, j)),
        out_shape=jax.ShapeDtypeStruct((r, n), F32),
        scratch_shapes=[pltpu.VMEM((tm, d), BF16)],
        compiler_params=_cparams("parallel", "arbitrary"),
        name="cd_proj",
    )(h, g, w)


def _rglru_kernel(cx_ref, cg_ref, valid_ref, prev0_ref, h0_ref, cw_ref, cb_ref, gw_ref, gab_ref, gxb_ref, lam_ref,
                  o_ref, hl_ref, prev_s, h_s, a_s, b_s):
    @pl.when(pl.program_id(1) == 0)
    def _():
        prev_s[...] = prev0_ref[0]
        h_s[...] = jnp.broadcast_to(h0_ref[0], h_s.shape)

    cx = cx_ref[...]
    tm = cx.shape[0]
    full = jnp.concatenate([prev_s[...], cx], axis=0)
    prev_s[...] = cx[tm - 8:, :]
    xc = cb_ref[...] + cx * cw_ref[CONV_W - 1:CONV_W, :]
    for j in range(CONV_W - 1):
        sh = CONV_W - 1 - j
        xc = xc + full[8 - sh:8 - sh + tm, :] * cw_ref[j:j + 1, :]
    valid = valid_ref[:, 0:1] >= 0
    softplus = jnp.logaddexp(-lam_ref[...], 0.0)
    for n in range(C_BLOCKS):
        sl = slice(n * C_BLOCK, (n + 1) * C_BLOCK)
        xb = xc[:, sl]
        gates = jnp.dot(xb.astype(BF16), gw_ref[n], preferred_element_type=F32)
        r = jax.nn.sigmoid(gates[:, :C_BLOCK] + gab_ref[:, sl])
        ig = jax.nn.sigmoid(gates[:, C_BLOCK:] + gxb_ref[:, sl])
        log_a = -RG_C * r * softplus[:, sl]
        a = jnp.exp(log_a)
        a_s[:, sl] = a
        b = jnp.sqrt(1.0 - a * a) * (ig * xb)
        b_s[:, sl] = jnp.where(valid, b, 0.0)

    rid = lax.broadcasted_iota(jnp.int32, h_s.shape, 0)

    def group(g, h):
        rows = pl.ds(pl.multiple_of(g * 8, 8), 8)
        ca, cb = a_s[rows, :], b_s[rows, :]
        for s in (1, 2, 4):
            m = rid >= s
            cb = jnp.where(m, ca * pltpu.roll(cb, s, 0) + cb, cb)
            ca = jnp.where(m, ca * pltpu.roll(ca, s, 0), ca)
        hh = ca * h + cb
        b_s[rows, :] = hh
        return jnp.broadcast_to(hh[7:8, :], hh.shape)

    h = lax.fori_loop(0, tm // 8, group, h_s[...])
    h_s[...] = h
    hl_ref[0] = h[0:1, :]
    o_ref[...] = (b_s[...] * jax.nn.gelu(cg_ref[...])).astype(BF16)


def _rglru(z, valid, prev0, h0, cw, cb, gw, gab, gxb, lam, row0, nb, t, tm):
    cwid = C_BLOCKS * C_BLOCK
    nj = t // tm
    b0 = row0 // tm
    zspec = lambda c: pl.BlockSpec((tm, cwid), lambda b, j: (b0 + b * nj + j, c))
    return pl.pallas_call(
        _rglru_kernel,
        grid=(nb, nj),
        in_specs=[zspec(0), zspec(1), pl.BlockSpec((tm, LANES), lambda b, j: (b0 + b * nj + j, 0)),
                  pl.BlockSpec((1, 8, cwid), lambda b, j: (b, 0, 0)), pl.BlockSpec((1, 1, cwid), lambda b, j: (b, 0, 0)),
                  _const_spec(cw.shape), _const_spec(cb.shape), _const_spec(gw.shape), _const_spec(gab.shape),
                  _const_spec(gxb.shape), _const_spec(lam.shape)],
        out_specs=[pl.BlockSpec((tm, cwid), lambda b, j: (b * nj + j, 0)),
                   pl.BlockSpec((1, 1, cwid), lambda b, j: (b, 0, 0))],
        out_shape=[jax.ShapeDtypeStruct((nb * t, cwid), BF16), jax.ShapeDtypeStruct((nb, 1, cwid), F32)],
        scratch_shapes=[pltpu.VMEM((8, cwid), F32), pltpu.VMEM((8, cwid), F32), pltpu.VMEM((tm, cwid), F32),
                        pltpu.VMEM((tm, cwid), F32)],
        compiler_params=_cparams("parallel", "arbitrary"),
        name="rglru",
    )(z, z, valid, prev0, h0, cw, cb, gw, gab, gxb, lam)


def _retention_kernel(q_ref, k_ref, v_ref, g_ref, cos_ref, sin_ref, s0_ref, gdn_ref, lg_ref,
                      o_ref, s_ref, *, blk):
    @pl.when(pl.program_id(1) == 0)
    def _():
        s_ref[...] = s0_ref[...]

    cos, sin_signed = cos_ref[...], sin_ref[...]

    def rope(x):
        return x * cos + pltpu.roll(x, D_KDIM // 2, 1) * sin_signed

    ri = lax.broadcasted_iota(jnp.int32, (blk, blk), 0)
    ci = lax.broadcasted_iota(jnp.int32, (blk, blk), 1)
    diff = (ri - ci).astype(F32)
    rows = lax.broadcasted_iota(jnp.int32, (blk, 1), 0).astype(F32)
    for hd in range(D_HEADS):
        lg = lg_ref[hd]
        qsl = slice(hd * D_KDIM, (hd + 1) * D_KDIM)
        vsl = slice(hd * D_VDIM, (hd + 1) * D_VDIM)
        q = rope(q_ref[:, qsl])
        k = rope(k_ref[:, qsl]) * (D_KDIM ** -0.5)
        decay = jnp.where(diff >= 0, jnp.exp(lg * jnp.maximum(diff, 0.0)), 0.0)
        qb, kb, vb = q.astype(BF16), k.astype(BF16), v_ref[:, vsl].astype(BF16)
        att = _dot_t(qb, kb) * decay
        o = jnp.dot(att.astype(BF16), vb, preferred_element_type=F32)
        s_prev = s_ref[0, hd]
        o = o + jnp.dot(qb, s_prev.astype(BF16), preferred_element_type=F32) * jnp.exp(lg * (rows + 1.0))
        kdec = (k * jnp.exp(lg * (blk - 1.0 - rows))).astype(BF16)
        s_ref[0, hd] = jnp.exp(lg * blk) * s_prev + lax.dot_general(kdec, vb, (((0,), (0,)), ((), ())),
                                                                    preferred_element_type=F32)
        od = _rms(o, gdn_ref[...]) * jax.nn.silu(g_ref[:, vsl])
        o_ref[:, vsl] = od.astype(BF16)


def _retention(z, cos, sin_signed, s0, gdn, log_gamma, col0, row0, nb, t, blk):
    nj = t // blk
    b0 = row0 // blk
    wid = D_HEADS * D_KDIM
    c0 = col0 // wid
    zspec = lambda c: pl.BlockSpec((blk, wid), lambda b, j: (b0 + b * nj + j, c0 + c))
    tspec = pl.BlockSpec((blk, D_KDIM), lambda b, j: (b0 + b * nj + j, 0))
    sspec = pl.BlockSpec((1, D_HEADS, D_KDIM, D_VDIM), lambda b, j: (b, 0, 0, 0))
    return pl.pallas_call(
        functools.partial(_retention_kernel, blk=blk),
        grid=(nb, nj),
        in_specs=[zspec(0), zspec(1), zspec(2), zspec(3), tspec, tspec, sspec, _const_spec(gdn.shape),
                  pl.BlockSpec(memory_space=pltpu.SMEM)],
        out_specs=[pl.BlockSpec((blk, wid), lambda b, j: (b * nj + j, 0)), sspec],
        out_shape=[jax.ShapeDtypeStruct((nb * t, wid), BF16), jax.ShapeDtypeStruct(s0.shape, F32)],
        compiler_params=_cparams("parallel", "arbitrary"),
        name="retention",
    )(z, z, z, z, cos, sin_signed, s0, gdn, log_gamma)


def _rope_tables(pos_groups, reps_groups, tail, d, width):
    inv = ROPE_THETA ** (-jnp.arange(0, d, 2, dtype=F32) / d)
    cos_rows, sin_rows = [], []
    for pos, reps in zip(pos_groups, reps_groups):
        ang = pos.astype(F32)[:, None] * inv[None, :]
        cos, sin = lax.optimization_barrier((jnp.cos(ang), jnp.sin(ang)))
        cos_rows.append(jnp.tile(jnp.tile(jnp.concatenate([cos, cos], axis=-1), (1, width // d)), (reps, 1)))
        sin_rows.append(jnp.tile(jnp.tile(jnp.concatenate([-sin, sin], axis=-1), (1, width // d)), (reps, 1)))
    cos_rows.append(jnp.ones((tail, width), F32))
    sin_rows.append(jnp.zeros((tail, width), F32))
    return jnp.concatenate(cos_rows, axis=0), jnp.concatenate(sin_rows, axis=0)


def kernel(x_prompt, x_sample, cache_a_k, cache_a_v, cache_a_kidx, cache_b_latent, cache_b_krope, state_c_conv,
           state_c_h, state_d_s, meta_tokens, norm_mix, norm_mlp, ab_w_in, ab_w_out, a_q_norm, a_k_norm,
           b_q_lat_norm, b_w_uq, b_kv_lat_norm, b_w_ukv, b_qn_norm, b_qr_norm, b_kn_norm, b_kr_norm, cd_w_in,
           cd_w_out, c_conv_w, c_conv_b, c_gate_a_w, c_gate_a_b, c_gate_x_w, c_gate_x_b, c_lambda, d_out_norm,
           mlp_w_up, mlp_w_down):
    nb_p, seq_p, d_model = x_prompt.shape
    nb_s, seq_s, _ = x_sample.shape
    past = cache_a_k.shape[2]
    n_meta = meta_tokens.shape[0]
    depth = norm_mix.shape[0]
    t_real = n_meta + seq_p
    tp = -(-t_real // ROW_BLOCK) * ROW_BLOCK
    pad = tp - t_real
    rp = nb_p * tp
    rs = nb_s * seq_s
    assert seq_s % 8 == 0 and rp % seq_s == 0 and past % 8 == 0
    assert seq_p % ROW_BLOCK == 0 and ROW_BLOCK % CHUNK == 0
    r_real = rp + rs
    tile = 512
    r_tot = -(-r_real // tile) * tile
    tm_out = math.gcd(tile, rp, rs)
    topk_p = min(TOPK_MAX, seq_p // 4)
    topk_s = min(TOPK_MAX, (past + seq_s) // 4)

    pos_p = jnp.concatenate([jnp.zeros((pad,), jnp.int32), jnp.arange(t_real, dtype=jnp.int32)])
    ck_p = jnp.concatenate([jnp.zeros((n_meta,), jnp.int32), 1 + jnp.arange(seq_p, dtype=jnp.int32) // CHUNK])
    cq_p = jnp.concatenate([jnp.full((pad,), -1, jnp.int32), ck_p])
    ckk_p = jnp.concatenate([jnp.full((pad,), 2 ** 30, jnp.int32), ck_p])[None, :]
    pos_s_all = jnp.arange(past + seq_s, dtype=jnp.int32)
    ck_s = pos_s_all // CHUNK
    pos_s = pos_s_all[past:]
    tail = r_tot - r_real
    cq_rows =jnp.concatenate([jnp.tile(cq_p, nb_p), jnp.tile(ck_s[past:], nb_s), jnp.full((tail,), -1, jnp.int32)])
    cq_rows = jnp.broadcast_to(cq_rows[:, None], (r_tot, LANES))
    tabs128 = _rope_tables((pos_p, pos_s), (nb_p, nb_s), tail, 128, 128)
    tabs64 = _rope_tables((pos_p, pos_s), (nb_p, nb_s), tail, 64, 128)
    tabs256 = _rope_tables((pos_p, pos_s), (nb_p, nb_s), tail, 256, 256)

    head = jnp.concatenate([jnp.zeros((pad, d_model), F32), meta_tokens.astype(F32)], axis=0)
    pieces = [p for b in range(nb_p) for p in (head, x_prompt[b])]
    h = jnp.concatenate(pieces + [x_sample.reshape(rs, d_model), jnp.zeros((tail, d_model), F32)], axis=0)
    mlp_up, mlp_down = mlp_w_up.astype(BF16), mlp_w_down.astype(BF16)

    def prompt_rows(x):
        return x[:rp].reshape(nb_p, tp, -1)[:, pad:]

    def sample_rows(x):
        return x[rp:rp + rs].reshape(nb_s, seq_s, -1)

    ab_p, ab_s, cd_p, cd_s = [], [], [], []
    for layer in range(depth):
        i = layer // 2
        gmix = norm_mix[layer][None, :]
        if layer % 2 == 0:
            w = ab_w_in[i]
            offs = np.cumsum([0, 1024, 256, 256, 1024, 64, 16, 512, 256, 64])
            sec = lambda n: w[:, offs[n]:offs[n + 1]]
            win = jnp.concatenate([sec(0), sec(1), sec(2), sec(3), sec(6), sec(7), sec(4), sec(8), sec(5),
                                   jnp.zeros((d_model, AB_IN_PAD - 3472), F32)], axis=1).astype(BF16)
            wuq = b_w_uq[i].reshape(-1, B_HEADS, B_NOPE + B_ROPE)
            wuq = jnp.concatenate([wuq[:, :, :B_NOPE].reshape(-1, B_HEADS * B_NOPE),
                                   wuq[:, :, B_NOPE:].reshape(-1, B_HEADS * B_ROPE)], axis=1).astype(BF16)
            wukv = b_w_ukv[i].reshape(-1, B_HEADS, B_NOPE + B_V)
            wukv = jnp.concatenate([wukv[:, :, :B_NOPE].reshape(-1, B_HEADS * B_NOPE),
                                    wukv[:, :, B_NOPE:].reshape(-1, B_HEADS * B_V)], axis=1).astype(BF16)
            two = lambda g: jnp.concatenate([g, g])[None, :]
            gkr2 = jnp.concatenate([jnp.ones((64,), F32), b_kr_norm[i]])[None, :]
            (q, kf, vf, kb, vb_a, qi, kikr, kikrb, wi, qn, qr, lat) = _ab_proj(
                h, gmix, win, wuq, a_q_norm[i][None, :], a_k_norm[i][None, :], b_q_lat_norm[i][None, :],
                b_kv_lat_norm[i][None, :], b_qn_norm[i][None, :], two(b_qr_norm[i]), gkr2,
                tabs128 + tabs64, tile)
            gkn = b_kn_norm[i][None, :]
            kn, vbb = _ukv(lat, wukv, gkn, 512)
            pkn, pvb = _ukv(cache_b_latent[i].reshape(nb_s * past, -1), wukv, gkn, 512)
            oa_p, ob_p = _attn_prompt(qi, wi, q, qn, qr, cq_rows, kikrb, kb, vb_a, kn, vbb, ckk_p, nb_p, tp, topk_p, 8)
            oa_s, ob_s = _attn_sample(qi, wi, q, qn, qr, cq_rows, kikrb, kb, vb_a, kn, vbb,
                                      cache_a_kidx[i], cache_a_k[i], cache_a_v[i], pkn, cache_b_krope[i], pvb,
                                      ck_s[None, :past], ck_s[None, past:], rp, nb_s, seq_s, topk_s)
            h = _out_proj(h, oa_p, ob_p, oa_s, ob_s, ab_w_out[i].astype(BF16), tm_out)
            kv4 = lambda x: x.reshape(x.shape[0], x.shape[1], A_KV_HEADS, A_HEAD_DIM)
            ab_p.append((kv4(prompt_rows(kf)), kv4(prompt_rows(vf)), prompt_rows(kikr)[..., :64],
                         prompt_rows(lat), prompt_rows(kikr)[..., 64:]))
            ab_s.append((kv4(sample_rows(kf)), kv4(sample_rows(vf)), sample_rows(kikr)[..., :64],
                         sample_rows(lat), sample_rows(kikr)[..., 64:]))
        else:
            z = _cd_proj(h, gmix, cd_w_in[i].astype(BF16), 1024 if r_tot % 1024 == 0 else tile, 2048)
            cwid = C_BLOCKS * C_BLOCK
            gw = jnp.concatenate([c_gate_a_w[i], c_gate_x_w[i]], axis=-1).astype(BF16)
            cargs = (c_conv_w[i], c_conv_b[i][None, :], gw, c_gate_a_b[i][None, :], c_gate_x_b[i][None, :],
                     c_lambda[i][None, :])
            oc_p, hl_p = _rglru(z, cq_rows, jnp.zeros((nb_p, 8, cwid), F32), jnp.zeros((nb_p, 1, cwid), F32),
                                *cargs, 0, nb_p, tp, ROW_BLOCK)
            prev_s = jnp.concatenate([jnp.zeros((nb_s, 8 - (CONV_W - 1), cwid), F32), state_c_conv[i]], axis=1)
            oc_s, hl_s = _rglru(z, cq_rows, prev_s, state_c_h[i][:, None, :], *cargs, rp, nb_s, seq_s, seq_s)
            hl_p, hl_s = hl_p[:, 0], hl_s[:, 0]
            log_gamma = jnp.log(1.0 - 2.0 ** (-5.0 - jnp.arange(D_HEADS, dtype=F32)))
            gdn = d_out_norm[i][None, :]
            od_p, ds_p = _retention(z, *tabs256, jnp.zeros((nb_p, D_HEADS, D_KDIM, D_VDIM), F32), gdn, log_gamma,
                                    2 * cwid, 0, nb_p, tp, ROW_BLOCK)
            od_s, ds_s = _retention(z, *tabs256, state_d_s[i], gdn, log_gamma, 2 * cwid, rp, nb_s, seq_s, seq_s)
            h = _out_proj(h, oc_p, od_p, oc_s, od_s, cd_w_out[i].astype(BF16), tm_out)
            nc = CONV_W - 1
            def seq_tails(row0, nb, t):
                n = min(nc, t)
                rows = row0 + t - n + (jnp.arange(nb, dtype=jnp.int32) * t)[:, None] + jnp.arange(n, dtype=jnp.int32)
                return jnp.take(z, rows.reshape(-1), axis=0)[:, :cwid].reshape(nb, n, cwid)

            cd_p.append((seq_tails(0, nb_p, tp), hl_p, ds_p))
            cd_s.append((jnp.concatenate([state_c_conv[i], seq_tails(rp, nb_s, seq_s)], axis=1)[:, -nc:], hl_s, ds_s))
        last = layer == depth - 1
        h = _mlp(h, norm_mlp[layer][None, :], mlp_up, mlp_down, layer, tm_out if last else tile, 1024,
                 split_rows=(rp, rs) if last else None)

    def stack(entries, j):
        return jnp.stack([e[j] for e in entries])

    y_p = h[0].reshape(nb_p, tp, d_model)[:, pad + n_meta:]
    y_s = h[1].reshape(nb_s, seq_s, d_model)
    return (y_p, y_s,
            stack(ab_p, 0), stack(ab_p, 1), stack(ab_p, 2), stack(ab_p, 3), stack(ab_p, 4),
            stack(cd_p, 0), stack(cd_p, 1), stack(cd_p, 2),
            stack(ab_s, 0), stack(ab_s, 1), stack(ab_s, 2), stack(ab_s, 3), stack(ab_s, 4),
            stack(cd_s, 0), stack(cd_s, 1), stack(cd_s, 2))
```

```python
import functools

import jax
import jax.numpy as jnp
import numpy as np
from jax import lax
from jax.experimental import pallas as pl
from jax.experimental.pallas import tpu as pltpu

F32 = jnp.float32
BF16 = jnp.bfloat16

CHUNK = 64
ROPE_THETA = 10000.0
EPS = 1e-6
A_HEADS, A_KV_HEADS, A_HEAD_DIM = 8, 2, 128
IDX_HEADS, IDX_DIM = 16, 64
TOPK_MAX = 256
B_HEADS, B_NOPE, B_ROPE, B_V = 8, 128, 64, 128
C_BLOCKS, C_BLOCK, CONV_W = 8, 128, 4
RG_C = 8.0
D_HEADS, D_KDIM, D_VDIM = 4, 256, 256

LANES = 128
ROW_BLOCK = 128
NEG = -1e30
INT_MIN = -2 ** 31
VMEM_LIMIT = 56 * 1024 * 1024


def _cparams(*sem):
    return pltpu.CompilerParams(dimension_semantics=sem, vmem_limit_bytes=VMEM_LIMIT)


def _const_spec(shape):
    nd = len(shape)
    return pl.BlockSpec(shape, lambda *_: (0,) * nd)


def _rms(x, g):
    ms = jnp.mean(x * x, axis=-1, keepdims=True)
    return x * lax.rsqrt(ms + EPS) * g


def _rms_half(x, g):
    lane = lax.broadcasted_iota(jnp.int32, x.shape, 1)
    lo = lane < 64
    xx = x * x
    s_lo = jnp.sum(jnp.where(lo, xx, 0.0), axis=-1, keepdims=True)
    s_hi = jnp.sum(jnp.where(lo, 0.0, xx), axis=-1, keepdims=True)
    ms = jnp.where(lo, s_lo, s_hi) * (1.0 / 64.0)
    return x * lax.rsqrt(ms + EPS) * g


def _rope128(x, cos, sin_signed):
    return x * cos + pltpu.roll(x, 64, 1) * sin_signed


def _rope64(x, cos, sin_signed):
    lane = lax.broadcasted_iota(jnp.int32, x.shape, 1)
    first = (lane % 64) < 32
    rot = jnp.where(first, pltpu.roll(x, 96, 1), pltpu.roll(x, 32, 1))
    return x * cos + rot * sin_signed


AB_COLS = dict(qa=(0, 1024), ka=(1024, 1280), va=(1280, 1536), qi=(1536, 2560), cq=(2560, 3072),
               ckv=(3072, 3328), kikr=(3328, 3456), wi=(3456, 3584))
AB_IN_PAD = 3584


def _ab_proj_kernel(h_ref, gmix_ref, win_ref, wuq_ref, gaq_ref, gak_ref, gqlat_ref, gkvlat_ref, gqn_ref,
                    gqr_ref, gkr_ref, c128_ref, s128_ref, c64_ref, s64_ref,
                    q_ref, kf_ref, vf_ref, kb_ref, vb_ref, qi_ref, kikr_ref, kikrb_ref, wi_ref, qn_ref, qr_ref,
                    lat_ref):
    xn = _rms(h_ref[...], gmix_ref[...]).astype(BF16)
    c128, s128, c64, s64 = c128_ref[...], s128_ref[...], c64_ref[...], s64_ref[...]

    def proj(name):
        a, b = AB_COLS[name]
        return jnp.dot(xn, win_ref[:, a:b], preferred_element_type=F32)

    z = proj('qa')
    qscale = A_HEAD_DIM ** -0.5
    for h in range(A_HEADS):
        x = _rope128(_rms(z[:, h * 128:(h + 1) * 128], gaq_ref[...]), c128, s128)
        q_ref[:, h * 128:(h + 1) * 128] = (x * qscale).astype(BF16)
    z = proj('ka')
    for h in range(A_KV_HEADS):
        x = _rope128(_rms(z[:, h * 128:(h + 1) * 128], gak_ref[...]), c128, s128)
        kf_ref[:, h * 128:(h + 1) * 128] = x
        kb_ref[:, h * 128:(h + 1) * 128] = x.astype(BF16)
    z = proj('va')
    vf_ref[...] = z
    vb_ref[...] = z.astype(BF16)
    z = proj('qi')
    for p in range(IDX_HEADS // 2):
        x = _rope64(z[:, p * 128:(p + 1) * 128], c64, s64)
        qi_ref[:, p * 128:(p + 1) * 128] = (x * (IDX_DIM ** -0.5)).astype(BF16)
    z = proj('kikr')
    lane = lax.broadcasted_iota(jnp.int32, z.shape, 1)
    x = jnp.where(lane < 64, z, _rms_half(z, gkr_ref[...]))
    x = _rope64(x, c64, s64)
    kikr_ref[...] = x
    kikrb_ref[...] = x.astype(BF16)
    wi_ref[...] = proj('wi') * (IDX_HEADS ** -0.5)
    lat = _rms(proj('ckv'), gkvlat_ref[...])
    lat_ref[...] = lat
    cq = _rms(proj('cq'), gqlat_ref[...]).astype(BF16)
    bscale = (B_NOPE + B_ROPE) ** -0.5
    zq = jnp.dot(cq, wuq_ref[:, :B_HEADS * B_NOPE], preferred_element_type=F32)
    for h in range(B_HEADS):
        x = _rms(zq[:, h * 128:(h + 1) * 128], gqn_ref[...])
        qn_ref[:, h * 128:(h + 1) * 128] = (x * bscale).astype(BF16)
    zq = jnp.dot(cq, wuq_ref[:, B_HEADS * B_NOPE:], preferred_element_type=F32)
    for p in range(B_HEADS // 2):
        x = _rope64(_rms_half(zq[:, p * 128:(p + 1) * 128], gqr_ref[...]), c64, s64)
        qr_ref[:, p * 128:(p + 1) * 128] = (x * bscale).astype(BF16)


def _ab_proj(h, gmix, win, wuq, gaq, gak, gqlat, gkvlat, gqn, gqr2, gkr2, tabs, tm):
    r, d = h.shape
    row = lambda w: pl.BlockSpec((tm, w), lambda i: (i, 0))
    outs = [(1024, BF16), (256, F32), (256, F32), (256, BF16), (256, BF16), (1024, BF16), (128, F32),
            (128, BF16), (128, F32), (1024, BF16), (512, BF16), (256, F32)]
    return pl.pallas_call(
        _ab_proj_kernel,
        grid=(r // tm,),
        in_specs=[row(d), _const_spec(gmix.shape), _const_spec(win.shape), _const_spec(wuq.shape),
                  _const_spec(gaq.shape), _const_spec(gak.shape), _const_spec(gqlat.shape),
                  _const_spec(gkvlat.shape), _const_spec(gqn.shape), _const_spec(gqr2.shape),
                  _const_spec(gkr2.shape), row(128), row(128), row(128), row(128)],
        out_specs=[row(w) for w, _ in outs],
        out_shape=[jax.ShapeDtypeStruct((r, w), dt) for w, dt in outs],
        compiler_params=_cparams("parallel"),
        name="ab_proj",
    )(h, gmix, win, wuq, gaq, gak, gqlat, gkvlat, gqn, gqr2, gkr2, *tabs)


def _ukv_kernel(lat_ref, w_ref, gkn_ref, kn_ref, vb_ref):
    lat = lat_ref[...].astype(BF16)
    z = jnp.dot(lat, w_ref[:, :B_HEADS * B_NOPE], preferred_element_type=F32)
    for h in range(B_HEADS):
        kn_ref[:, h * 128:(h + 1) * 128] = _rms(z[:, h * 128:(h + 1) * 128], gkn_ref[...]).astype(BF16)
    vb_ref[...] = jnp.dot(lat, w_ref[:, B_HEADS * B_NOPE:], preferred_element_type=F32).astype(BF16)


def _ukv(lat, w, gkn, tm):
    n = lat.shape[0]
    row = lambda wd: pl.BlockSpec((tm, wd), lambda i: (i, 0))
    return pl.pallas_call(
        _ukv_kernel,
        grid=(n // tm,),
        in_specs=[row(lat.shape[1]), _const_spec(w.shape), _const_spec(gkn.shape)],
        out_specs=[row(1024), row(1024)],
        out_shape=[jax.ShapeDtypeStruct((n, 1024), BF16)] * 2,
        compiler_params=_cparams("parallel"),
        name="mla_ukv",
    )(lat, w, gkn)


def _dot_t(a, b):
    return lax.dot_general(a, b, (((1,), (1,)), ((), ())), preferred_element_type=F32)


def _float_key(x):
    bits = pltpu.bitcast(x, jnp.int32)
    return bits ^ ((bits >> 31) & jnp.int32(0x7FFFFFFF))


def _counts_ge16(vals, cands):
    rows = cands[0].shape[0]
    c16 = [jnp.broadcast_to(c, (rows, LANES)).astype(jnp.int16) for c in cands]
    one, zero = jnp.int16(1), jnp.int16(0)
    accs = [jnp.zeros((rows, LANES), jnp.int16) for _ in cands]
    rest = [jnp.zeros((rows, 1), F32) for _ in cands]
    for v in vals:
        n_full = v.shape[1] // LANES * LANES
        for s0 in range(0, n_full, LANES):
            blk = v[:, s0:s0 + LANES]
            accs = [a + jnp.where(blk >= c, one, zero) for a, c in zip(accs, c16)]
        if n_full < v.shape[1]:
            blk = v[:, n_full:]
            w = blk.shape[1]
            rest = [r + jnp.sum(jnp.where(blk >= c[:, :w], one, zero).astype(F32), axis=-1, keepdims=True)
                    for r, c in zip(rest, c16)]
    return [jnp.sum(a.astype(F32), axis=-1, keepdims=True) + r for a, r in zip(accs, rest)]


SEARCH_STEPS = 8
UNROLL_SEARCH_MAX_KEYS = 1152


def _search16(vals, k, side_jobs):
    def advance(t, step):
        cnts = _counts_ge16(vals, [t + j * step for j in (1, 2, 3)])
        digit = functools.reduce(jnp.add, [jnp.where(c >= float(k), 1, 0) for c in cnts])
        return t + digit * step

    t = jnp.full((vals[0].shape[0], 1), -32768, jnp.int32)
    if sum(v.shape[1] for v in vals) > UNROLL_SEARCH_MAX_KEYS:
        return lax.fori_loop(0, SEARCH_STEPS, lambda i, t: advance(t, jnp.left_shift(jnp.int32(1), 14 - 2 * i)), t)
    for i in range(SEARCH_STEPS):
        t = advance(t, 1 << (14 - 2 * i))
        if i % 2 == 1 and side_jobs:
            side_jobs.pop(0)()
    return t


def _kth_largest(keys, k, side_jobs):
    his = [(key >> 16).astype(jnp.int16) for key in keys]
    los = [((key & 0xFFFF) - 32768).astype(jnp.int16) for key in keys]
    h = _search16(his, k, side_jobs)
    top, bottom = jnp.int16(32767), jnp.int16(-32768)
    los2 = []
    for hi, lo in zip(his, los):
        h16 = jnp.broadcast_to(h, hi.shape).astype(jnp.int16)
        los2.append(jnp.where(hi == h16, lo, jnp.where(hi > h16, top, bottom)))
    l = _search16(los2, k, side_jobs)
    return (h << 16) | ((l + 32768) & 0xFFFF)


def _softmax_pv(ss, vs):
    m = functools.reduce(jnp.maximum, [jnp.max(s, axis=-1, keepdims=True) for s in ss])
    acc = functools.reduce(jnp.add, [
        jnp.dot(jnp.exp(s - m).astype(BF16), jnp.concatenate([v, jnp.ones_like(v)], axis=1),
                preferred_element_type=F32) for s, v in zip(ss, vs)])
    d = vs[0].shape[1]
    return acc[:, :d] / acc[:, d:]


def _kv_head(x, c):
    if callable(x):
        return x(c)
    return x[c] if isinstance(x, tuple) else x[:, c * 128:(c + 1) * 128]


def _tie_break(keys, thr, topk, bias_refs):
    nq = keys[0].shape[0]
    n_gt = functools.reduce(jnp.add, [jnp.sum(jnp.where(key > thr, 1.0, 0.0), axis=-1, keepdims=True)
                                      for key in keys])
    need = float(topk) - n_gt
    r_i = lax.broadcasted_iota(jnp.int32, (LANES, LANES), 0)
    c_i = lax.broadcasted_iota(jnp.int32, (LANES, LANES), 1)
    before = jnp.where(r_i < c_i, 1.0, 0.0).astype(BF16)
    seen = jnp.zeros((nq, 1), F32)
    for key, ref in zip(keys, bias_refs):
        for s0 in range(0, key.shape[1], LANES):
            blk = key[:, s0:s0 + LANES]
            w = blk.shape[1]
            eq = jnp.where(blk == thr, 1.0, 0.0)
            rank = seen + jnp.dot(eq.astype(BF16), before[:w, :w], preferred_element_type=F32)
            sel = jnp.logical_or(blk > thr, jnp.logical_and(eq > 0.0, rank < need))
            ref[:, s0:s0 + w] = jnp.where(jnp.logical_and(sel, blk > INT_MIN), 0.0, NEG)
            seen = seen + jnp.sum(eq, axis=-1, keepdims=True)


def _dsa_core(qi, wi, q, cq, pieces, topk, bias_refs, side_jobs):
    nq = qi.shape[0]
    keys = []
    for ki, _, _, ck in pieces:
        score = jnp.zeros((nq, ki.shape[0]), F32)
        for h in range(IDX_HEADS):
            s_h = _dot_t(qi[:, h * 64:(h + 1) * 64], ki)
            score = score + jnp.maximum(s_h, 0.0) * wi[:, h:h + 1]
        keys.append(jnp.where(ck <= cq, _float_key(score), INT_MIN))
    thr = _kth_largest(keys, topk, side_jobs)
    n_ge = jnp.zeros((nq, 1), F32)
    for key, ref in zip(keys, bias_refs):
        ge = key >= thr
        ref[...] = jnp.where(jnp.logical_and(ge, key > INT_MIN), 0.0, NEG)
        n_ge = n_ge + jnp.sum(jnp.where(ge, 1.0, 0.0), axis=-1, keepdims=True)
    tied = jnp.max(jnp.where(jnp.logical_and(n_ge > float(topk), thr > INT_MIN), 1.0, 0.0))
    pl.when(tied > 0.0)(functools.partial(_tie_break, keys, thr, topk, bias_refs))
    group = A_HEADS // A_KV_HEADS
    biases = [jnp.concatenate([ref[...]] * group, axis=0) for ref in bias_refs]
    outs = []
    for c in range(A_KV_HEADS):
        qg = jnp.concatenate([q[:, (c * group + g) * 128:(c * group + g + 1) * 128] for g in range(group)], axis=0)
        ss = [_dot_t(qg, _kv_head(k, c)) + bias for (_, k, _, _), bias in zip(pieces, biases)]
        o = _softmax_pv(ss, [_kv_head(v, c) for _, _, v, _ in pieces])
        outs.extend([o[g * nq:(g + 1) * nq] for g in range(group)])
    return jnp.concatenate(outs, axis=-1)


def _mla_jobs(qn, qr, cq, pieces, store):
    biases = [jnp.where(ck <= cq, 0.0, NEG) for _, _, _, ck in pieces]

    def head(h):
        ss = []
        for (kn, kr, _, _), bias in zip(pieces, biases):
            s = _dot_t(qn[:, h * B_NOPE:(h + 1) * B_NOPE], kn(h))
            s = s + _dot_t(qr[:, h * B_ROPE:(h + 1) * B_ROPE], kr)
            ss.append(s + bias)
        store(h, _softmax_pv(ss, [vb(h) for _, _, vb, _ in pieces]))

    return [functools.partial(head, h) for h in range(B_HEADS)]


def _attn_prompt_kernel(qi_ref, wi_ref, q_ref, qn_ref, qr_ref, cq_ref, mkikr_ref, mk_ref, mv_ref, mkn_ref, mvb_ref,
                        ckm_ref, *rest, topk, s_len):
    nf = 6 if s_len else 0
    oa_ref, ob_ref = rest[nf + 2:nf + 4]
    bias_refs = list(rest[nf + 4:])
    cq = cq_ref[:, 0:1]
    valid = cq >= 0
    head_of = lambda ref, n: lambda h: ref[:n, h * B_V:(h + 1) * B_V]
    kv_of = lambda ref, n: lambda c: ref[:n, c * A_HEAD_DIM:(c + 1) * A_HEAD_DIM]
    mb = mkikr_ref.shape[0]
    mkikr = mkikr_ref[...]
    ckm = ckm_ref[...]
    pieces_a = [(mkikr[:, :64], kv_of(mk_ref, mb), kv_of(mv_ref, mb), ckm)]
    pieces_b = [(head_of(mkn_ref, mb), mkikr[:, 64:], head_of(mvb_ref, mb), ckm)]
    if s_len:
        fkikr_ref, fk_ref, fv_ref, fkn_ref, fvb_ref, ckf_ref = rest[:nf]
        fkikr = fkikr_ref[:s_len, :]
        ckf = ckf_ref[:, :s_len]
        pieces_a.append((fkikr[:, :64], kv_of(fk_ref, s_len), kv_of(fv_ref, s_len), ckf))
        pieces_b.append((head_of(fkn_ref, s_len), fkikr[:, 64:], head_of(fvb_ref, s_len), ckf))

    def store_b(h, o):
        ob_ref[:, h * B_V:(h + 1) * B_V] = jnp.where(valid, o, 0.0).astype(BF16)

    jobs = _mla_jobs(qn_ref, qr_ref, cq, pieces_b, store_b)
    oa = _dsa_core(qi_ref[...], wi_ref[...], q_ref[...], cq, pieces_a, topk, bias_refs, jobs)
    oa_ref[...] = jnp.where(valid, oa, 0.0).astype(BF16)
    for job in jobs:
        job()


def _attn_prompt(rows, ckm, ckf, bufs, nb, seq, meta_row0, mb, topk, n_ranges):
    qi, wi, q, qn, qr, cq, kikr, k, v, kn, vb = rows
    tq = ROW_BLOCK
    nj = seq // tq
    edges = sorted({-(-nj * r // n_ranges) for r in range(n_ranges + 1)})
    mspec = lambda w: pl.BlockSpec((mb, w), lambda b, j: (meta_row0 // mb, 0))
    fspec = lambda w: pl.BlockSpec((seq, w), lambda b, j: (b, 0))
    key_arrays = (kikr, k, v, kn, vb)
    key_widths = (128, 256, 256, 1024, 1024)
    for j0, j1 in [(None, None)] + list(zip(edges[:-1], edges[1:])):
        if j0 is None:
            grid, s_len = (1, 1), 0
            qspec = lambda w: pl.BlockSpec((tq, w), lambda b, j: (meta_row0 // tq, 0))
            frame_specs, frame_args = [], ()
            scratch = [pltpu.VMEM((tq, mb), F32)]
        else:
            grid, s_len = (nb, j1 - j0), j1 * tq
            qspec = lambda w, j0=j0: pl.BlockSpec((tq, w), lambda b, j: (b * nj + j0 + j, 0))
            frame_specs = [fspec(w) for w in key_widths] + [_const_spec(ckf.shape)]
            frame_args = key_arrays + (ckf,)
            scratch = [pltpu.VMEM((tq, mb), F32), pltpu.VMEM((tq, s_len), F32)]
        n_in = 12 + len(frame_specs)
        bufs = pl.pallas_call(
            functools.partial(_attn_prompt_kernel, topk=topk, s_len=s_len),
            grid=grid,
            in_specs=[qspec(1024), qspec(128), qspec(1024), qspec(1024), qspec(512), qspec(128)]
                     + [mspec(w) for w in key_widths] + [_const_spec(ckm.shape)] + frame_specs
                     + [pl.BlockSpec(memory_space=pl.ANY)] * 2,
            out_specs=[qspec(1024), qspec(1024)],
            out_shape=[jax.ShapeDtypeStruct(bufs[0].shape, BF16)] * 2,
            scratch_shapes=scratch,
            input_output_aliases={n_in: 0, n_in + 1: 1},
            compiler_params=_cparams("parallel", "arbitrary"),
            name="attn_prompt",
        )(qi, wi, q, qn, qr, cq, *key_arrays, ckm, *frame_args, *bufs)
    return bufs


def _attn_sample_kernel(qi_ref, wi_ref, q_ref, qn_ref, qr_ref, cq_ref, kikr_ref, k_ref, v_ref, kn_ref, vb_ref,
                        pki_ref, pk_ref, pv_ref, pkn_ref, pkr_ref, pvb_ref, ckp_ref, ckn_ref, bufa_ref, bufb_ref,
                        oa_ref, ob_ref, biasp_ref, biasn_ref, *, topk):
    cq = cq_ref[:, 0:1]
    ckp, ckn = ckp_ref[...], ckn_ref[...]
    kikr = kikr_ref[...]
    past = pki_ref.shape[1]
    heads = lambda ref: lambda c: ref[pl.ds(c, past, stride=A_KV_HEADS), :].astype(BF16)
    kv_of = lambda ref: lambda c: ref[:, c * A_HEAD_DIM:(c + 1) * A_HEAD_DIM]
    past_a = (pki_ref[0].astype(BF16), heads(pk_ref), heads(pv_ref), ckp)
    new_a = (kikr[:, :64], kv_of(k_ref), kv_of(v_ref), ckn)
    head_of = lambda ref: lambda h: ref[:, h * B_V:(h + 1) * B_V]
    past_b = (head_of(pkn_ref), pkr_ref[0].astype(BF16), head_of(pvb_ref), ckp)
    new_b = (head_of(kn_ref), kikr[:, 64:], head_of(vb_ref), ckn)

    def store_b(h, o):
        ob_ref[:, h * B_V:(h + 1) * B_V] = o.astype(BF16)

    jobs = _mla_jobs(qn_ref, qr_ref, cq, [past_b, new_b], store_b)
    oa_ref[...] = _dsa_core(qi_ref[...], wi_ref[...], q_ref[...], cq, [past_a, new_a], topk,
                            [biasp_ref, biasn_ref], jobs).astype(BF16)
    for job in jobs:
        job()


def _attn_sample(rows, pki, pk, pv, pkn, pkr, pvb, ckp, ckn, bufs, row0, nb, ts, topk):
    qi, wi, q, qn, qr, cq, kikr, k, v, kn, vb = rows
    past = pk.shape[1]
    blk0 = row0 // ts
    nspec = lambda w: pl.BlockSpec((ts, w), lambda b: (blk0 + b, 0))
    pspec = lambda w: pl.BlockSpec((1, past, w), lambda b: (b, 0, 0))
    p2spec = lambda w: pl.BlockSpec((past, w), lambda b: (b, 0))
    kvspec = pl.BlockSpec((None, past * A_KV_HEADS, A_HEAD_DIM), lambda b: (b, 0, 0))
    pk = pk.reshape(nb, past * A_KV_HEADS, A_HEAD_DIM)
    pv = pv.reshape(nb, past * A_KV_HEADS, A_HEAD_DIM)
    return pl.pallas_call(
        functools.partial(_attn_sample_kernel, topk=topk),
        grid=(nb,),
        in_specs=[nspec(1024), nspec(128), nspec(1024), nspec(1024), nspec(512), nspec(128),
                  nspec(128), nspec(256), nspec(256), nspec(1024), nspec(1024),
                  pspec(64), kvspec, kvspec, p2spec(1024), pspec(64), p2spec(1024),
                  _const_spec(ckp.shape), _const_spec(ckn.shape)] + [pl.BlockSpec(memory_space=pl.ANY)] * 2,
        out_specs=[nspec(1024), nspec(1024)],
        out_shape=[jax.ShapeDtypeStruct(bufs[0].shape, BF16)] * 2,
        scratch_shapes=[pltpu.VMEM((ts, past), F32), pltpu.VMEM((ts, ts), F32)],
        input_output_aliases={19: 0, 20: 1},
        compiler_params=_cparams("parallel"),
        name="attn_sample",
    )(qi, wi, q, qn, qr, cq, kikr, k, v, kn, vb, pki, pk, pv, pkn, pkr, pvb, ckp, ckn, *bufs)


def _out_proj_kernel(h_ref, oa_ref, ob_ref, w_ref, o_ref):
    half = oa_ref.shape[1]
    y = jnp.dot(oa_ref[...], w_ref[:half, :], preferred_element_type=F32)
    y = y + jnp.dot(ob_ref[...], w_ref[half:, :], preferred_element_type=F32)
    o_ref[...] = h_ref[...] + y


def _out_proj(h, oa, ob, w, tm):
    r, d = h.shape
    row = lambda wd: pl.BlockSpec((tm, wd), lambda i: (i, 0))
    return pl.pallas_call(
        _out_proj_kernel,
        grid=(r // tm,),
        in_specs=[row(d), row(oa.shape[1]), row(ob.shape[1]), _const_spec(w.shape)],
        out_specs=row(d),
        out_shape=jax.ShapeDtypeStruct((r, d), F32),
        compiler_params=_cparams("parallel"),
        name="out_proj",
    )(h, oa, ob, w)


def _mlp_kernel(h_ref, g_ref, wup_ref, wdn_ref, *rest, n_split, n_end):
    *o_refs, xn_ref, acc_ref = rest
    i, f = pl.program_id(0), pl.program_id(1)

    @pl.when(f == 0)
    def _():
        x = h_ref[...]
        xn_ref[...] = _rms(x, g_ref[...]).astype(BF16)
        acc_ref[...] = x

    u = jnp.dot(xn_ref[...], wup_ref[...], preferred_element_type=F32)
    u = jnp.maximum(u, 0.0)
    acc_ref[...] += jnp.dot((u * u).astype(BF16), wdn_ref[...], preferred_element_type=F32)

    @pl.when(f == pl.num_programs(1) - 1)
    def _():
        if len(o_refs) == 1:
            o_refs[0][...] = acc_ref[...]
        else:
            @pl.when(i < n_split)
            def _():
                o_refs[0][...] = acc_ref[...]

            @pl.when(jnp.logical_and(i >= n_split, i < n_end))
            def _():
                o_refs[1][...] = acc_ref[...]


def _mlp(h, g, wup, wdn, layer, tm, tf, split_rows=None):
    r, d = h.shape
    dff = wup.shape[2]
    n_end = r // tm
    if split_rows is None:
        n_split = 0
        out_specs = [pl.BlockSpec((tm, d), lambda i, f: (i, 0))]
        out_shape = [jax.ShapeDtypeStruct((r, d), F32)]
    else:
        a, b = split_rows
        assert a % tm == 0 and b % tm == 0
        n_split, n_b = a // tm, b // tm
        n_end = n_split + n_b
        out_specs = [pl.BlockSpec((tm, d), lambda i, f: (jnp.minimum(i, n_split - 1), 0)),
                     pl.BlockSpec((tm, d), lambda i, f: (jnp.clip(i - n_split, 0, n_b - 1), 0))]
        out_shape = [jax.ShapeDtypeStruct((a, d), F32), jax.ShapeDtypeStruct((b, d), F32)]
    res = pl.pallas_call(
        functools.partial(_mlp_kernel, n_split=n_split, n_end=n_end),
        grid=(r // tm, dff // tf),
        in_specs=[pl.BlockSpec((tm, d), lambda i, f: (i, 0)), pl.BlockSpec((1, d), lambda i, f: (0, 0)),
                  pl.BlockSpec((None, d, tf), lambda i, f: (layer, 0, f)),
                  pl.BlockSpec((None, tf, d), lambda i, f: (layer, f, 0))],
        out_specs=out_specs,
        out_shape=out_shape,
        scratch_shapes=[pltpu.VMEM((tm, d), BF16), pltpu.VMEM((tm, d), F32)],
        compiler_params=_cparams("parallel" if split_rows is None else "arbitrary", "arbitrary"),
        name="mlp",
    )(h, g, wup, wdn)
    return res[0] if split_rows is None else res


def _cd_proj_kernel(h_ref, g_ref, w_ref, z_ref, xn_ref):
    @pl.when(pl.program_id(1) == 0)
    def _():
        xn_ref[...] = _rms(h_ref[...], g_ref[...]).astype(BF16)

    z_ref[...] = jnp.dot(xn_ref[...], w_ref[...], preferred_element_type=F32)


def _cd_proj(h, g, w, tm, tn):
    r, d = h.shape
    n = w.shape[1]
    return pl.pallas_call(
        _cd_proj_kernel,
        grid=(r // tm, n // tn),
        in_specs=[pl.BlockSpec((tm, d), lambda i, j: (i, 0)), pl.BlockSpec((1, d), lambda i, j: (0, 0)),
                  pl.BlockSpec((d, tn), lambda i, j: (0, j))],
        out_specs=pl.BlockSpec((tm, tn), lambda i, j: (i, j)),
        out_shape=jax.ShapeDtypeStruct((r, n), F32),
        scratch_shapes=[pltpu.VMEM((tm, d), BF16)],
        compiler_params=_cparams("parallel", "arbitrary"),
        name="cd_proj",
    )(h, g, w)


def _rglru_kernel(cx_ref, cg_ref, valid_ref, prev0_ref, h0_ref, cw_ref, cb_ref, gw_ref, gab_ref, gxb_ref, lam_ref,
                  buf_ref, o_ref, hl_ref, prev_s, h_s, a_s, b_s):
    @pl.when(pl.program_id(1) == 0)
    def _():
        prev_s[...] = prev0_ref[0]
        h_s[...] = jnp.broadcast_to(h0_ref[0], h_s.shape)

    cx = cx_ref[...]
    tm = cx.shape[0]
    full = jnp.concatenate([prev_s[...], cx], axis=0)
    prev_s[...] = cx[tm - 8:, :]
    xc = cb_ref[...] + cx * cw_ref[CONV_W - 1:CONV_W, :]
    for j in range(CONV_W - 1):
        sh = CONV_W - 1 - j
        xc = xc + full[8 - sh:8 - sh + tm, :] * cw_ref[j:j + 1, :]
    valid = valid_ref[:, 0:1] >= 0
    softplus = jnp.logaddexp(-lam_ref[...], 0.0)
    for n in range(C_BLOCKS):
        sl = slice(n * C_BLOCK, (n + 1) * C_BLOCK)
        xb = xc[:, sl]
        gates = jnp.dot(xb.astype(BF16), gw_ref[n], preferred_element_type=F32)
        r = jax.nn.sigmoid(gates[:, :C_BLOCK] + gab_ref[:, sl])
        ig = jax.nn.sigmoid(gates[:, C_BLOCK:] + gxb_ref[:, sl])
        log_a = -RG_C * r * softplus[:, sl]
        a = jnp.exp(log_a)
        a_s[:, sl] = a
        b = jnp.sqrt(1.0 - a * a) * (ig * xb)
        b_s[:, sl] = jnp.where(valid, b, 0.0)

    rid = lax.broadcasted_iota(jnp.int32, h_s.shape, 0)

    def group(g, h):
        rows = pl.ds(pl.multiple_of(g * 8, 8), 8)
        ca, cb = a_s[rows, :], b_s[rows, :]
        for s in (1, 2, 4):
            m = rid >= s
            cb = jnp.where(m, ca * pltpu.roll(cb, s, 0) + cb, cb)
            ca = jnp.where(m, ca * pltpu.roll(ca, s, 0), ca)
        hh = ca * h + cb
        b_s[rows, :] = hh
        return jnp.broadcast_to(hh[7:8, :], hh.shape)

    h = lax.fori_loop(0, tm // 8, group, h_s[...])
    h_s[...] = h
    hl_ref[0] = h[0:1, :]
    o_ref[...] = (b_s[...] * jax.nn.gelu(cg_ref[...])).astype(BF16)


def _rglru(z, valid, prev0, h0, cw, cb, gw, gab, gxb, lam, buf, row0, nb, t, tm):
    cwid = C_BLOCKS * C_BLOCK
    nj = t // tm
    b0 = row0 // tm
    zspec = lambda c: pl.BlockSpec((tm, cwid), lambda b, j: (b0 + b * nj + j, c))
    return pl.pallas_call(
        _rglru_kernel,
        grid=(nb, nj),
        in_specs=[zspec(0), zspec(1), pl.BlockSpec((tm, LANES), lambda b, j: (b0 + b * nj + j, 0)),
                  pl.BlockSpec((1, 8, cwid), lambda b, j: (b, 0, 0)), pl.BlockSpec((1, 1, cwid), lambda b, j: (b, 0, 0)),
                  _const_spec(cw.shape), _const_spec(cb.shape), _const_spec(gw.shape), _const_spec(gab.shape),
                  _const_spec(gxb.shape), _const_spec(lam.shape), pl.BlockSpec(memory_space=pl.ANY)],
        out_specs=[pl.BlockSpec((tm, cwid), lambda b, j: (b0 + b * nj + j, 0)),
                   pl.BlockSpec((1, 1, cwid), lambda b, j: (b, 0, 0))],
        out_shape=[jax.ShapeDtypeStruct(buf.shape, BF16), jax.ShapeDtypeStruct((nb, 1, cwid), F32)],
        input_output_aliases={11: 0},
        scratch_shapes=[pltpu.VMEM((8, cwid), F32), pltpu.VMEM((8, cwid), F32), pltpu.VMEM((tm, cwid), F32),
                        pltpu.VMEM((tm, cwid), F32)],
        compiler_params=_cparams("parallel", "arbitrary"),
        name="rglru",
    )(z, z, valid, prev0, h0, cw, cb, gw, gab, gxb, lam, buf)


def _retention_kernel(q_ref, k_ref, v_ref, g_ref, cos_ref, sin_ref, s0_ref, gdn_ref, lg_ref, buf_ref,
                      o_ref, s_ref, *, blk):
    @pl.when(pl.program_id(1) == 0)
    def _():
        s_ref[...] = s0_ref[...]

    cos, sin_signed = cos_ref[...], sin_ref[...]

    def rope(x):
        return x * cos + pltpu.roll(x, D_KDIM // 2, 1) * sin_signed

    ri = lax.broadcasted_iota(jnp.int32, (blk, blk), 0)
    ci = lax.broadcasted_iota(jnp.int32, (blk, blk), 1)
    diff = (ri - ci).astype(F32)
    rows = lax.broadcasted_iota(jnp.int32, (blk, 1), 0).astype(F32)
    for hd in range(D_HEADS):
        lg = lg_ref[hd]
        qsl = slice(hd * D_KDIM, (hd + 1) * D_KDIM)
        vsl = slice(hd * D_VDIM, (hd + 1) * D_VDIM)
        q = rope(q_ref[:, qsl])
        k = rope(k_ref[:, qsl]) * (D_KDIM ** -0.5)
        decay = jnp.where(diff >= 0, jnp.exp(lg * jnp.maximum(diff, 0.0)), 0.0)
        qb, kb, vb = q.astype(BF16), k.astype(BF16), v_ref[:, vsl].astype(BF16)
        att = _dot_t(qb, kb) * decay
        o = jnp.dot(att.astype(BF16), vb, preferred_element_type=F32)
        s_prev = s_ref[0, hd]
        o = o + jnp.dot(qb, s_prev.astype(BF16), preferred_element_type=F32) * jnp.exp(lg * (rows + 1.0))
        kdec = (k * jnp.exp(lg * (blk - 1.0 - rows))).astype(BF16)
        s_ref[0, hd] = jnp.exp(lg * blk) * s_prev + lax.dot_general(kdec, vb, (((0,), (0,)), ((), ())),
                                                                    preferred_element_type=F32)
        od = _rms(o, gdn_ref[...]) * jax.nn.silu(g_ref[:, vsl])
        o_ref[:, vsl] = od.astype(BF16)


def _retention(z, cos, sin_signed, s0, gdn, log_gamma, buf, col0, row0, nb, t, blk):
    nj = t // blk
    b0 = row0 // blk
    wid = D_HEADS * D_KDIM
    c0 = col0 // wid
    zspec = lambda c: pl.BlockSpec((blk, wid), lambda b, j: (b0 + b * nj + j, c0 + c))
    tspec = pl.BlockSpec((blk, D_KDIM), lambda b, j: (b0 + b * nj + j, 0))
    sspec = pl.BlockSpec((1, D_HEADS, D_KDIM, D_VDIM), lambda b, j: (b, 0, 0, 0))
    return pl.pallas_call(
        functools.partial(_retention_kernel, blk=blk),
        grid=(nb, nj),
        in_specs=[zspec(0), zspec(1), zspec(2), zspec(3), tspec, tspec, sspec, _const_spec(gdn.shape),
                  pl.BlockSpec(memory_space=pltpu.SMEM), pl.BlockSpec(memory_space=pl.ANY)],
        out_specs=[pl.BlockSpec((blk, wid), lambda b, j: (b0 + b * nj + j, 0)), sspec],
        out_shape=[jax.ShapeDtypeStruct(buf.shape, BF16), jax.ShapeDtypeStruct(s0.shape, F32)],
        input_output_aliases={9: 0},
        compiler_params=_cparams("parallel", "arbitrary"),
        name="retention",
    )(z, z, z, z, cos, sin_signed, s0, gdn, log_gamma, buf)


def _rope_tables(pos_groups, reps_groups, tail, d, width):
    inv = ROPE_THETA ** (-jnp.arange(0, d, 2, dtype=F32) / d)
    cos_rows, sin_rows = [], []
    for pos, reps in zip(pos_groups, reps_groups):
        ang = pos.astype(F32)[:, None] * inv[None, :]
        cos, sin = lax.optimization_barrier((jnp.cos(ang), jnp.sin(ang)))
        cos_rows.append(jnp.tile(jnp.tile(jnp.concatenate([cos, cos], axis=-1), (1, width // d)), (reps, 1)))
        sin_rows.append(jnp.tile(jnp.tile(jnp.concatenate([-sin, sin], axis=-1), (1, width // d)), (reps, 1)))
    cos_rows.append(jnp.ones((tail, width), F32))
    sin_rows.append(jnp.zeros((tail, width), F32))
    return jnp.concatenate(cos_rows, axis=0), jnp.concatenate(sin_rows, axis=0)


def kernel(x_prompt, x_sample, cache_a_k, cache_a_v, cache_a_kidx, cache_b_latent, cache_b_krope, state_c_conv,
           state_c_h, state_d_s, meta_tokens, norm_mix, norm_mlp, ab_w_in, ab_w_out, a_q_norm, a_k_norm,
           b_q_lat_norm, b_w_uq, b_kv_lat_norm, b_w_ukv, b_qn_norm, b_qr_norm, b_kn_norm, b_kr_norm, cd_w_in,
           cd_w_out, c_conv_w, c_conv_b, c_gate_a_w, c_gate_a_b, c_gate_x_w, c_gate_x_b, c_lambda, d_out_norm,
           mlp_w_up, mlp_w_down):
    nb_p, seq_p, d_model = x_prompt.shape
    nb_s, seq_s, _ = x_sample.shape
    past = cache_a_k.shape[2]
    n_meta = meta_tokens.shape[0]
    depth = norm_mix.shape[0]
    mb = -(-n_meta // ROW_BLOCK) * ROW_BLOCK
    pad = mb - n_meta
    rf = nb_p * seq_p
    rs = nb_s * seq_s
    row_m, row_s = rf, rf + mb
    r_tot = rf + mb + rs
    assert seq_p % ROW_BLOCK == 0 and ROW_BLOCK % CHUNK == 0 and seq_s % 8 == 0 and past % 8 == 0
    assert row_s % seq_s == 0 and r_tot % ROW_BLOCK == 0
    blocks = r_tot // ROW_BLOCK
    tile = ROW_BLOCK * max(d for d in range(1, 9) if blocks % d == 0)
    topk_p = min(TOPK_MAX, seq_p // 4)
    topk_s = min(TOPK_MAX, (past + seq_s) // 4)

    i32 = jnp.int32
    pos_f = n_meta + jnp.arange(seq_p, dtype=i32)
    pos_m = jnp.concatenate([jnp.zeros((pad,), i32), jnp.arange(n_meta, dtype=i32)])
    ck_f = 1 + jnp.arange(seq_p, dtype=i32) // CHUNK
    cq_m = jnp.concatenate([jnp.full((pad,), -1, i32), jnp.zeros((n_meta,), i32)])
    ck_m = jnp.concatenate([jnp.full((pad,), 2 ** 30, i32), jnp.zeros((n_meta,), i32)])
    pos_s_all = jnp.arange(past + seq_s, dtype=i32)
    ck_s = pos_s_all // CHUNK
    pos_s = pos_s_all[past:]
    cq_rows = jnp.concatenate([jnp.tile(ck_f, nb_p), cq_m, jnp.tile(ck_s[past:], nb_s)])
    cq_rows = jnp.broadcast_to(cq_rows[:, None], (r_tot, LANES))
    groups, reps = (pos_f, pos_m, pos_s), (nb_p, 1, nb_s)
    tabs128 = _rope_tables(groups, reps, 0, 128, 128)
    tabs64 = _rope_tables(groups, reps, 0, 64, 128)
    tabs256 = _rope_tables(groups, reps, 0, 256, 256)

    h = jnp.concatenate([x_prompt.reshape(rf, d_model), jnp.zeros((pad, d_model), F32), meta_tokens.astype(F32),
                         x_sample.reshape(rs, d_model)], axis=0)
    mlp_up, mlp_down = mlp_w_up.astype(BF16), mlp_w_down.astype(BF16)

    def prompt_rows(x):
        meta = jnp.broadcast_to(x[row_m + pad:row_s][None], (nb_p, n_meta, x.shape[1]))
        return jnp.concatenate([meta, x[:rf].reshape(nb_p, seq_p, -1)], axis=1)

    def sample_rows(x):
        return x[row_s:].reshape(nb_s, seq_s, -1)

    ab_p, ab_s, cd_p, cd_s = [], [], [], []
    for layer in range(depth):
        i = layer // 2
        gmix = norm_mix[layer][None, :]
        if layer % 2 == 0:
            w = ab_w_in[i]
            offs = np.cumsum([0, 1024, 256, 256, 1024, 64, 16, 512, 256, 64])
            sec = lambda n: w[:, offs[n]:offs[n + 1]]
            win = jnp.concatenate([sec(0), sec(1), sec(2), sec(3), sec(6), sec(7), sec(4), sec(8), sec(5),
                                   jnp.zeros((d_model, AB_IN_PAD - 3472), F32)], axis=1).astype(BF16)
            wuq = b_w_uq[i].reshape(-1, B_HEADS, B_NOPE + B_ROPE)
            wuq = jnp.concatenate([wuq[:, :, :B_NOPE].reshape(-1, B_HEADS * B_NOPE),
                                   wuq[:, :, B_NOPE:].reshape(-1, B_HEADS * B_ROPE)], axis=1).astype(BF16)
            wukv = b_w_ukv[i].reshape(-1, B_HEADS, B_NOPE + B_V)
            wukv = jnp.concatenate([wukv[:, :, :B_NOPE].reshape(-1, B_HEADS * B_NOPE),
                                    wukv[:, :, B_NOPE:].reshape(-1, B_HEADS * B_V)], axis=1).astype(BF16)
            two = lambda g: jnp.concatenate([g, g])[None, :]
            gkr2 = jnp.concatenate([jnp.ones((64,), F32), b_kr_norm[i]])[None, :]
            (q, kf, vf, kb, vb_a, qi, kikr, kikrb, wi, qn, qr, lat) = _ab_proj(
                h, gmix, win, wuq, a_q_norm[i][None, :], a_k_norm[i][None, :], b_q_lat_norm[i][None, :],
                b_kv_lat_norm[i][None, :], b_qn_norm[i][None, :], two(b_qr_norm[i]), gkr2,
                tabs128 + tabs64, tile)
            gkn = b_kn_norm[i][None, :]
            kn, vbb = _ukv(lat, wukv, gkn, tile)
            pkn, pvb = _ukv(cache_b_latent[i].reshape(nb_s * past, -1), wukv, gkn, 512)
            rows = (qi, wi, q, qn, qr, cq_rows, kikrb, kb, vb_a, kn, vbb)
            bufs = (jnp.zeros((r_tot, 1024), BF16), jnp.zeros((r_tot, 1024), BF16))
            bufs = _attn_prompt(rows, ck_m[None, :], ck_f[None, :], bufs, nb_p, seq_p, row_m, mb, topk_p, 8)
            oa, ob = _attn_sample(rows, cache_a_kidx[i], cache_a_k[i], cache_a_v[i], pkn, cache_b_krope[i], pvb,
                                  ck_s[None, :past], ck_s[None, past:], bufs, row_s, nb_s, seq_s, topk_s)
            h = _out_proj(h, oa, ob, ab_w_out[i].astype(BF16), tile)
            kv4 = lambda x: x.reshape(x.shape[0], x.shape[1], A_KV_HEADS, A_HEAD_DIM)
            ab_p.append((kv4(prompt_rows(kf)), kv4(prompt_rows(vf)), prompt_rows(kikr)[..., :64],
                         prompt_rows(lat), prompt_rows(kikr)[..., 64:]))
            ab_s.append((kv4(sample_rows(kf)), kv4(sample_rows(vf)), sample_rows(kikr)[..., :64],
                         sample_rows(lat), sample_rows(kikr)[..., 64:]))
        else:
            z = _cd_proj(h, gmix, cd_w_in[i].astype(BF16), tile, 2048)
            cwid = C_BLOCKS * C_BLOCK
            nc = CONV_W - 1
            gw = jnp.concatenate([c_gate_a_w[i], c_gate_x_w[i]], axis=-1).astype(BF16)
            cargs = (c_conv_w[i], c_conv_b[i][None, :], gw, c_gate_a_b[i][None, :], c_gate_x_b[i][None, :],
                     c_lambda[i][None, :])
            oc = jnp.zeros((r_tot, cwid), BF16)
            oc, hl_m = _rglru(z, cq_rows, jnp.zeros((1, 8, cwid), F32), jnp.zeros((1, 1, cwid), F32), *cargs, oc,
                              row_m, 1, mb, ROW_BLOCK)
            prev_f = jnp.broadcast_to(z[row_s - 8:row_s, :cwid][None], (nb_p, 8, cwid))
            oc, hl_p = _rglru(z, cq_rows, prev_f, jnp.broadcast_to(hl_m, (nb_p, 1, cwid)), *cargs, oc,
                              0, nb_p, seq_p, ROW_BLOCK)
            prev_s = jnp.concatenate([jnp.zeros((nb_s, 8 - nc, cwid), F32), state_c_conv[i]], axis=1)
            oc, hl_s = _rglru(z, cq_rows, prev_s, state_c_h[i][:, None, :], *cargs, oc, row_s, nb_s, seq_s, seq_s)
            hl_p, hl_s = hl_p[:, 0], hl_s[:, 0]
            log_gamma = jnp.log(1.0 - 2.0 ** (-5.0 - jnp.arange(D_HEADS, dtype=F32)))
            rargs = (d_out_norm[i][None, :], log_gamma)
            od = jnp.zeros((r_tot, D_HEADS * D_VDIM), BF16)
            od, ds_m = _retention(z, *tabs256, jnp.zeros((1, D_HEADS, D_KDIM, D_VDIM), F32), *rargs, od,
                                  2 * cwid, row_m, 1, mb, ROW_BLOCK)
            od, ds_p = _retention(z, *tabs256, jnp.broadcast_to(ds_m, (nb_p, D_HEADS, D_KDIM, D_VDIM)), *rargs, od,
                                  2 * cwid, 0, nb_p, seq_p, ROW_BLOCK)
            od, ds_s = _retention(z, *tabs256, state_d_s[i], *rargs, od, 2 * cwid, row_s, nb_s, seq_s, seq_s)
            h = _out_proj(h, oc, od, cd_w_out[i].astype(BF16), tile)

            def seq_tails(row0, nb, t):
                n = min(nc, t)
                rws = row0 + t - n + (jnp.arange(nb, dtype=i32) * t)[:, None] + jnp.arange(n, dtype=i32)
                return jnp.take(z, rws.reshape(-1), axis=0)[:, :cwid].reshape(nb, n, cwid)

            assert seq_p >= nc
            cd_p.append((seq_tails(0, nb_p, seq_p), hl_p, ds_p))
            cd_s.append((jnp.concatenate([state_c_conv[i], seq_tails(row_s, nb_s, seq_s)], axis=1)[:, -nc:],
                         hl_s, ds_s))
        h = _mlp(h, norm_mlp[layer][None, :], mlp_up, mlp_down, layer, tile, 1024)

    def stack(entries, j):
        return jnp.stack([e[j] for e in entries])

    y_p = h[:rf].reshape(nb_p, seq_p, d_model)
    y_s = h[row_s:].reshape(nb_s, seq_s, d_model)
    return (y_p, y_s,
            stack(ab_p, 0), stack(ab_p, 1), stack(ab_p, 2), stack(ab_p, 3), stack(ab_p, 4),
            stack(cd_p, 0), stack(cd_p, 1), stack(cd_p, 2),
            stack(ab_s, 0), stack(ab_s, 1), stack(ab_s, 2), stack(ab_s, 3), stack(ab_s, 4),
            stack(cd_s, 0), stack(cd_s, 1), stack(cd_s, 2))
```

```python
import functools

import jax
import jax.numpy as jnp
import numpy as np
from jax import lax
from jax.experimental import pallas as pl
from jax.experimental.pallas import tpu as pltpu

F32 = jnp.float32
BF16 = jnp.bfloat16

CHUNK = 64
ROPE_THETA = 10000.0
EPS = 1e-6
A_HEADS, A_KV_HEADS, A_HEAD_DIM = 8, 2, 128
IDX_HEADS, IDX_DIM = 16, 64
TOPK_MAX = 256
B_HEADS, B_NOPE, B_ROPE, B_V = 8, 128, 64, 128
C_BLOCKS, C_BLOCK, CONV_W = 8, 128, 4
RG_C = 8.0
D_HEADS, D_KDIM, D_VDIM = 4, 256, 256

LANES = 128
ROW_BLOCK = 128
NEG = -1e30
INT_MIN = -2 ** 31
VMEM_LIMIT = 56 * 1024 * 1024


def _cparams(*sem):
    return pltpu.CompilerParams(dimension_semantics=sem, vmem_limit_bytes=VMEM_LIMIT)


def _const_spec(shape):
    nd = len(shape)
    return pl.BlockSpec(shape, lambda *_: (0,) * nd)


def _rms(x, g):
    ms = jnp.mean(x * x, axis=-1, keepdims=True)
    return x * lax.rsqrt(ms + EPS) * g


def _rms_half(x, g):
    lane = lax.broadcasted_iota(jnp.int32, x.shape, 1)
    lo = lane < 64
    xx = x * x
    s_lo = jnp.sum(jnp.where(lo, xx, 0.0), axis=-1, keepdims=True)
    s_hi = jnp.sum(jnp.where(lo, 0.0, xx), axis=-1, keepdims=True)
    ms = jnp.where(lo, s_lo, s_hi) * (1.0 / 64.0)
    return x * lax.rsqrt(ms + EPS) * g


def _rope128(x, cos, sin_signed):
    return x * cos + pltpu.roll(x, 64, 1) * sin_signed


def _rope64(x, cos, sin_signed):
    lane = lax.broadcasted_iota(jnp.int32, x.shape, 1)
    first = (lane % 64) < 32
    rot = jnp.where(first, pltpu.roll(x, 96, 1), pltpu.roll(x, 32, 1))
    return x * cos + rot * sin_signed


AB_COLS = dict(qa=(0, 1024), ka=(1024, 1280), va=(1280, 1536), qi=(1536, 2560), cq=(2560, 3072),
               ckv=(3072, 3328), kikr=(3328, 3456), wi=(3456, 3584))
AB_IN_PAD = 3584


def _ab_proj_kernel(h_ref, gmix_ref, win_ref, wuq_ref, gaq_ref, gak_ref, gqlat_ref, gkvlat_ref, gqn_ref,
                    gqr_ref, gkr_ref, c128_ref, s128_ref, c64_ref, s64_ref,
                    q_ref, kf_ref, vf_ref, kb_ref, vb_ref, qi_ref, kikr_ref, kikrb_ref, wi_ref, qn_ref, qr_ref,
                    lat_ref):
    xn = _rms(h_ref[...], gmix_ref[...]).astype(BF16)
    c128, s128, c64, s64 = c128_ref[...], s128_ref[...], c64_ref[...], s64_ref[...]

    def proj(name):
        a, b = AB_COLS[name]
        return jnp.dot(xn, win_ref[:, a:b], preferred_element_type=F32)

    z = proj('qa')
    qscale = A_HEAD_DIM ** -0.5
    for h in range(A_HEADS):
        x = _rope128(_rms(z[:, h * 128:(h + 1) * 128], gaq_ref[...]), c128, s128)
        q_ref[:, h * 128:(h + 1) * 128] = (x * qscale).astype(BF16)
    z = proj('ka')
    for h in range(A_KV_HEADS):
        x = _rope128(_rms(z[:, h * 128:(h + 1) * 128], gak_ref[...]), c128, s128)
        kf_ref[:, h * 128:(h + 1) * 128] = x
        kb_ref[:, h * 128:(h + 1) * 128] = x.astype(BF16)
    z = proj('va')
    vf_ref[...] = z
    vb_ref[...] = z.astype(BF16)
    z = proj('qi')
    for p in range(IDX_HEADS // 2):
        x = _rope64(z[:, p * 128:(p + 1) * 128], c64, s64)
        qi_ref[:, p * 128:(p + 1) * 128] = (x * (IDX_DIM ** -0.5)).astype(BF16)
    z = proj('kikr')
    lane = lax.broadcasted_iota(jnp.int32, z.shape, 1)
    x = jnp.where(lane < 64, z, _rms_half(z, gkr_ref[...]))
    x = _rope64(x, c64, s64)
    kikr_ref[...] = x
    kikrb_ref[...] = x.astype(BF16)
    wi_ref[...] = proj('wi') * (IDX_HEADS ** -0.5)
    lat = _rms(proj('ckv'), gkvlat_ref[...])
    lat_ref[...] = lat
    cq = _rms(proj('cq'), gqlat_ref[...]).astype(BF16)
    bscale = (B_NOPE + B_ROPE) ** -0.5
    zq = jnp.dot(cq, wuq_ref[:, :B_HEADS * B_NOPE], preferred_element_type=F32)
    for h in range(B_HEADS):
        x = _rms(zq[:, h * 128:(h + 1) * 128], gqn_ref[...])
        qn_ref[:, h * 128:(h + 1) * 128] = (x * bscale).astype(BF16)
    zq = jnp.dot(cq, wuq_ref[:, B_HEADS * B_NOPE:], preferred_element_type=F32)
    for p in range(B_HEADS // 2):
        x = _rope64(_rms_half(zq[:, p * 128:(p + 1) * 128], gqr_ref[...]), c64, s64)
        qr_ref[:, p * 128:(p + 1) * 128] = (x * bscale).astype(BF16)


def _ab_proj(h, gmix, win, wuq, gaq, gak, gqlat, gkvlat, gqn, gqr2, gkr2, tabs, tm):
    r, d = h.shape
    row = lambda w: pl.BlockSpec((tm, w), lambda i: (i, 0))
    outs = [(1024, BF16), (256, F32), (256, F32), (256, BF16), (256, BF16), (1024, BF16), (128, F32),
            (128, BF16), (128, F32), (1024, BF16), (512, BF16), (256, F32)]
    return pl.pallas_call(
        _ab_proj_kernel,
        grid=(r // tm,),
        in_specs=[row(d), _const_spec(gmix.shape), _const_spec(win.shape), _const_spec(wuq.shape),
                  _const_spec(gaq.shape), _const_spec(gak.shape), _const_spec(gqlat.shape),
                  _const_spec(gkvlat.shape), _const_spec(gqn.shape), _const_spec(gqr2.shape),
                  _const_spec(gkr2.shape), row(128), row(128), row(128), row(128)],
        out_specs=[row(w) for w, _ in outs],
        out_shape=[jax.ShapeDtypeStruct((r, w), dt) for w, dt in outs],
        compiler_params=_cparams("parallel"),
        name="ab_proj",
    )(h, gmix, win, wuq, gaq, gak, gqlat, gkvlat, gqn, gqr2, gkr2, *tabs)


def _ukv_kernel(lat_ref, w_ref, gkn_ref, kn_ref, vb_ref):
    lat = lat_ref[...].astype(BF16)
    z = jnp.dot(lat, w_ref[:, :B_HEADS * B_NOPE], preferred_element_type=F32)
    for h in range(B_HEADS):
        kn_ref[:, h * 128:(h + 1) * 128] = _rms(z[:, h * 128:(h + 1) * 128], gkn_ref[...]).astype(BF16)
    vb_ref[...] = jnp.dot(lat, w_ref[:, B_HEADS * B_NOPE:], preferred_element_type=F32).astype(BF16)


def _ukv(lat, w, gkn, tm):
    n = lat.shape[0]
    row = lambda wd: pl.BlockSpec((tm, wd), lambda i: (i, 0))
    return pl.pallas_call(
        _ukv_kernel,
        grid=(n // tm,),
        in_specs=[row(lat.shape[1]), _const_spec(w.shape), _const_spec(gkn.shape)],
        out_specs=[row(1024), row(1024)],
        out_shape=[jax.ShapeDtypeStruct((n, 1024), BF16)] * 2,
        compiler_params=_cparams("parallel"),
        name="mla_ukv",
    )(lat, w, gkn)


def _dot_t(a, b):
    return lax.dot_general(a, b, (((1,), (1,)), ((), ())), preferred_element_type=F32)


def _float_key(x):
    bits = pltpu.bitcast(x, jnp.int32)
    return bits ^ ((bits >> 31) & jnp.int32(0x7FFFFFFF))


def _counts_ge16(vals, cands):
    rows = cands[0].shape[0]
    c16 = [jnp.broadcast_to(c, (rows, LANES)).astype(jnp.int16) for c in cands]
    one, zero = jnp.int16(1), jnp.int16(0)
    accs = [jnp.zeros((rows, LANES), jnp.int16) for _ in cands]
    rest = [jnp.zeros((rows, 1), F32) for _ in cands]
    for v in vals:
        n_full = v.shape[1] // LANES * LANES
        for s0 in range(0, n_full, LANES):
            blk = v[:, s0:s0 + LANES]
            accs = [a + jnp.where(blk >= c, one, zero) for a, c in zip(accs, c16)]
        if n_full < v.shape[1]:
            blk = v[:, n_full:]
            w = blk.shape[1]
            rest = [r + jnp.sum(jnp.where(blk >= c[:, :w], one, zero).astype(F32), axis=-1, keepdims=True)
                    for r, c in zip(rest, c16)]
    return [jnp.sum(a.astype(F32), axis=-1, keepdims=True) + r for a, r in zip(accs, rest)]


SEARCH_STEPS = 8
UNROLL_SEARCH_MAX_KEYS = 1152
WIDE_QUERY_MAX_KEYS = 768


def _search16(vals, k, side_jobs):
    def advance(t, step):
        cnts = _counts_ge16(vals, [t + j * step for j in (1, 2, 3)])
        digit = functools.reduce(jnp.add, [jnp.where(c >= float(k), 1, 0) for c in cnts])
        return t + digit * step

    t = jnp.full((vals[0].shape[0], 1), -32768, jnp.int32)
    if sum(v.shape[1] for v in vals) > UNROLL_SEARCH_MAX_KEYS:
        return lax.fori_loop(0, SEARCH_STEPS, lambda i, t: advance(t, jnp.left_shift(jnp.int32(1), 14 - 2 * i)), t)
    for i in range(SEARCH_STEPS):
        t = advance(t, 1 << (14 - 2 * i))
        if i % 2 == 1 and side_jobs:
            side_jobs.pop(0)()
    return t


def _kth_largest(keys, k, side_jobs):
    his = [(key >> 16).astype(jnp.int16) for key in keys]
    los = [((key & 0xFFFF) - 32768).astype(jnp.int16) for key in keys]
    h = _search16(his, k, side_jobs)
    top, bottom = jnp.int16(32767), jnp.int16(-32768)
    los2 = []
    for hi, lo in zip(his, los):
        h16 = jnp.broadcast_to(h, hi.shape).astype(jnp.int16)
        los2.append(jnp.where(hi == h16, lo, jnp.where(hi > h16, top, bottom)))
    l = _search16(los2, k, side_jobs)
    return (h << 16) | ((l + 32768) & 0xFFFF)


def _softmax_pv(ss, vs):
    m = functools.reduce(jnp.maximum, [jnp.max(s, axis=-1, keepdims=True) for s in ss])
    acc = functools.reduce(jnp.add, [
        jnp.dot(jnp.exp(s - m).astype(BF16), jnp.concatenate([v, jnp.ones_like(v)], axis=1),
                preferred_element_type=F32) for s, v in zip(ss, vs)])
    d = vs[0].shape[1]
    return acc[:, :d] / acc[:, d:]


def _kv_head(x, c):
    if callable(x):
        return x(c)
    return x[c] if isinstance(x, tuple) else x[:, c * 128:(c + 1) * 128]


def _tie_break(keys, thr, topk, bias_refs):
    nq = keys[0].shape[0]
    n_gt = functools.reduce(jnp.add, [jnp.sum(jnp.where(key > thr, 1.0, 0.0), axis=-1, keepdims=True)
                                      for key in keys])
    need = float(topk) - n_gt
    r_i = lax.broadcasted_iota(jnp.int32, (LANES, LANES), 0)
    c_i = lax.broadcasted_iota(jnp.int32, (LANES, LANES), 1)
    before = jnp.where(r_i < c_i, 1.0, 0.0).astype(BF16)
    seen = jnp.zeros((nq, 1), F32)
    for key, ref in zip(keys, bias_refs):
        for s0 in range(0, key.shape[1], LANES):
            blk = key[:, s0:s0 + LANES]
            w = blk.shape[1]
            eq = jnp.where(blk == thr, 1.0, 0.0)
            rank = seen + jnp.dot(eq.astype(BF16), before[:w, :w], preferred_element_type=F32)
            sel = jnp.logical_or(blk > thr, jnp.logical_and(eq > 0.0, rank < need))
            ref[:, s0:s0 + w] = jnp.where(jnp.logical_and(sel, blk > INT_MIN), 0.0, NEG)
            seen = seen + jnp.sum(eq, axis=-1, keepdims=True)


def _dsa_core(qi, wi, q, cq, pieces, topk, bias_refs, side_jobs):
    nq = qi.shape[0]
    keys = []
    for ki, _, _, ck in pieces:
        score = jnp.zeros((nq, ki.shape[0]), F32)
        for h in range(IDX_HEADS):
            s_h = _dot_t(qi[:, h * 64:(h + 1) * 64], ki)
            score = score + jnp.maximum(s_h, 0.0) * wi[:, h:h + 1]
        keys.append(jnp.where(ck <= cq, _float_key(score), INT_MIN))
    thr = _kth_largest(keys, topk, side_jobs)
    n_ge = jnp.zeros((nq, 1), F32)
    for key, ref in zip(keys, bias_refs):
        ge = key >= thr
        ref[...] = jnp.where(jnp.logical_and(ge, key > INT_MIN), 0.0, NEG)
        n_ge = n_ge + jnp.sum(jnp.where(ge, 1.0, 0.0), axis=-1, keepdims=True)
    tied = jnp.max(jnp.where(jnp.logical_and(n_ge > float(topk), thr > INT_MIN), 1.0, 0.0))
    pl.when(tied > 0.0)(functools.partial(_tie_break, keys, thr, topk, bias_refs))
    group = A_HEADS // A_KV_HEADS
    biases = [jnp.concatenate([ref[...]] * group, axis=0) for ref in bias_refs]
    outs = []
    for c in range(A_KV_HEADS):
        qg = jnp.concatenate([q[:, (c * group + g) * 128:(c * group + g + 1) * 128] for g in range(group)], axis=0)
        ss = [_dot_t(qg, _kv_head(k, c)) + bias for (_, k, _, _), bias in zip(pieces, biases)]
        o = _softmax_pv(ss, [_kv_head(v, c) for _, _, v, _ in pieces])
        outs.extend([o[g * nq:(g + 1) * nq] for g in range(group)])
    return jnp.concatenate(outs, axis=-1)


def _mla_jobs(qn, qr, cq, pieces, store):
    biases = [jnp.where(ck <= cq, 0.0, NEG) for _, _, _, ck in pieces]

    def head(h):
        ss = []
        for (kn, kr, _, _), bias in zip(pieces, biases):
            s = _dot_t(qn[:, h * B_NOPE:(h + 1) * B_NOPE], kn(h))
            s = s + _dot_t(qr[:, h * B_ROPE:(h + 1) * B_ROPE], kr)
            ss.append(s + bias)
        store(h, _softmax_pv(ss, [vb(h) for _, _, vb, _ in pieces]))

    return [functools.partial(head, h) for h in range(B_HEADS)]


def _attn_prompt_kernel(qi_ref, wi_ref, q_ref, qn_ref, qr_ref, cq_ref, mkikr_ref, mk_ref, mv_ref, mkn_ref, mvb_ref,
                        ckm_ref, *rest, topk, s_len):
    nf = 6 if s_len else 0
    oa_ref, ob_ref = rest[nf + 2:nf + 4]
    bias_refs = list(rest[nf + 4:])
    cq = cq_ref[:, 0:1]
    valid = cq >= 0
    head_of = lambda ref, n: lambda h: ref[:n, h * B_V:(h + 1) * B_V]
    kv_of = lambda ref, n: lambda c: ref[:n, c * A_HEAD_DIM:(c + 1) * A_HEAD_DIM]
    mb = mkikr_ref.shape[0]
    mkikr = mkikr_ref[...]
    ckm = ckm_ref[...]
    pieces_a = [(mkikr[:, :64], kv_of(mk_ref, mb), kv_of(mv_ref, mb), ckm)]
    pieces_b = [(head_of(mkn_ref, mb), mkikr[:, 64:], head_of(mvb_ref, mb), ckm)]
    if s_len:
        fkikr_ref, fk_ref, fv_ref, fkn_ref, fvb_ref, ckf_ref = rest[:nf]
        fkikr = fkikr_ref[:s_len, :]
        ckf = ckf_ref[:, :s_len]
        pieces_a.append((fkikr[:, :64], kv_of(fk_ref, s_len), kv_of(fv_ref, s_len), ckf))
        pieces_b.append((head_of(fkn_ref, s_len), fkikr[:, 64:], head_of(fvb_ref, s_len), ckf))

    def store_b(h, o):
        ob_ref[:, h * B_V:(h + 1) * B_V] = jnp.where(valid, o, 0.0).astype(BF16)

    jobs = _mla_jobs(qn_ref, qr_ref, cq, pieces_b, store_b)
    oa = _dsa_core(qi_ref[...], wi_ref[...], q_ref[...], cq, pieces_a, topk, bias_refs, jobs)
    oa_ref[...] = jnp.where(valid, oa, 0.0).astype(BF16)
    for job in jobs:
        job()


def _attn_prompt(rows, ckm, ckf, bufs, nb, seq, meta_row0, mb, topk, n_ranges):
    qi, wi, q, qn, qr, cq, kikr, k, v, kn, vb = rows
    tq = ROW_BLOCK
    nj = seq // tq
    edges = sorted({-(-nj * r // n_ranges) for r in range(n_ranges + 1)})
    mspec = lambda w: pl.BlockSpec((mb, w), lambda b, j: (meta_row0 // mb, 0))
    fspec = lambda w: pl.BlockSpec((seq, w), lambda b, j: (b, 0))
    key_arrays = (kikr, k, v, kn, vb)
    key_widths = (128, 256, 256, 1024, 1024)
    for j0, j1 in [(None, None)] + list(zip(edges[:-1], edges[1:])):
        if j0 is None:
            grid, s_len = (1, 1), 0
            qspec = lambda w: pl.BlockSpec((tq, w), lambda b, j: (meta_row0 // tq, 0))
            frame_specs, frame_args = [], ()
            scratch = [pltpu.VMEM((tq, mb), F32)]
        else:
            s_len, span = j1 * tq, j1 - j0
            wide = s_len <= WIDE_QUERY_MAX_KEYS and j0 % span == 0 and nj % span == 0
            rq, steps = (tq * span, 1) if wide else (tq, span)
            grid = (nb, steps)
            qspec = lambda w, j0=j0, rq=rq: pl.BlockSpec((rq, w), lambda b, j: ((b * nj + j0) * tq // rq + j, 0))
            frame_specs = [fspec(w) for w in key_widths] + [_const_spec(ckf.shape)]
            frame_args = key_arrays + (ckf,)
            scratch = [pltpu.VMEM((rq, mb), F32), pltpu.VMEM((rq, s_len), F32)]
        n_in = 12 + len(frame_specs)
        bufs = pl.pallas_call(
            functools.partial(_attn_prompt_kernel, topk=topk, s_len=s_len),
            grid=grid,
            in_specs=[qspec(1024), qspec(128), qspec(1024), qspec(1024), qspec(512), qspec(128)]
                     + [mspec(w) for w in key_widths] + [_const_spec(ckm.shape)] + frame_specs
                     + [pl.BlockSpec(memory_space=pl.ANY)] * 2,
            out_specs=[qspec(1024), qspec(1024)],
            out_shape=[jax.ShapeDtypeStruct(bufs[0].shape, BF16)] * 2,
            scratch_shapes=scratch,
            input_output_aliases={n_in: 0, n_in + 1: 1},
            compiler_params=_cparams("parallel", "arbitrary"),
            name="attn_prompt",
        )(qi, wi, q, qn, qr, cq, *key_arrays, ckm, *frame_args, *bufs)
    return bufs


def _attn_sample_kernel(qi_ref, wi_ref, q_ref, qn_ref, qr_ref, cq_ref, kikr_ref, k_ref, v_ref, kn_ref, vb_ref,
                        pki_ref, pk_ref, pv_ref, pkn_ref, pkr_ref, pvb_ref, ckp_ref, ckn_ref, bufa_ref, bufb_ref,
                        oa_ref, ob_ref, biasp_ref, biasn_ref, *, topk):
    cq = cq_ref[:, 0:1]
    ckp, ckn = ckp_ref[...], ckn_ref[...]
    kikr = kikr_ref[...]
    past = pki_ref.shape[1]
    heads = lambda ref: lambda c: ref[pl.ds(c, past, stride=A_KV_HEADS), :].astype(BF16)
    kv_of = lambda ref: lambda c: ref[:, c * A_HEAD_DIM:(c + 1) * A_HEAD_DIM]
    past_a = (pki_ref[0].astype(BF16), heads(pk_ref), heads(pv_ref), ckp)
    new_a = (kikr[:, :64], kv_of(k_ref), kv_of(v_ref), ckn)
    head_of = lambda ref: lambda h: ref[:, h * B_V:(h + 1) * B_V]
    past_b = (head_of(pkn_ref), pkr_ref[0].astype(BF16), head_of(pvb_ref), ckp)
    new_b = (head_of(kn_ref), kikr[:, 64:], head_of(vb_ref), ckn)

    def store_b(h, o):
        ob_ref[:, h * B_V:(h + 1) * B_V] = o.astype(BF16)

    jobs = _mla_jobs(qn_ref, qr_ref, cq, [past_b, new_b], store_b)
    oa_ref[...] = _dsa_core(qi_ref[...], wi_ref[...], q_ref[...], cq, [past_a, new_a], topk,
                            [biasp_ref, biasn_ref], jobs).astype(BF16)
    for job in jobs:
        job()


def _attn_sample(rows, pki, pk, pv, pkn, pkr, pvb, ckp, ckn, bufs, row0, nb, ts, topk):
    qi, wi, q, qn, qr, cq, kikr, k, v, kn, vb = rows
    past = pk.shape[1]
    blk0 = row0 // ts
    nspec = lambda w: pl.BlockSpec((ts, w), lambda b: (blk0 + b, 0))
    pspec = lambda w: pl.BlockSpec((1, past, w), lambda b: (b, 0, 0))
    p2spec = lambda w: pl.BlockSpec((past, w), lambda b: (b, 0))
    kvspec = pl.BlockSpec((None, past * A_KV_HEADS, A_HEAD_DIM), lambda b: (b, 0, 0))
    pk = pk.reshape(nb, past * A_KV_HEADS, A_HEAD_DIM)
    pv = pv.reshape(nb, past * A_KV_HEADS, A_HEAD_DIM)
    return pl.pallas_call(
        functools.partial(_attn_sample_kernel, topk=topk),
        grid=(nb,),
        in_specs=[nspec(1024), nspec(128), nspec(1024), nspec(1024), nspec(512), nspec(128),
                  nspec(128), nspec(256), nspec(256), nspec(1024), nspec(1024),
                  pspec(64), kvspec, kvspec, p2spec(1024), pspec(64), p2spec(1024),
                  _const_spec(ckp.shape), _const_spec(ckn.shape)] + [pl.BlockSpec(memory_space=pl.ANY)] * 2,
        out_specs=[nspec(1024), nspec(1024)],
        out_shape=[jax.ShapeDtypeStruct(bufs[0].shape, BF16)] * 2,
        scratch_shapes=[pltpu.VMEM((ts, past), F32), pltpu.VMEM((ts, ts), F32)],
        input_output_aliases={19: 0, 20: 1},
        compiler_params=_cparams("parallel"),
        name="attn_sample",
    )(qi, wi, q, qn, qr, cq, kikr, k, v, kn, vb, pki, pk, pv, pkn, pkr, pvb, ckp, ckn, *bufs)


def _out_proj_kernel(h_ref, oa_ref, ob_ref, w_ref, o_ref):
    half = oa_ref.shape[1]
    y = jnp.dot(oa_ref[...], w_ref[:half, :], preferred_element_type=F32)
    y = y + jnp.dot(ob_ref[...], w_ref[half:, :], preferred_element_type=F32)
    o_ref[...] = h_ref[...] + y


def _out_proj(h, oa, ob, w, tm):
    r, d = h.shape
    row = lambda wd: pl.BlockSpec((tm, wd), lambda i: (i, 0))
    return pl.pallas_call(
        _out_proj_kernel,
        grid=(r // tm,),
        in_specs=[row(d), row(oa.shape[1]), row(ob.shape[1]), _const_spec(w.shape)],
        out_specs=row(d),
        out_shape=jax.ShapeDtypeStruct((r, d), F32),
        compiler_params=_cparams("parallel"),
        name="out_proj",
    )(h, oa, ob, w)


def _mlp_kernel(h_ref, g_ref, wup_ref, wdn_ref, *rest, n_split, n_end):
    *o_refs, xn_ref, acc_ref = rest
    i, f = pl.program_id(0), pl.program_id(1)

    @pl.when(f == 0)
    def _():
        x = h_ref[...]
        xn_ref[...] = _rms(x, g_ref[...]).astype(BF16)
        acc_ref[...] = x

    u = jnp.dot(xn_ref[...], wup_ref[...], preferred_element_type=F32)
    u = jnp.maximum(u, 0.0)
    acc_ref[...] += jnp.dot((u * u).astype(BF16), wdn_ref[...], preferred_element_type=F32)

    @pl.when(f == pl.num_programs(1) - 1)
    def _():
        if len(o_refs) == 1:
            o_refs[0][...] = acc_ref[...]
        else:
            @pl.when(i < n_split)
            def _():
                o_refs[0][...] = acc_ref[...]

            @pl.when(jnp.logical_and(i >= n_split, i < n_end))
            def _():
                o_refs[1][...] = acc_ref[...]


def _mlp(h, g, wup, wdn, layer, tm, tf, split_rows=None):
    r, d = h.shape
    dff = wup.shape[2]
    n_end = r // tm
    if split_rows is None:
        n_split = 0
        out_specs = [pl.BlockSpec((tm, d), lambda i, f: (i, 0))]
        out_shape = [jax.ShapeDtypeStruct((r, d), F32)]
    else:
        a, b = split_rows
        assert a % tm == 0 and b % tm == 0
        n_split, n_b = a // tm, b // tm
        n_end = n_split + n_b
        out_specs = [pl.BlockSpec((tm, d), lambda i, f: (jnp.minimum(i, n_split - 1), 0)),
                     pl.BlockSpec((tm, d), lambda i, f: (jnp.clip(i - n_split, 0, n_b - 1), 0))]
        out_shape = [jax.ShapeDtypeStruct((a, d), F32), jax.ShapeDtypeStruct((b, d), F32)]
    res = pl.pallas_call(
        functools.partial(_mlp_kernel, n_split=n_split, n_end=n_end),
        grid=(r // tm, dff // tf),
        in_specs=[pl.BlockSpec((tm, d), lambda i, f: (i, 0)), pl.BlockSpec((1, d), lambda i, f: (0, 0)),
                  pl.BlockSpec((None, d, tf), lambda i, f: (layer, 0, f)),
                  pl.BlockSpec((None, tf, d), lambda i, f: (layer, f, 0))],
        out_specs=out_specs,
        out_shape=out_shape,
        scratch_shapes=[pltpu.VMEM((tm, d), BF16), pltpu.VMEM((tm, d), F32)],
        compiler_params=_cparams("parallel" if split_rows is None else "arbitrary", "arbitrary"),
        name="mlp",
    )(h, g, wup, wdn)
    return res[0] if split_rows is None else res


def _cd_proj_kernel(h_ref, g_ref, w_ref, z_ref, xn_ref):
    @pl.when(pl.program_id(1) == 0)
    def _():
        xn_ref[...] = _rms(h_ref[...], g_ref[...]).astype(BF16)

    z_ref[...] = jnp.dot(xn_ref[...], w_ref[...], preferred_element_type=F32)


def _cd_proj(h, g, w, tm, tn):
    r, d = h.shape
    n = w.shape[1]
    return pl.pallas_call(
        _cd_proj_kernel,
        grid=(r // tm, n // tn),
        in_specs=[pl.BlockSpec((tm, d), lambda i, j: (i, 0)), pl.BlockSpec((1, d), lambda i, j: (0, 0)),
                  pl.BlockSpec((d, tn), lambda i, j: (0, j))],
        out_specs=pl.BlockSpec((tm, tn), lambda i, j: (i, j)),
        out_shape=jax.ShapeDtypeStruct((r, n), F32),
        scratch_shapes=[pltpu.VMEM((tm, d), BF16)],
        compiler_params=_cparams("parallel", "arbitrary"),
        name="cd_proj",
    )(h, g, w)


def _rglru_kernel(cx_ref, cg_ref, valid_ref, prev0_ref, h0_ref, cw_ref, cb_ref, gw_ref, gab_ref, gxb_ref, lam_ref,
                  buf_ref, o_ref, hl_ref, prev_s, h_s, a_s, b_s):
    @pl.when(pl.program_id(1) == 0)
    def _():
        prev_s[...] = prev0_ref[0]
        h_s[...] = jnp.broadcast_to(h0_ref[0], h_s.shape)

    cx = cx_ref[...]
    tm = cx.shape[0]
    full = jnp.concatenate([prev_s[...], cx], axis=0)
    prev_s[...] = cx[tm - 8:, :]
    xc = cb_ref[...] + cx * cw_ref[CONV_W - 1:CONV_W, :]
    for j in range(CONV_W - 1):
        sh = CONV_W - 1 - j
        xc = xc + full[8 - sh:8 - sh + tm, :] * cw_ref[j:j + 1, :]
    valid = valid_ref[:, 0:1] >= 0
    softplus = jnp.logaddexp(-lam_ref[...], 0.0)
    for n in range(C_BLOCKS):
        sl = slice(n * C_BLOCK, (n + 1) * C_BLOCK)
        xb = xc[:, sl]
        gates = jnp.dot(xb.astype(BF16), gw_ref[n], preferred_element_type=F32)
        r = jax.nn.sigmoid(gates[:, :C_BLOCK] + gab_ref[:, sl])
        ig = jax.nn.sigmoid(gates[:, C_BLOCK:] + gxb_ref[:, sl])
        log_a = -RG_C * r * softplus[:, sl]
        a = jnp.exp(log_a)
        a_s[:, sl] = a
        b = jnp.sqrt(1.0 - a * a) * (ig * xb)
        b_s[:, sl] = jnp.where(valid, b, 0.0)

    rid = lax.broadcasted_iota(jnp.int32, h_s.shape, 0)

    def group(g, h):
        rows = pl.ds(pl.multiple_of(g * 8, 8), 8)
        ca, cb = a_s[rows, :], b_s[rows, :]
        for s in (1, 2, 4):
            m = rid >= s
            cb = jnp.where(m, ca * pltpu.roll(cb, s, 0) + cb, cb)
            ca = jnp.where(m, ca * pltpu.roll(ca, s, 0), ca)
        hh = ca * h + cb
        b_s[rows, :] = hh
        return jnp.broadcast_to(hh[7:8, :], hh.shape)

    h = lax.fori_loop(0, tm // 8, group, h_s[...])
    h_s[...] = h
    hl_ref[0] = h[0:1, :]
    o_ref[...] = (b_s[...] * jax.nn.gelu(cg_ref[...])).astype(BF16)


def _rglru(z, valid, prev0, h0, cw, cb, gw, gab, gxb, lam, buf, row0, nb, t, tm):
    cwid = C_BLOCKS * C_BLOCK
    nj = t // tm
    b0 = row0 // tm
    zspec = lambda c: pl.BlockSpec((tm, cwid), lambda b, j: (b0 + b * nj + j, c))
    return pl.pallas_call(
        _rglru_kernel,
        grid=(nb, nj),
        in_specs=[zspec(0), zspec(1), pl.BlockSpec((tm, LANES), lambda b, j: (b0 + b * nj + j, 0)),
                  pl.BlockSpec((1, 8, cwid), lambda b, j: (b, 0, 0)), pl.BlockSpec((1, 1, cwid), lambda b, j: (b, 0, 0)),
                  _const_spec(cw.shape), _const_spec(cb.shape), _const_spec(gw.shape), _const_spec(gab.shape),
                  _const_spec(gxb.shape), _const_spec(lam.shape), pl.BlockSpec(memory_space=pl.ANY)],
        out_specs=[pl.BlockSpec((tm, cwid), lambda b, j: (b0 + b * nj + j, 0)),
                   pl.BlockSpec((1, 1, cwid), lambda b, j: (b, 0, 0))],
        out_shape=[jax.ShapeDtypeStruct(buf.shape, BF16), jax.ShapeDtypeStruct((nb, 1, cwid), F32)],
        input_output_aliases={11: 0},
        scratch_shapes=[pltpu.VMEM((8, cwid), F32), pltpu.VMEM((8, cwid), F32), pltpu.VMEM((tm, cwid), F32),
                        pltpu.VMEM((tm, cwid), F32)],
        compiler_params=_cparams("parallel", "arbitrary"),
        name="rglru",
    )(z, z, valid, prev0, h0, cw, cb, gw, gab, gxb, lam, buf)


def _retention_kernel(q_ref, k_ref, v_ref, g_ref, cos_ref, sin_ref, s0_ref, gdn_ref, lg_ref, buf_ref,
                      o_ref, s_ref, *, blk):
    @pl.when(pl.program_id(1) == 0)
    def _():
        s_ref[...] = s0_ref[...]

    cos, sin_signed = cos_ref[...], sin_ref[...]

    def rope(x):
        return x * cos + pltpu.roll(x, D_KDIM // 2, 1) * sin_signed

    ri = lax.broadcasted_iota(jnp.int32, (blk, blk), 0)
    ci = lax.broadcasted_iota(jnp.int32, (blk, blk), 1)
    diff = (ri - ci).astype(F32)
    rows = lax.broadcasted_iota(jnp.int32, (blk, 1), 0).astype(F32)
    for hd in range(D_HEADS):
        lg = lg_ref[hd]
        qsl = slice(hd * D_KDIM, (hd + 1) * D_KDIM)
        vsl = slice(hd * D_VDIM, (hd + 1) * D_VDIM)
        q = rope(q_ref[:, qsl])
        k = rope(k_ref[:, qsl]) * (D_KDIM ** -0.5)
        decay = jnp.where(diff >= 0, jnp.exp(lg * jnp.maximum(diff, 0.0)), 0.0)
        qb, kb, vb = q.astype(BF16), k.astype(BF16), v_ref[:, vsl].astype(BF16)
        att = _dot_t(qb, kb) * decay
        o = jnp.dot(att.astype(BF16), vb, preferred_element_type=F32)
        s_prev = s_ref[0, hd]
        o = o + jnp.dot(qb, s_prev.astype(BF16), preferred_element_type=F32) * jnp.exp(lg * (rows + 1.0))
        kdec = (k * jnp.exp(lg * (blk - 1.0 - rows))).astype(BF16)
        s_ref[0, hd] = jnp.exp(lg * blk) * s_prev + lax.dot_general(kdec, vb, (((0,), (0,)), ((), ())),
                                                                    preferred_element_type=F32)
        od = _rms(o, gdn_ref[...]) * jax.nn.silu(g_ref[:, vsl])
        o_ref[:, vsl] = od.astype(BF16)


def _retention(z, cos, sin_signed, s0, gdn, log_gamma, buf, col0, row0, nb, t, blk):
    nj = t // blk
    b0 = row0 // blk
    wid = D_HEADS * D_KDIM
    c0 = col0 // wid
    zspec = lambda c: pl.BlockSpec((blk, wid), lambda b, j: (b0 + b * nj + j, c0 + c))
    tspec = pl.BlockSpec((blk, D_KDIM), lambda b, j: (b0 + b * nj + j, 0))
    sspec = pl.BlockSpec((1, D_HEADS, D_KDIM, D_VDIM), lambda b, j: (b, 0, 0, 0))
    return pl.pallas_call(
        functools.partial(_retention_kernel, blk=blk),
        grid=(nb, nj),
        in_specs=[zspec(0), zspec(1), zspec(2), zspec(3), tspec, tspec, sspec, _const_spec(gdn.shape),
                  pl.BlockSpec(memory_space=pltpu.SMEM), pl.BlockSpec(memory_space=pl.ANY)],
        out_specs=[pl.BlockSpec((blk, wid), lambda b, j: (b0 + b * nj + j, 0)), sspec],
        out_shape=[jax.ShapeDtypeStruct(buf.shape, BF16), jax.ShapeDtypeStruct(s0.shape, F32)],
        input_output_aliases={9: 0},
        compiler_params=_cparams("parallel", "arbitrary"),
        name="retention",
    )(z, z, z, z, cos, sin_signed, s0, gdn, log_gamma, buf)


def _rope_tables(pos_groups, reps_groups, tail, d, width):
    inv = ROPE_THETA ** (-jnp.arange(0, d, 2, dtype=F32) / d)
    cos_rows, sin_rows = [], []
    for pos, reps in zip(pos_groups, reps_groups):
        ang = pos.astype(F32)[:, None] * inv[None, :]
        cos, sin = lax.optimization_barrier((jnp.cos(ang), jnp.sin(ang)))
        cos_rows.append(jnp.tile(jnp.tile(jnp.concatenate([cos, cos], axis=-1), (1, width // d)), (reps, 1)))
        sin_rows.append(jnp.tile(jnp.tile(jnp.concatenate([-sin, sin], axis=-1), (1, width // d)), (reps, 1)))
    cos_rows.append(jnp.ones((tail, width), F32))
    sin_rows.append(jnp.zeros((tail, width), F32))
    return jnp.concatenate(cos_rows, axis=0), jnp.concatenate(sin_rows, axis=0)


def kernel(x_prompt, x_sample, cache_a_k, cache_a_v, cache_a_kidx, cache_b_latent, cache_b_krope, state_c_conv,
           state_c_h, state_d_s, meta_tokens, norm_mix, norm_mlp, ab_w_in, ab_w_out, a_q_norm, a_k_norm,
           b_q_lat_norm, b_w_uq, b_kv_lat_norm, b_w_ukv, b_qn_norm, b_qr_norm, b_kn_norm, b_kr_norm, cd_w_in,
           cd_w_out, c_conv_w, c_conv_b, c_gate_a_w, c_gate_a_b, c_gate_x_w, c_gate_x_b, c_lambda, d_out_norm,
           mlp_w_up, mlp_w_down):
    nb_p, seq_p, d_model = x_prompt.shape
    nb_s, seq_s, _ = x_sample.shape
    past = cache_a_k.shape[2]
    n_meta = meta_tokens.shape[0]
    depth = norm_mix.shape[0]
    mb = -(-n_meta // ROW_BLOCK) * ROW_BLOCK
    pad = mb - n_meta
    rf = nb_p * seq_p
    rs = nb_s * seq_s
    row_m, row_s = rf, rf + mb
    r_tot = rf + mb + rs
    assert seq_p % ROW_BLOCK == 0 and ROW_BLOCK % CHUNK == 0 and seq_s % 8 == 0 and past % 8 == 0
    assert row_s % seq_s == 0 and r_tot % ROW_BLOCK == 0
    blocks = r_tot // ROW_BLOCK
    tile = ROW_BLOCK * max(d for d in range(1, 9) if blocks % d == 0)
    topk_p = min(TOPK_MAX, seq_p // 4)
    topk_s = min(TOPK_MAX, (past + seq_s) // 4)

    i32 = jnp.int32
    pos_f = n_meta + jnp.arange(seq_p, dtype=i32)
    pos_m = jnp.concatenate([jnp.zeros((pad,), i32), jnp.arange(n_meta, dtype=i32)])
    ck_f = 1 + jnp.arange(seq_p, dtype=i32) // CHUNK
    cq_m = jnp.concatenate([jnp.full((pad,), -1, i32), jnp.zeros((n_meta,), i32)])
    ck_m = jnp.concatenate([jnp.full((pad,), 2 ** 30, i32), jnp.zeros((n_meta,), i32)])
    pos_s_all = jnp.arange(past + seq_s, dtype=i32)
    ck_s = pos_s_all // CHUNK
    pos_s = pos_s_all[past:]
    cq_rows = jnp.concatenate([jnp.tile(ck_f, nb_p), cq_m, jnp.tile(ck_s[past:], nb_s)])
    cq_rows = jnp.broadcast_to(cq_rows[:, None], (r_tot, LANES))
    groups, reps = (pos_f, pos_m, pos_s), (nb_p, 1, nb_s)
    tabs128 = _rope_tables(groups, reps, 0, 128, 128)
    tabs64 = _rope_tables(groups, reps, 0, 64, 128)
    tabs256 = _rope_tables(groups, reps, 0, 256, 256)

    h = jnp.concatenate([x_prompt.reshape(rf, d_model), jnp.zeros((pad, d_model), F32), meta_tokens.astype(F32),
                         x_sample.reshape(rs, d_model)], axis=0)
    mlp_up, mlp_down = mlp_w_up.astype(BF16), mlp_w_down.astype(BF16)

    def prompt_rows(x):
        meta = jnp.broadcast_to(x[row_m + pad:row_s][None], (nb_p, n_meta, x.shape[1]))
        return jnp.concatenate([meta, x[:rf].reshape(nb_p, seq_p, -1)], axis=1)

    def sample_rows(x):
        return x[row_s:].reshape(nb_s, seq_s, -1)

    ab_p, ab_s, cd_p, cd_s = [], [], [], []
    for layer in range(depth):
        i = layer // 2
        gmix = norm_mix[layer][None, :]
        if layer % 2 == 0:
            w = ab_w_in[i]
            offs = np.cumsum([0, 1024, 256, 256, 1024, 64, 16, 512, 256, 64])
            sec = lambda n: w[:, offs[n]:offs[n + 1]]
            win = jnp.concatenate([sec(0), sec(1), sec(2), sec(3), sec(6), sec(7), sec(4), sec(8), sec(5),
                                   jnp.zeros((d_model, AB_IN_PAD - 3472), F32)], axis=1).astype(BF16)
            wuq = b_w_uq[i].reshape(-1, B_HEADS, B_NOPE + B_ROPE)
            wuq = jnp.concatenate([wuq[:, :, :B_NOPE].reshape(-1, B_HEADS * B_NOPE),
                                   wuq[:, :, B_NOPE:].reshape(-1, B_HEADS * B_ROPE)], axis=1).astype(BF16)
            wukv = b_w_ukv[i].reshape(-1, B_HEADS, B_NOPE + B_V)
            wukv = jnp.concatenate([wukv[:, :, :B_NOPE].reshape(-1, B_HEADS * B_NOPE),
                                    wukv[:, :, B_NOPE:].reshape(-1, B_HEADS * B_V)], axis=1).astype(BF16)
            two = lambda g: jnp.concatenate([g, g])[None, :]
            gkr2 = jnp.concatenate([jnp.ones((64,), F32), b_kr_norm[i]])[None, :]
            (q, kf, vf, kb, vb_a, qi, kikr, kikrb, wi, qn, qr, lat) = _ab_proj(
                h, gmix, win, wuq, a_q_norm[i][None, :], a_k_norm[i][None, :], b_q_lat_norm[i][None, :],
                b_kv_lat_norm[i][None, :], b_qn_norm[i][None, :], two(b_qr_norm[i]), gkr2,
                tabs128 + tabs64, tile)
            gkn = b_kn_norm[i][None, :]
            kn, vbb = _ukv(lat, wukv, gkn, tile)
            pkn, pvb = _ukv(cache_b_latent[i].reshape(nb_s * past, -1), wukv, gkn, 512)
            rows = (qi, wi, q, qn, qr, cq_rows, kikrb, kb, vb_a, kn, vbb)
            bufs = (jnp.zeros((r_tot, 1024), BF16), jnp.zeros((r_tot, 1024), BF16))
            bufs = _attn_prompt(rows, ck_m[None, :], ck_f[None, :], bufs, nb_p, seq_p, row_m, mb, topk_p, 8)
            oa, ob = _attn_sample(rows, cache_a_kidx[i], cache_a_k[i], cache_a_v[i], pkn, cache_b_krope[i], pvb,
                                  ck_s[None, :past], ck_s[None, past:], bufs, row_s, nb_s, seq_s, topk_s)
            h = _out_proj(h, oa, ob, ab_w_out[i].astype(BF16), tile)
            kv4 = lambda x: x.reshape(x.shape[0], x.shape[1], A_KV_HEADS, A_HEAD_DIM)
            ab_p.append((kv4(prompt_rows(kf)), kv4(prompt_rows(vf)), prompt_rows(kikr)[..., :64],
                         prompt_rows(lat), prompt_rows(kikr)[..., 64:]))
            ab_s.append((kv4(sample_rows(kf)), kv4(sample_rows(vf)), sample_rows(kikr)[..., :64],
                         sample_rows(lat), sample_rows(kikr)[..., 64:]))
        else:
            tile_cd = max(t for t in range(16, 1025, 16) if r_tot % t == 0)
            z = _cd_proj(h, gmix, cd_w_in[i].astype(BF16), tile_cd, 2048)
            cwid = C_BLOCKS * C_BLOCK
            nc = CONV_W - 1
            gw = jnp.concatenate([c_gate_a_w[i], c_gate_x_w[i]], axis=-1).astype(BF16)
            cargs = (c_conv_w[i], c_conv_b[i][None, :], gw, c_gate_a_b[i][None, :], c_gate_x_b[i][None, :],
                     c_lambda[i][None, :])
            oc = jnp.zeros((r_tot, cwid), BF16)
            oc, hl_m = _rglru(z, cq_rows, jnp.zeros((1, 8, cwid), F32), jnp.zeros((1, 1, cwid), F32), *cargs, oc,
                              row_m, 1, mb, ROW_BLOCK)
            prev_f = jnp.broadcast_to(z[row_s - 8:row_s, :cwid][None], (nb_p, 8, cwid))
            oc, hl_p = _rglru(z, cq_rows, prev_f, jnp.broadcast_to(hl_m, (nb_p, 1, cwid)), *cargs, oc,
                              0, nb_p, seq_p, ROW_BLOCK)
            prev_s = jnp.concatenate([jnp.zeros((nb_s, 8 - nc, cwid), F32), state_c_conv[i]], axis=1)
            oc, hl_s = _rglru(z, cq_rows, prev_s, state_c_h[i][:, None, :], *cargs, oc, row_s, nb_s, seq_s, seq_s)
            hl_p, hl_s = hl_p[:, 0], hl_s[:, 0]
            log_gamma = jnp.log(1.0 - 2.0 ** (-5.0 - jnp.arange(D_HEADS, dtype=F32)))
            rargs = (d_out_norm[i][None, :], log_gamma)
            od = jnp.zeros((r_tot, D_HEADS * D_VDIM), BF16)
            od, ds_m = _retention(z, *tabs256, jnp.zeros((1, D_HEADS, D_KDIM, D_VDIM), F32), *rargs, od,
                                  2 * cwid, row_m, 1, mb, ROW_BLOCK)
            od, ds_p = _retention(z, *tabs256, jnp.broadcast_to(ds_m, (nb_p, D_HEADS, D_KDIM, D_VDIM)), *rargs, od,
                                  2 * cwid, 0, nb_p, seq_p, ROW_BLOCK)
            od, ds_s = _retention(z, *tabs256, state_d_s[i], *rargs, od, 2 * cwid, row_s, nb_s, seq_s, seq_s)
            h = _out_proj(h, oc, od, cd_w_out[i].astype(BF16), tile)

            def seq_tails(row0, nb, t):
                n = min(nc, t)
                rws = row0 + t - n + (jnp.arange(nb, dtype=i32) * t)[:, None] + jnp.arange(n, dtype=i32)
                return jnp.take(z, rws.reshape(-1), axis=0)[:, :cwid].reshape(nb, n, cwid)

            assert seq_p >= nc
            cd_p.append((seq_tails(0, nb_p, seq_p), hl_p, ds_p))
            cd_s.append((jnp.concatenate([state_c_conv[i], seq_tails(row_s, nb_s, seq_s)], axis=1)[:, -nc:],
                         hl_s, ds_s))
        h = _mlp(h, norm_mlp[layer][None, :], mlp_up, mlp_down, layer, tile, 1024)

    def stack(entries, j):
        return jnp.stack([e[j] for e in entries])

    y_p = h[:rf].reshape(nb_p, seq_p, d_model)
    y_s = h[row_s:].reshape(nb_s, seq_s, d_model)
    return (y_p, y_s,
            stack(ab_p, 0), stack(ab_p, 1), stack(ab_p, 2), stack(ab_p, 3), stack(ab_p, 4),
            stack(cd_p, 0), stack(cd_p, 1), stack(cd_p, 2),
            stack(ab_s, 0), stack(ab_s, 1), stack(ab_s, 2), stack(ab_s, 3), stack(ab_s, 4),
            stack(cd_s, 0), stack(cd_s, 1), stack(cd_s, 2))
```

```python
import functools

import jax
import jax.numpy as jnp
import numpy as np
from jax import lax
from jax.experimental import pallas as pl
from jax.experimental.pallas import tpu as pltpu

F32 = jnp.float32
BF16 = jnp.bfloat16

CHUNK = 64
ROPE_THETA = 10000.0
EPS = 1e-6
A_HEADS, A_KV_HEADS, A_HEAD_DIM = 8, 2, 128
IDX_HEADS, IDX_DIM = 16, 64
TOPK_MAX = 256
B_HEADS, B_NOPE, B_ROPE, B_V = 8, 128, 64, 128
C_BLOCKS, C_BLOCK, CONV_W = 8, 128, 4
RG_C = 8.0
D_HEADS, D_KDIM, D_VDIM = 4, 256, 256

LANES = 128
ROW_BLOCK = 128
NEG = -1e30
INT_MIN = -2 ** 31
VMEM_LIMIT = 56 * 1024 * 1024


def _cparams(*sem):
    return pltpu.CompilerParams(dimension_semantics=sem, vmem_limit_bytes=VMEM_LIMIT)


def _const_spec(shape):
    nd = len(shape)
    return pl.BlockSpec(shape, lambda *_: (0,) * nd)


def _rms(x, g):
    ms = jnp.mean(x * x, axis=-1, keepdims=True)
    return x * lax.rsqrt(ms + EPS) * g


def _rms_half(x, g):
    lane = lax.broadcasted_iota(jnp.int32, x.shape, 1)
    lo = lane < 64
    xx = x * x
    s_lo = jnp.sum(jnp.where(lo, xx, 0.0), axis=-1, keepdims=True)
    s_hi = jnp.sum(jnp.where(lo, 0.0, xx), axis=-1, keepdims=True)
    ms = jnp.where(lo, s_lo, s_hi) * (1.0 / 64.0)
    return x * lax.rsqrt(ms + EPS) * g


def _rope128(x, cos, sin_signed):
    return x * cos + pltpu.roll(x, 64, 1) * sin_signed


def _rope64(x, cos, sin_signed):
    lane = lax.broadcasted_iota(jnp.int32, x.shape, 1)
    first = (lane % 64) < 32
    rot = jnp.where(first, pltpu.roll(x, 96, 1), pltpu.roll(x, 32, 1))
    return x * cos + rot * sin_signed


AB_COLS = dict(qa=(0, 1024), ka=(1024, 1280), va=(1280, 1536), qi=(1536, 2560), cq=(2560, 3072),
               ckv=(3072, 3328), kikr=(3328, 3456), wi=(3456, 3584))
AB_IN_PAD = 3584


def _ab_proj_kernel(h_ref, gmix_ref, win_ref, wuq_ref, gaq_ref, gak_ref, gqlat_ref, gkvlat_ref, gqn_ref,
                    gqr_ref, gkr_ref, c128_ref, s128_ref, c64_ref, s64_ref,
                    q_ref, kf_ref, vf_ref, kb_ref, vb_ref, qi_ref, kikr_ref, kikrb_ref, wi_ref, qn_ref, qr_ref,
                    lat_ref):
    xn = _rms(h_ref[...], gmix_ref[...]).astype(BF16)
    c128, s128, c64, s64 = c128_ref[...], s128_ref[...], c64_ref[...], s64_ref[...]

    def proj(name):
        a, b = AB_COLS[name]
        return jnp.dot(xn, win_ref[:, a:b], preferred_element_type=F32)

    z = proj('qa')
    qscale = A_HEAD_DIM ** -0.5
    for h in range(A_HEADS):
        x = _rope128(_rms(z[:, h * 128:(h + 1) * 128], gaq_ref[...]), c128, s128)
        q_ref[:, h * 128:(h + 1) * 128] = (x * qscale).astype(BF16)
    z = proj('ka')
    for h in range(A_KV_HEADS):
        x = _rope128(_rms(z[:, h * 128:(h + 1) * 128], gak_ref[...]), c128, s128)
        kf_ref[:, h * 128:(h + 1) * 128] = x
        kb_ref[:, h * 128:(h + 1) * 128] = x.astype(BF16)
    z = proj('va')
    vf_ref[...] = z
    vb_ref[...] = z.astype(BF16)
    z = proj('qi')
    for p in range(IDX_HEADS // 2):
        x = _rope64(z[:, p * 128:(p + 1) * 128], c64, s64)
        qi_ref[:, p * 128:(p + 1) * 128] = (x * (IDX_DIM ** -0.5)).astype(BF16)
    z = proj('kikr')
    lane = lax.broadcasted_iota(jnp.int32, z.shape, 1)
    x = jnp.where(lane < 64, z, _rms_half(z, gkr_ref[...]))
    x = _rope64(x, c64, s64)
    kikr_ref[...] = x
    kikrb_ref[...] = x.astype(BF16)
    wi_ref[...] = proj('wi') * (IDX_HEADS ** -0.5)
    lat = _rms(proj('ckv'), gkvlat_ref[...])
    lat_ref[...] = lat
    cq = _rms(proj('cq'), gqlat_ref[...]).astype(BF16)
    bscale = (B_NOPE + B_ROPE) ** -0.5
    zq = jnp.dot(cq, wuq_ref[:, :B_HEADS * B_NOPE], preferred_element_type=F32)
    for h in range(B_HEADS):
        x = _rms(zq[:, h * 128:(h + 1) * 128], gqn_ref[...])
        qn_ref[:, h * 128:(h + 1) * 128] = (x * bscale).astype(BF16)
    zq = jnp.dot(cq, wuq_ref[:, B_HEADS * B_NOPE:], preferred_element_type=F32)
    for p in range(B_HEADS // 2):
        x = _rope64(_rms_half(zq[:, p * 128:(p + 1) * 128], gqr_ref[...]), c64, s64)
        qr_ref[:, p * 128:(p + 1) * 128] = (x * bscale).astype(BF16)


def _ab_proj(h, gmix, win, wuq, gaq, gak, gqlat, gkvlat, gqn, gqr2, gkr2, tabs, tm):
    r, d = h.shape
    row = lambda w: pl.BlockSpec((tm, w), lambda i: (i, 0))
    outs = [(1024, BF16), (256, F32), (256, F32), (256, BF16), (256, BF16), (1024, BF16), (128, F32),
            (128, BF16), (128, F32), (1024, BF16), (512, BF16), (256, F32)]
    return pl.pallas_call(
        _ab_proj_kernel,
        grid=(r // tm,),
        in_specs=[row(d), _const_spec(gmix.shape), _const_spec(win.shape), _const_spec(wuq.shape),
                  _const_spec(gaq.shape), _const_spec(gak.shape), _const_spec(gqlat.shape),
                  _const_spec(gkvlat.shape), _const_spec(gqn.shape), _const_spec(gqr2.shape),
                  _const_spec(gkr2.shape), row(128), row(128), row(128), row(128)],
        out_specs=[row(w) for w, _ in outs],
        out_shape=[jax.ShapeDtypeStruct((r, w), dt) for w, dt in outs],
        compiler_params=_cparams("parallel"),
        name="ab_proj",
    )(h, gmix, win, wuq, gaq, gak, gqlat, gkvlat, gqn, gqr2, gkr2, *tabs)


def _ukv_kernel(lat_ref, w_ref, gkn_ref, kn_ref, vb_ref):
    lat = lat_ref[...].astype(BF16)
    z = jnp.dot(lat, w_ref[:, :B_HEADS * B_NOPE], preferred_element_type=F32)
    for h in range(B_HEADS):
        kn_ref[:, h * 128:(h + 1) * 128] = _rms(z[:, h * 128:(h + 1) * 128], gkn_ref[...]).astype(BF16)
    vb_ref[...] = jnp.dot(lat, w_ref[:, B_HEADS * B_NOPE:], preferred_element_type=F32).astype(BF16)


def _ukv(lat, w, gkn, tm):
    n = lat.shape[0]
    row = lambda wd: pl.BlockSpec((tm, wd), lambda i: (i, 0))
    return pl.pallas_call(
        _ukv_kernel,
        grid=(n // tm,),
        in_specs=[row(lat.shape[1]), _const_spec(w.shape), _const_spec(gkn.shape)],
        out_specs=[row(1024), row(1024)],
        out_shape=[jax.ShapeDtypeStruct((n, 1024), BF16)] * 2,
        compiler_params=_cparams("parallel"),
        name="mla_ukv",
    )(lat, w, gkn)


def _dot_t(a, b):
    return lax.dot_general(a, b, (((1,), (1,)), ((), ())), preferred_element_type=F32)


def _float_key(x):
    bits = pltpu.bitcast(x, jnp.int32)
    return bits ^ ((bits >> 31) & jnp.int32(0x7FFFFFFF))


def _counts_ge16(vals, cands):
    rows = cands[0].shape[0]
    c16 = [jnp.broadcast_to(c, (rows, LANES)).astype(jnp.int16) for c in cands]
    one, zero = jnp.int16(1), jnp.int16(0)
    accs = [jnp.zeros((rows, LANES), jnp.int16) for _ in cands]
    rest = [jnp.zeros((rows, 1), F32) for _ in cands]
    for v in vals:
        n_full = v.shape[1] // LANES * LANES
        for s0 in range(0, n_full, LANES):
            blk = v[:, s0:s0 + LANES]
            accs = [a + jnp.where(blk >= c, one, zero) for a, c in zip(accs, c16)]
        if n_full < v.shape[1]:
            blk = v[:, n_full:]
            w = blk.shape[1]
            rest = [r + jnp.sum(jnp.where(blk >= c[:, :w], one, zero).astype(F32), axis=-1, keepdims=True)
                    for r, c in zip(rest, c16)]
    return [jnp.sum(a.astype(F32), axis=-1, keepdims=True) + r for a, r in zip(accs, rest)]


SEARCH_STEPS = 8
UNROLL_SEARCH_MAX_KEYS = 1152
WIDE_QUERY_MAX_KEYS = 768


def _search16(vals, k, side_jobs):
    def advance(t, step):
        cnts = _counts_ge16(vals, [t + j * step for j in (1, 2, 3)])
        digit = functools.reduce(jnp.add, [jnp.where(c >= float(k), 1, 0) for c in cnts])
        return t + digit * step

    t = jnp.full((vals[0].shape[0], 1), -32768, jnp.int32)
    if sum(v.shape[1] for v in vals) > UNROLL_SEARCH_MAX_KEYS:
        return lax.fori_loop(0, SEARCH_STEPS, lambda i, t: advance(t, jnp.left_shift(jnp.int32(1), 14 - 2 * i)), t)
    for i in range(SEARCH_STEPS):
        t = advance(t, 1 << (14 - 2 * i))
        if i % 2 == 1 and side_jobs:
            side_jobs.pop(0)()
    return t


def _kth_largest(keys, k, side_jobs):
    his = [(key >> 16).astype(jnp.int16) for key in keys]
    los = [((key & 0xFFFF) - 32768).astype(jnp.int16) for key in keys]
    h = _search16(his, k, side_jobs)
    top, bottom = jnp.int16(32767), jnp.int16(-32768)
    los2 = []
    for hi, lo in zip(his, los):
        h16 = jnp.broadcast_to(h, hi.shape).astype(jnp.int16)
        los2.append(jnp.where(hi == h16, lo, jnp.where(hi > h16, top, bottom)))
    l = _search16(los2, k, side_jobs)
    return (h << 16) | ((l + 32768) & 0xFFFF)


def _softmax_pv(ss, vs):
    m = functools.reduce(jnp.maximum, [jnp.max(s, axis=-1, keepdims=True) for s in ss])
    acc = functools.reduce(jnp.add, [
        jnp.dot(jnp.exp(s - m).astype(BF16), jnp.concatenate([v, jnp.ones_like(v)], axis=1),
                preferred_element_type=F32) for s, v in zip(ss, vs)])
    d = vs[0].shape[1]
    return acc[:, :d] / acc[:, d:]


def _kv_head(x, c):
    return x(c) if callable(x) else x[:, c * A_HEAD_DIM:(c + 1) * A_HEAD_DIM]


def _tie_break(keys, thr, topk, bias_refs):
    nq = keys[0].shape[0]
    n_gt = functools.reduce(jnp.add, [jnp.sum(jnp.where(key > thr, 1.0, 0.0), axis=-1, keepdims=True)
                                      for key in keys])
    need = float(topk) - n_gt
    r_i = lax.broadcasted_iota(jnp.int32, (LANES, LANES), 0)
    c_i = lax.broadcasted_iota(jnp.int32, (LANES, LANES), 1)
    before = jnp.where(r_i < c_i, 1.0, 0.0).astype(BF16)
    seen = jnp.zeros((nq, 1), F32)
    for key, ref in zip(keys, bias_refs):
        for s0 in range(0, key.shape[1], LANES):
            blk = key[:, s0:s0 + LANES]
            w = blk.shape[1]
            eq = jnp.where(blk == thr, 1.0, 0.0)
            rank = seen + jnp.dot(eq.astype(BF16), before[:w, :w], preferred_element_type=F32)
            sel = jnp.logical_or(blk > thr, jnp.logical_and(eq > 0.0, rank < need))
            ref[:, s0:s0 + w] = jnp.where(jnp.logical_and(sel, blk > INT_MIN), 0.0, NEG)
            seen = seen + jnp.sum(eq, axis=-1, keepdims=True)


def _dsa_core(qi, wi, q, cq, pieces, topk, bias_refs, side_jobs):
    nq = qi.shape[0]
    keys = []
    for ki, _, _, ck in pieces:
        score = jnp.zeros((nq, ki.shape[0]), F32)
        for h in range(IDX_HEADS):
            s_h = _dot_t(qi[:, h * 64:(h + 1) * 64], ki)
            score = score + jnp.maximum(s_h, 0.0) * wi[:, h:h + 1]
        keys.append(jnp.where(ck <= cq, _float_key(score), INT_MIN))
    thr = _kth_largest(keys, topk, side_jobs)
    n_ge = jnp.zeros((nq, 1), F32)
    for key, ref in zip(keys, bias_refs):
        ge = key >= thr
        ref[...] = jnp.where(jnp.logical_and(ge, key > INT_MIN), 0.0, NEG)
        n_ge = n_ge + jnp.sum(jnp.where(ge, 1.0, 0.0), axis=-1, keepdims=True)
    tied = jnp.max(jnp.where(jnp.logical_and(n_ge > float(topk), thr > INT_MIN), 1.0, 0.0))
    pl.when(tied > 0.0)(functools.partial(_tie_break, keys, thr, topk, bias_refs))
    group = A_HEADS // A_KV_HEADS
    biases = [jnp.concatenate([ref[...]] * group, axis=0) for ref in bias_refs]
    outs = []
    for c in range(A_KV_HEADS):
        qg = jnp.concatenate([q[:, (c * group + g) * 128:(c * group + g + 1) * 128] for g in range(group)], axis=0)
        ss = [_dot_t(qg, _kv_head(k, c)) + bias for (_, k, _, _), bias in zip(pieces, biases)]
        o = _softmax_pv(ss, [_kv_head(v, c) for _, _, v, _ in pieces])
        outs.extend([o[g * nq:(g + 1) * nq] for g in range(group)])
    return jnp.concatenate(outs, axis=-1)


def _mla_jobs(qn, qr, cq, pieces, store):
    biases = [jnp.where(ck <= cq, 0.0, NEG) for _, _, _, ck in pieces]

    def head(h):
        ss = []
        for (kn, kr, _, _), bias in zip(pieces, biases):
            s = _dot_t(qn[:, h * B_NOPE:(h + 1) * B_NOPE], kn(h))
            s = s + _dot_t(qr[:, h * B_ROPE:(h + 1) * B_ROPE], kr)
            ss.append(s + bias)
        store(h, _softmax_pv(ss, [vb(h) for _, _, vb, _ in pieces]))

    return [functools.partial(head, h) for h in range(B_HEADS)]


def _attn_prompt_kernel(qi_ref, wi_ref, q_ref, qn_ref, qr_ref, cq_ref, mkikr_ref, mk_ref, mv_ref, mkn_ref, mvb_ref,
                        ckm_ref, *rest, topk, s_len):
    nf = 6 if s_len else 0
    oa_ref, ob_ref = rest[nf + 2:nf + 4]
    bias_refs = list(rest[nf + 4:])
    cq = cq_ref[:, 0:1]
    valid = cq >= 0
    head_of = lambda ref, n: lambda h: ref[:n, h * B_V:(h + 1) * B_V]
    kv_of = lambda ref, n: lambda c: ref[:n, c * A_HEAD_DIM:(c + 1) * A_HEAD_DIM]
    mb = mkikr_ref.shape[0]
    mkikr = mkikr_ref[...]
    ckm = ckm_ref[...]
    pieces_a = [(mkikr[:, :64], kv_of(mk_ref, mb), kv_of(mv_ref, mb), ckm)]
    pieces_b = [(head_of(mkn_ref, mb), mkikr[:, 64:], head_of(mvb_ref, mb), ckm)]
    if s_len:
        fkikr_ref, fk_ref, fv_ref, fkn_ref, fvb_ref, ckf_ref = rest[:nf]
        fkikr = fkikr_ref[:s_len, :]
        ckf = ckf_ref[:, :s_len]
        pieces_a.append((fkikr[:, :64], kv_of(fk_ref, s_len), kv_of(fv_ref, s_len), ckf))
        pieces_b.append((head_of(fkn_ref, s_len), fkikr[:, 64:], head_of(fvb_ref, s_len), ckf))

    def store_b(h, o):
        ob_ref[:, h * B_V:(h + 1) * B_V] = jnp.where(valid, o, 0.0).astype(BF16)

    jobs = _mla_jobs(qn_ref, qr_ref, cq, pieces_b, store_b)
    oa = _dsa_core(qi_ref[...], wi_ref[...], q_ref[...], cq, pieces_a, topk, bias_refs, jobs)
    oa_ref[...] = jnp.where(valid, oa, 0.0).astype(BF16)
    for job in jobs:
        job()


def _attn_prompt(rows, ckm, ckf, bufs, nb, seq, meta_row0, mb, topk, n_ranges):
    qi, wi, q, qn, qr, cq, kikr, k, v, kn, vb = rows
    tq = ROW_BLOCK
    nj = seq // tq
    edges = sorted({-(-nj * r // n_ranges) for r in range(n_ranges + 1)})
    mspec = lambda w: pl.BlockSpec((mb, w), lambda b, j: (meta_row0 // mb, 0))
    fspec = lambda w: pl.BlockSpec((seq, w), lambda b, j: (b, 0))
    key_arrays = (kikr, k, v, kn, vb)
    key_widths = (128, 256, 256, 1024, 1024)
    for j0, j1 in [(None, None)] + list(zip(edges[:-1], edges[1:])):
        if j0 is None:
            grid, s_len = (1, 1), 0
            qspec = lambda w: pl.BlockSpec((tq, w), lambda b, j: (meta_row0 // tq, 0))
            frame_specs, frame_args = [], ()
            scratch = [pltpu.VMEM((tq, mb), F32)]
        else:
            s_len, span = j1 * tq, j1 - j0
            wide = s_len <= WIDE_QUERY_MAX_KEYS and j0 % span == 0 and nj % span == 0
            rq, steps = (tq * span, 1) if wide else (tq, span)
            grid = (nb, steps)
            qspec = lambda w, j0=j0, rq=rq: pl.BlockSpec((rq, w), lambda b, j: ((b * nj + j0) * tq // rq + j, 0))
            frame_specs = [fspec(w) for w in key_widths] + [_const_spec(ckf.shape)]
            frame_args = key_arrays + (ckf,)
            scratch = [pltpu.VMEM((rq, mb), F32), pltpu.VMEM((rq, s_len), F32)]
        n_in = 12 + len(frame_specs)
        bufs = pl.pallas_call(
            functools.partial(_attn_prompt_kernel, topk=topk, s_len=s_len),
            grid=grid,
            in_specs=[qspec(1024), qspec(128), qspec(1024), qspec(1024), qspec(512), qspec(128)]
                     + [mspec(w) for w in key_widths] + [_const_spec(ckm.shape)] + frame_specs
                     + [pl.BlockSpec(memory_space=pl.ANY)] * 2,
            out_specs=[qspec(1024), qspec(1024)],
            out_shape=[jax.ShapeDtypeStruct(bufs[0].shape, BF16)] * 2,
            scratch_shapes=scratch,
            input_output_aliases={n_in: 0, n_in + 1: 1},
            compiler_params=_cparams("parallel", "arbitrary"),
            name="attn_prompt",
        )(qi, wi, q, qn, qr, cq, *key_arrays, ckm, *frame_args, *bufs)
    return bufs


def _attn_sample_kernel(qi_ref, wi_ref, q_ref, qn_ref, qr_ref, cq_ref, kikr_ref, k_ref, v_ref, kn_ref, vb_ref,
                        pki_ref, pk_ref, pv_ref, pkn_ref, pkr_ref, pvb_ref, ckp_ref, ckn_ref, bufa_ref, bufb_ref,
                        oa_ref, ob_ref, biasp_ref, biasn_ref, *, topk):
    cq = cq_ref[:, 0:1]
    ckp, ckn = ckp_ref[...], ckn_ref[...]
    kikr = kikr_ref[...]
    past = pki_ref.shape[1]
    heads = lambda ref: lambda c: ref[pl.ds(c, past, stride=A_KV_HEADS), :].astype(BF16)
    kv_of = lambda ref: lambda c: ref[:, c * A_HEAD_DIM:(c + 1) * A_HEAD_DIM]
    past_a = (pki_ref[0].astype(BF16), heads(pk_ref), heads(pv_ref), ckp)
    new_a = (kikr[:, :64], kv_of(k_ref), kv_of(v_ref), ckn)
    head_of = lambda ref: lambda h: ref[:, h * B_V:(h + 1) * B_V]
    past_b = (head_of(pkn_ref), pkr_ref[0].astype(BF16), head_of(pvb_ref), ckp)
    new_b = (head_of(kn_ref), kikr[:, 64:], head_of(vb_ref), ckn)

    def store_b(h, o):
        ob_ref[:, h * B_V:(h + 1) * B_V] = o.astype(BF16)

    jobs = _mla_jobs(qn_ref, qr_ref, cq, [past_b, new_b], store_b)
    oa_ref[...] = _dsa_core(qi_ref[...], wi_ref[...], q_ref[...], cq, [past_a, new_a], topk,
                            [biasp_ref, biasn_ref], jobs).astype(BF16)
    for job in jobs:
        job()


def _attn_sample(rows, pki, pk, pv, pkn, pkr, pvb, ckp, ckn, bufs, row0, nb, ts, topk):
    qi, wi, q, qn, qr, cq, kikr, k, v, kn, vb = rows
    past = pk.shape[1]
    blk0 = row0 // ts
    nspec = lambda w: pl.BlockSpec((ts, w), lambda b: (blk0 + b, 0))
    pspec = lambda w: pl.BlockSpec((1, past, w), lambda b: (b, 0, 0))
    p2spec = lambda w: pl.BlockSpec((past, w), lambda b: (b, 0))
    kvspec = pl.BlockSpec((None, past * A_KV_HEADS, A_HEAD_DIM), lambda b: (b, 0, 0))
    pk = pk.reshape(nb, past * A_KV_HEADS, A_HEAD_DIM)
    pv = pv.reshape(nb, past * A_KV_HEADS, A_HEAD_DIM)
    return pl.pallas_call(
        functools.partial(_attn_sample_kernel, topk=topk),
        grid=(nb,),
        in_specs=[nspec(1024), nspec(128), nspec(1024), nspec(1024), nspec(512), nspec(128),
                  nspec(128), nspec(256), nspec(256), nspec(1024), nspec(1024),
                  pspec(64), kvspec, kvspec, p2spec(1024), pspec(64), p2spec(1024),
                  _const_spec(ckp.shape), _const_spec(ckn.shape)] + [pl.BlockSpec(memory_space=pl.ANY)] * 2,
        out_specs=[nspec(1024), nspec(1024)],
        out_shape=[jax.ShapeDtypeStruct(bufs[0].shape, BF16)] * 2,
        scratch_shapes=[pltpu.VMEM((ts, past), F32), pltpu.VMEM((ts, ts), F32)],
        input_output_aliases={19: 0, 20: 1},
        compiler_params=_cparams("parallel"),
        name="attn_sample",
    )(qi, wi, q, qn, qr, cq, kikr, k, v, kn, vb, pki, pk, pv, pkn, pkr, pvb, ckp, ckn, *bufs)


def _out_proj_kernel(h_ref, oa_ref, ob_ref, w_ref, o_ref):
    half = oa_ref.shape[1]
    y = jnp.dot(oa_ref[...], w_ref[:half, :], preferred_element_type=F32)
    y = y + jnp.dot(ob_ref[...], w_ref[half:, :], preferred_element_type=F32)
    o_ref[...] = h_ref[...] + y


def _out_proj(h, oa, ob, w, tm):
    r, d = h.shape
    row = lambda wd: pl.BlockSpec((tm, wd), lambda i: (i, 0))
    return pl.pallas_call(
        _out_proj_kernel,
        grid=(r // tm,),
        in_specs=[row(d), row(oa.shape[1]), row(ob.shape[1]), _const_spec(w.shape)],
        out_specs=row(d),
        out_shape=jax.ShapeDtypeStruct((r, d), F32),
        compiler_params=_cparams("parallel"),
        name="out_proj",
    )(h, oa, ob, w)


def _mlp_kernel(h_ref, g_ref, wup_ref, wdn_ref, o_ref, xn_ref):
    @pl.when(pl.program_id(1) == 0)
    def _():
        x = h_ref[...]
        xn_ref[...] = _rms(x, g_ref[...]).astype(BF16)
        o_ref[...] = x

    u = jnp.dot(xn_ref[...], wup_ref[...], preferred_element_type=F32)
    u = jnp.maximum(u, 0.0)
    o_ref[...] += jnp.dot((u * u).astype(BF16), wdn_ref[...], preferred_element_type=F32)


def _mlp(h, g, wup, wdn, layer, tm, tf):
    r, d = h.shape
    dff = wup.shape[2]
    return pl.pallas_call(
        _mlp_kernel,
        grid=(r // tm, dff // tf),
        in_specs=[pl.BlockSpec((tm, d), lambda i, f: (i, 0)), pl.BlockSpec((1, d), lambda i, f: (0, 0)),
                  pl.BlockSpec((None, d, tf), lambda i, f: (layer, 0, f)),
                  pl.BlockSpec((None, tf, d), lambda i, f: (layer, f, 0))],
        out_specs=pl.BlockSpec((tm, d), lambda i, f: (i, 0)),
        out_shape=jax.ShapeDtypeStruct((r, d), F32),
        scratch_shapes=[pltpu.VMEM((tm, d), BF16)],
        compiler_params=_cparams("parallel", "arbitrary"),
        name="mlp",
    )(h, g, wup, wdn)


def _cd_proj_kernel(h_ref, g_ref, w_ref, z_ref, xn_ref):
    @pl.when(pl.program_id(1) == 0)
    def _():
        xn_ref[...] = _rms(h_ref[...], g_ref[...]).astype(BF16)

    z_ref[...] = jnp.dot(xn_ref[...], w_ref[...], preferred_element_type=F32)


def _cd_proj(h, g, w, tm, tn):
    r, d = h.shape
    n = w.shape[1]
    return pl.pallas_call(
        _cd_proj_kernel,
        grid=(r // tm, n // tn),
        in_specs=[pl.BlockSpec((tm, d), lambda i, j: (i, 0)), pl.BlockSpec((1, d), lambda i, j: (0, 0)),
                  pl.BlockSpec((d, tn), lambda i, j: (0, j))],
        out_specs=pl.BlockSpec((tm, tn), lambda i, j: (i, j)),
        out_shape=jax.ShapeDtypeStruct((r, n), F32),
        scratch_shapes=[pltpu.VMEM((tm, d), BF16)],
        compiler_params=_cparams("parallel", "arbitrary"),
        name="cd_proj",
    )(h, g, w)


def _rglru_kernel(cx_ref, cg_ref, valid_ref, prev0_ref, h0_ref, cw_ref, cb_ref, gw_ref, gab_ref, gxb_ref, lam_ref,
                  buf_ref, o_ref, hl_ref, prev_s, h_s, a_s, b_s):
    @pl.when(pl.program_id(1) == 0)
    def _():
        prev_s[...] = prev0_ref[0]
        h_s[...] = jnp.broadcast_to(h0_ref[0], h_s.shape)

    cx = cx_ref[...]
    tm = cx.shape[0]
    full = jnp.concatenate([prev_s[...], cx], axis=0)
    prev_s[...] = cx[tm - 8:, :]
    xc = cb_ref[...] + cx * cw_ref[CONV_W - 1:CONV_W, :]
    for j in range(CONV_W - 1):
        sh = CONV_W - 1 - j
        xc = xc + full[8 - sh:8 - sh + tm, :] * cw_ref[j:j + 1, :]
    valid = valid_ref[:, 0:1] >= 0
    softplus = jnp.logaddexp(-lam_ref[...], 0.0)
    for n in range(C_BLOCKS):
        sl = slice(n * C_BLOCK, (n + 1) * C_BLOCK)
        xb = xc[:, sl]
        gates = jnp.dot(xb.astype(BF16), gw_ref[n], preferred_element_type=F32)
        r = jax.nn.sigmoid(gates[:, :C_BLOCK] + gab_ref[:, sl])
        ig = jax.nn.sigmoid(gates[:, C_BLOCK:] + gxb_ref[:, sl])
        log_a = -RG_C * r * softplus[:, sl]
        a = jnp.exp(log_a)
        a_s[:, sl] = a
        b = jnp.sqrt(1.0 - a * a) * (ig * xb)
        b_s[:, sl] = jnp.where(valid, b, 0.0)

    rid = lax.broadcasted_iota(jnp.int32, h_s.shape, 0)

    def group(g, h):
        rows = pl.ds(pl.multiple_of(g * 8, 8), 8)
        ca, cb = a_s[rows, :], b_s[rows, :]
        for s in (1, 2, 4):
            m = rid >= s
            cb = jnp.where(m, ca * pltpu.roll(cb, s, 0) + cb, cb)
            ca = jnp.where(m, ca * pltpu.roll(ca, s, 0), ca)
        hh = ca * h + cb
        b_s[rows, :] = hh
        return jnp.broadcast_to(hh[7:8, :], hh.shape)

    h = lax.fori_loop(0, tm // 8, group, h_s[...])
    h_s[...] = h
    hl_ref[0] = h[0:1, :]
    o_ref[...] = (b_s[...] * jax.nn.gelu(cg_ref[...])).astype(BF16)


def _rglru(z, valid, prev0, h0, cw, cb, gw, gab, gxb, lam, buf, row0, nb, t, tm):
    cwid = C_BLOCKS * C_BLOCK
    nj = t // tm
    b0 = row0 // tm
    zspec = lambda c: pl.BlockSpec((tm, cwid), lambda b, j: (b0 + b * nj + j, c))
    return pl.pallas_call(
        _rglru_kernel,
        grid=(nb, nj),
        in_specs=[zspec(0), zspec(1), pl.BlockSpec((tm, LANES), lambda b, j: (b0 + b * nj + j, 0)),
                  pl.BlockSpec((1, 8, cwid), lambda b, j: (b, 0, 0)), pl.BlockSpec((1, 1, cwid), lambda b, j: (b, 0, 0)),
                  _const_spec(cw.shape), _const_spec(cb.shape), _const_spec(gw.shape), _const_spec(gab.shape),
                  _const_spec(gxb.shape), _const_spec(lam.shape), pl.BlockSpec(memory_space=pl.ANY)],
        out_specs=[pl.BlockSpec((tm, cwid), lambda b, j: (b0 + b * nj + j, 0)),
                   pl.BlockSpec((1, 1, cwid), lambda b, j: (b, 0, 0))],
        out_shape=[jax.ShapeDtypeStruct(buf.shape, BF16), jax.ShapeDtypeStruct((nb, 1, cwid), F32)],
        input_output_aliases={11: 0},
        scratch_shapes=[pltpu.VMEM((8, cwid), F32), pltpu.VMEM((8, cwid), F32), pltpu.VMEM((tm, cwid), F32),
                        pltpu.VMEM((tm, cwid), F32)],
        compiler_params=_cparams("parallel", "arbitrary"),
        name="rglru",
    )(z, z, valid, prev0, h0, cw, cb, gw, gab, gxb, lam, buf)


def _retention_kernel(q_ref, k_ref, v_ref, g_ref, cos_ref, sin_ref, s0_ref, gdn_ref, lg_ref, buf_ref,
                      o_ref, s_ref, *, blk):
    @pl.when(pl.program_id(1) == 0)
    def _():
        s_ref[...] = s0_ref[...]

    cos, sin_signed = cos_ref[...], sin_ref[...]

    def rope(x):
        return x * cos + pltpu.roll(x, D_KDIM // 2, 1) * sin_signed

    ri = lax.broadcasted_iota(jnp.int32, (blk, blk), 0)
    ci = lax.broadcasted_iota(jnp.int32, (blk, blk), 1)
    diff = (ri - ci).astype(F32)
    rows = lax.broadcasted_iota(jnp.int32, (blk, 1), 0).astype(F32)
    for hd in range(D_HEADS):
        lg = lg_ref[hd]
        qsl = slice(hd * D_KDIM, (hd + 1) * D_KDIM)
        vsl = slice(hd * D_VDIM, (hd + 1) * D_VDIM)
        q = rope(q_ref[:, qsl])
        k = rope(k_ref[:, qsl]) * (D_KDIM ** -0.5)
        decay = jnp.where(diff >= 0, jnp.exp(lg * jnp.maximum(diff, 0.0)), 0.0)
        qb, kb, vb = q.astype(BF16), k.astype(BF16), v_ref[:, vsl].astype(BF16)
        att = _dot_t(qb, kb) * decay
        o = jnp.dot(att.astype(BF16), vb, preferred_element_type=F32)
        s_prev = s_ref[0, hd]
        o = o + jnp.dot(qb, s_prev.astype(BF16), preferred_element_type=F32) * jnp.exp(lg * (rows + 1.0))
        kdec = (k * jnp.exp(lg * (blk - 1.0 - rows))).astype(BF16)
        s_ref[0, hd] = jnp.exp(lg * blk) * s_prev + lax.dot_general(kdec, vb, (((0,), (0,)), ((), ())),
                                                                    preferred_element_type=F32)
        od = _rms(o, gdn_ref[...]) * jax.nn.silu(g_ref[:, vsl])
        o_ref[:, vsl] = od.astype(BF16)


def _retention(z, cos, sin_signed, s0, gdn, log_gamma, buf, col0, row0, nb, t, blk):
    nj = t // blk
    b0 = row0 // blk
    wid = D_HEADS * D_KDIM
    c0 = col0 // wid
    zspec = lambda c: pl.BlockSpec((blk, wid), lambda b, j: (b0 + b * nj + j, c0 + c))
    tspec = pl.BlockSpec((blk, D_KDIM), lambda b, j: (b0 + b * nj + j, 0))
    sspec = pl.BlockSpec((1, D_HEADS, D_KDIM, D_VDIM), lambda b, j: (b, 0, 0, 0))
    return pl.pallas_call(
        functools.partial(_retention_kernel, blk=blk),
        grid=(nb, nj),
        in_specs=[zspec(0), zspec(1), zspec(2), zspec(3), tspec, tspec, sspec, _const_spec(gdn.shape),
                  pl.BlockSpec(memory_space=pltpu.SMEM), pl.BlockSpec(memory_space=pl.ANY)],
        out_specs=[pl.BlockSpec((blk, wid), lambda b, j: (b0 + b * nj + j, 0)), sspec],
        out_shape=[jax.ShapeDtypeStruct(buf.shape, BF16), jax.ShapeDtypeStruct(s0.shape, F32)],
        input_output_aliases={9: 0},
        compiler_params=_cparams("parallel", "arbitrary"),
        name="retention",
    )(z, z, z, z, cos, sin_signed, s0, gdn, log_gamma, buf)


def _rope_tables(pos_groups, reps_groups, d, width):
    inv = ROPE_THETA ** (-jnp.arange(0, d, 2, dtype=F32) / d)
    cos_rows, sin_rows = [], []
    for pos, reps in zip(pos_groups, reps_groups):
        ang = pos.astype(F32)[:, None] * inv[None, :]
        cos, sin = lax.optimization_barrier((jnp.cos(ang), jnp.sin(ang)))
        cos_rows.append(jnp.tile(jnp.tile(jnp.concatenate([cos, cos], axis=-1), (1, width // d)), (reps, 1)))
        sin_rows.append(jnp.tile(jnp.tile(jnp.concatenate([-sin, sin], axis=-1), (1, width // d)), (reps, 1)))
    return jnp.concatenate(cos_rows, axis=0), jnp.concatenate(sin_rows, axis=0)


def kernel(x_prompt, x_sample, cache_a_k, cache_a_v, cache_a_kidx, cache_b_latent, cache_b_krope, state_c_conv,
           state_c_h, state_d_s, meta_tokens, norm_mix, norm_mlp, ab_w_in, ab_w_out, a_q_norm, a_k_norm,
           b_q_lat_norm, b_w_uq, b_kv_lat_norm, b_w_ukv, b_qn_norm, b_qr_norm, b_kn_norm, b_kr_norm, cd_w_in,
           cd_w_out, c_conv_w, c_conv_b, c_gate_a_w, c_gate_a_b, c_gate_x_w, c_gate_x_b, c_lambda, d_out_norm,
           mlp_w_up, mlp_w_down):
    nb_p, seq_p, d_model = x_prompt.shape
    nb_s, seq_s, _ = x_sample.shape
    past = cache_a_k.shape[2]
    n_meta = meta_tokens.shape[0]
    depth = norm_mix.shape[0]
    mb = -(-n_meta // ROW_BLOCK) * ROW_BLOCK
    pad = mb - n_meta
    rf = nb_p * seq_p
    rs = nb_s * seq_s
    row_m, row_s = rf, rf + mb
    r_tot = rf + mb + rs
    assert seq_p % ROW_BLOCK == 0 and ROW_BLOCK % CHUNK == 0 and seq_s % 8 == 0 and past % 8 == 0
    assert row_s % seq_s == 0 and r_tot % ROW_BLOCK == 0
    blocks = r_tot // ROW_BLOCK
    tile = ROW_BLOCK * max(d for d in range(1, 9) if blocks % d == 0)
    topk_p = min(TOPK_MAX, seq_p // 4)
    topk_s = min(TOPK_MAX, (past + seq_s) // 4)

    i32 = jnp.int32
    pos_f = n_meta + jnp.arange(seq_p, dtype=i32)
    pos_m = jnp.concatenate([jnp.zeros((pad,), i32), jnp.arange(n_meta, dtype=i32)])
    ck_f = 1 + jnp.arange(seq_p, dtype=i32) // CHUNK
    cq_m = jnp.concatenate([jnp.full((pad,), -1, i32), jnp.zeros((n_meta,), i32)])
    ck_m = jnp.concatenate([jnp.full((pad,), 2 ** 30, i32), jnp.zeros((n_meta,), i32)])
    pos_s_all = jnp.arange(past + seq_s, dtype=i32)
    ck_s = pos_s_all // CHUNK
    pos_s = pos_s_all[past:]
    cq_rows = jnp.concatenate([jnp.tile(ck_f, nb_p), cq_m, jnp.tile(ck_s[past:], nb_s)])
    cq_rows = jnp.broadcast_to(cq_rows[:, None], (r_tot, LANES))
    groups, reps = (pos_f, pos_m, pos_s), (nb_p, 1, nb_s)
    tabs128 = _rope_tables(groups, reps, 128, 128)
    tabs64 = _rope_tables(groups, reps, 64, 128)
    tabs256 = _rope_tables(groups, reps, 256, 256)

    h = jnp.concatenate([x_prompt.reshape(rf, d_model), jnp.zeros((pad, d_model), F32), meta_tokens.astype(F32),
                         x_sample.reshape(rs, d_model)], axis=0)
    mlp_up, mlp_down = mlp_w_up.astype(BF16), mlp_w_down.astype(BF16)

    def prompt_rows(x):
        meta = jnp.broadcast_to(x[row_m + pad:row_s][None], (nb_p, n_meta, x.shape[1]))
        return jnp.concatenate([meta, x[:rf].reshape(nb_p, seq_p, -1)], axis=1)

    def sample_rows(x):
        return x[row_s:].reshape(nb_s, seq_s, -1)

    ab_p, ab_s, cd_p, cd_s = [], [], [], []
    for layer in range(depth):
        i = layer // 2
        gmix = norm_mix[layer][None, :]
        if layer % 2 == 0:
            w = ab_w_in[i]
            offs = np.cumsum([0, 1024, 256, 256, 1024, 64, 16, 512, 256, 64])
            sec = lambda n: w[:, offs[n]:offs[n + 1]]
            win = jnp.concatenate([sec(0), sec(1), sec(2), sec(3), sec(6), sec(7), sec(4), sec(8), sec(5),
                                   jnp.zeros((d_model, AB_IN_PAD - 3472), F32)], axis=1).astype(BF16)
            wuq = b_w_uq[i].reshape(-1, B_HEADS, B_NOPE + B_ROPE)
            wuq = jnp.concatenate([wuq[:, :, :B_NOPE].reshape(-1, B_HEADS * B_NOPE),
                                   wuq[:, :, B_NOPE:].reshape(-1, B_HEADS * B_ROPE)], axis=1).astype(BF16)
            wukv = b_w_ukv[i].reshape(-1, B_HEADS, B_NOPE + B_V)
            wukv = jnp.concatenate([wukv[:, :, :B_NOPE].reshape(-1, B_HEADS * B_NOPE),
                                    wukv[:, :, B_NOPE:].reshape(-1, B_HEADS * B_V)], axis=1).astype(BF16)
            two = lambda g: jnp.concatenate([g, g])[None, :]
            gkr2 = jnp.concatenate([jnp.ones((64,), F32), b_kr_norm[i]])[None, :]
            (q, kf, vf, kb, vb_a, qi, kikr, kikrb, wi, qn, qr, lat) = _ab_proj(
                h, gmix, win, wuq, a_q_norm[i][None, :], a_k_norm[i][None, :], b_q_lat_norm[i][None, :],
                b_kv_lat_norm[i][None, :], b_qn_norm[i][None, :], two(b_qr_norm[i]), gkr2,
                tabs128 + tabs64, tile)
            gkn = b_kn_norm[i][None, :]
            kn, vbb = _ukv(lat, wukv, gkn, tile)
            pkn, pvb = _ukv(cache_b_latent[i].reshape(nb_s * past, -1), wukv, gkn, 512)
            rows = (qi, wi, q, qn, qr, cq_rows, kikrb, kb, vb_a, kn, vbb)
            bufs = (jnp.zeros((r_tot, 1024), BF16), jnp.zeros((r_tot, 1024), BF16))
            bufs = _attn_prompt(rows, ck_m[None, :], ck_f[None, :], bufs, nb_p, seq_p, row_m, mb, topk_p, 8)
            oa, ob = _attn_sample(rows, cache_a_kidx[i], cache_a_k[i], cache_a_v[i], pkn, cache_b_krope[i], pvb,
                                  ck_s[None, :past], ck_s[None, past:], bufs, row_s, nb_s, seq_s, topk_s)
            h = _out_proj(h, oa, ob, ab_w_out[i].astype(BF16), tile)
            kv4 = lambda x: x.reshape(x.shape[0], x.shape[1], A_KV_HEADS, A_HEAD_DIM)
            ab_p.append((kv4(prompt_rows(kf)), kv4(prompt_rows(vf)), prompt_rows(kikr)[..., :64],
                         prompt_rows(lat), prompt_rows(kikr)[..., 64:]))
            ab_s.append((kv4(sample_rows(kf)), kv4(sample_rows(vf)), sample_rows(kikr)[..., :64],
                         sample_rows(lat), sample_rows(kikr)[..., 64:]))
        else:
            tile_cd = max(t for t in range(16, 1025, 16) if r_tot % t == 0)
            z = _cd_proj(h, gmix, cd_w_in[i].astype(BF16), tile_cd, 2048)
            cwid = C_BLOCKS * C_BLOCK
            nc = CONV_W - 1
            gw = jnp.concatenate([c_gate_a_w[i], c_gate_x_w[i]], axis=-1).astype(BF16)
            cargs = (c_conv_w[i], c_conv_b[i][None, :], gw, c_gate_a_b[i][None, :], c_gate_x_b[i][None, :],
                     c_lambda[i][None, :])
            oc = jnp.zeros((r_tot, cwid), BF16)
            oc, hl_m = _rglru(z, cq_rows, jnp.zeros((1, 8, cwid), F32), jnp.zeros((1, 1, cwid), F32), *cargs, oc,
                              row_m, 1, mb, ROW_BLOCK)
            prev_f = jnp.broadcast_to(z[row_s - 8:row_s, :cwid][None], (nb_p, 8, cwid))
            oc, hl_p = _rglru(z, cq_rows, prev_f, jnp.broadcast_to(hl_m, (nb_p, 1, cwid)), *cargs, oc,
                              0, nb_p, seq_p, ROW_BLOCK)
            prev_s = jnp.concatenate([jnp.zeros((nb_s, 8 - nc, cwid), F32), state_c_conv[i]], axis=1)
            oc, hl_s = _rglru(z, cq_rows, prev_s, state_c_h[i][:, None, :], *cargs, oc, row_s, nb_s, seq_s, seq_s)
            hl_p, hl_s = hl_p[:, 0], hl_s[:, 0]
            log_gamma = jnp.log(1.0 - 2.0 ** (-5.0 - jnp.arange(D_HEADS, dtype=F32)))
            rargs = (d_out_norm[i][None, :], log_gamma)
            od = jnp.zeros((r_tot, D_HEADS * D_VDIM), BF16)
            od, ds_m = _retention(z, *tabs256, jnp.zeros((1, D_HEADS, D_KDIM, D_VDIM), F32), *rargs, od,
                                  2 * cwid, row_m, 1, mb, ROW_BLOCK)
            od, ds_p = _retention(z, *tabs256, jnp.broadcast_to(ds_m, (nb_p, D_HEADS, D_KDIM, D_VDIM)), *rargs, od,
                                  2 * cwid, 0, nb_p, seq_p, ROW_BLOCK)
            od, ds_s = _retention(z, *tabs256, state_d_s[i], *rargs, od, 2 * cwid, row_s, nb_s, seq_s, seq_s)
            h = _out_proj(h, oc, od, cd_w_out[i].astype(BF16), tile)

            def seq_tails(row0, nb, t):
                n = min(nc, t)
                rws = row0 + t - n + (jnp.arange(nb, dtype=i32) * t)[:, None] + jnp.arange(n, dtype=i32)
                return jnp.take(z, rws.reshape(-1), axis=0)[:, :cwid].reshape(nb, n, cwid)

            assert seq_p >= nc
            cd_p.append((seq_tails(0, nb_p, seq_p), hl_p, ds_p))
            cd_s.append((jnp.concatenate([state_c_conv[i], seq_tails(row_s, nb_s, seq_s)], axis=1)[:, -nc:],
                         hl_s, ds_s))
        h = _mlp(h, norm_mlp[layer][None, :], mlp_up, mlp_down, layer, tile, 1024)

    def stack(entries, j):
        return jnp.stack([e[j] for e in entries])

    y_p = h[:rf].reshape(nb_p, seq_p, d_model)
    y_s = h[row_s:].reshape(nb_s, seq_s, d_model)
    return (y_p, y_s,
            stack(ab_p, 0), stack(ab_p, 1), stack(ab_p, 2), stack(ab_p, 3), stack(ab_p, 4),
            stack(cd_p, 0), stack(cd_p, 1), stack(cd_p, 2),
            stack(ab_s, 0), stack(ab_s, 1), stack(ab_s, 2), stack(ab_s, 3), stack(ab_s, 4),
            stack(cd_s, 0), stack(cd_s, 1), stack(cd_s, 2))
```

```python
import functools

import jax
import jax.numpy as jnp
import numpy as np
from jax import lax
from jax.experimental import pallas as pl
from jax.experimental.pallas import tpu as pltpu

F32 = jnp.float32
BF16 = jnp.bfloat16

CHUNK = 64
ROPE_THETA = 10000.0
EPS = 1e-6
A_HEADS, A_KV_HEADS, A_HEAD_DIM = 8, 2, 128
IDX_HEADS, IDX_DIM = 16, 64
TOPK_MAX = 256
B_HEADS, B_NOPE, B_ROPE, B_V = 8, 128, 64, 128
C_BLOCKS, C_BLOCK, CONV_W = 8, 128, 4
RG_C = 8.0
D_HEADS, D_KDIM, D_VDIM = 4, 256, 256

LANES = 128
ROW_BLOCK = 128
NEG = -1e30
INT_MIN = -2 ** 31
VMEM_LIMIT = 56 * 1024 * 1024


def _cparams(*sem):
    return pltpu.CompilerParams(dimension_semantics=sem, vmem_limit_bytes=VMEM_LIMIT)


def _const_spec(shape):
    nd = len(shape)
    return pl.BlockSpec(shape, lambda *_: (0,) * nd)


def _rms(x, g):
    ms = jnp.mean(x * x, axis=-1, keepdims=True)
    return x * lax.rsqrt(ms + EPS) * g


def _rms_half(x, g):
    lane = lax.broadcasted_iota(jnp.int32, x.shape, 1)
    lo = lane < 64
    xx = x * x
    s_lo = jnp.sum(jnp.where(lo, xx, 0.0), axis=-1, keepdims=True)
    s_hi = jnp.sum(jnp.where(lo, 0.0, xx), axis=-1, keepdims=True)
    ms = jnp.where(lo, s_lo, s_hi) * (1.0 / 64.0)
    return x * lax.rsqrt(ms + EPS) * g


def _rope128(x, cos, sin_signed):
    return x * cos + pltpu.roll(x, 64, 1) * sin_signed


def _rope64(x, cos, sin_signed):
    lane = lax.broadcasted_iota(jnp.int32, x.shape, 1)
    first = (lane % 64) < 32
    rot = jnp.where(first, pltpu.roll(x, 96, 1), pltpu.roll(x, 32, 1))
    return x * cos + rot * sin_signed


AB_COLS = dict(qa=(0, 1024), ka=(1024, 1280), va=(1280, 1536), qi=(1536, 2560), cq=(2560, 3072),
               ckv=(3072, 3328), kikr=(3328, 3456), wi=(3456, 3584))
AB_IN_PAD = 3584


def _ab_proj_kernel(h_ref, gmix_ref, win_ref, wuq_ref, gaq_ref, gak_ref, gqlat_ref, gkvlat_ref, gqn_ref,
                    gqr_ref, gkr_ref, c128_ref, s128_ref, c64_ref, s64_ref,
                    q_ref, kf_ref, vf_ref, kb_ref, vb_ref, qi_ref, kikr_ref, kikrb_ref, wi_ref, qn_ref, qr_ref,
                    lat_ref):
    xn = _rms(h_ref[...], gmix_ref[...]).astype(BF16)
    c128, s128, c64, s64 = c128_ref[...], s128_ref[...], c64_ref[...], s64_ref[...]

    def proj(name):
        a, b = AB_COLS[name]
        return jnp.dot(xn, win_ref[:, a:b], preferred_element_type=F32)

    z = proj('qa')
    qscale = A_HEAD_DIM ** -0.5
    for h in range(A_HEADS):
        x = _rope128(_rms(z[:, h * 128:(h + 1) * 128], gaq_ref[...]), c128, s128)
        q_ref[:, h * 128:(h + 1) * 128] = (x * qscale).astype(BF16)
    z = proj('ka')
    for h in range(A_KV_HEADS):
        x = _rope128(_rms(z[:, h * 128:(h + 1) * 128], gak_ref[...]), c128, s128)
        kf_ref[:, h * 128:(h + 1) * 128] = x
        kb_ref[:, h * 128:(h + 1) * 128] = x.astype(BF16)
    z = proj('va')
    vf_ref[...] = z
    vb_ref[...] = z.astype(BF16)
    z = proj('qi')
    for p in range(IDX_HEADS // 2):
        x = _rope64(z[:, p * 128:(p + 1) * 128], c64, s64)
        qi_ref[:, p * 128:(p + 1) * 128] = (x * (IDX_DIM ** -0.5)).astype(BF16)
    z = proj('kikr')
    lane = lax.broadcasted_iota(jnp.int32, z.shape, 1)
    x = jnp.where(lane < 64, z, _rms_half(z, gkr_ref[...]))
    x = _rope64(x, c64, s64)
    kikr_ref[...] = x
    kikrb_ref[...] = x.astype(BF16)
    wi_ref[...] = proj('wi') * (IDX_HEADS ** -0.5)
    lat = _rms(proj('ckv'), gkvlat_ref[...])
    lat_ref[...] = lat
    cq = _rms(proj('cq'), gqlat_ref[...]).astype(BF16)
    bscale = (B_NOPE + B_ROPE) ** -0.5
    zq = jnp.dot(cq, wuq_ref[:, :B_HEADS * B_NOPE], preferred_element_type=F32)
    for h in range(B_HEADS):
        x = _rms(zq[:, h * 128:(h + 1) * 128], gqn_ref[...])
        qn_ref[:, h * 128:(h + 1) * 128] = (x * bscale).astype(BF16)
    zq = jnp.dot(cq, wuq_ref[:, B_HEADS * B_NOPE:], preferred_element_type=F32)
    for p in range(B_HEADS // 2):
        x = _rope64(_rms_half(zq[:, p * 128:(p + 1) * 128], gqr_ref[...]), c64, s64)
        qr_ref[:, p * 128:(p + 1) * 128] = (x * bscale).astype(BF16)


def _ab_proj(h, gmix, win, wuq, gaq, gak, gqlat, gkvlat, gqn, gqr2, gkr2, tabs, tm):
    r, d = h.shape
    row = lambda w: pl.BlockSpec((tm, w), lambda i: (i, 0))
    outs = [(1024, BF16), (256, F32), (256, F32), (256, BF16), (256, BF16), (1024, BF16), (128, F32),
            (128, BF16), (128, F32), (1024, BF16), (512, BF16), (256, F32)]
    return pl.pallas_call(
        _ab_proj_kernel,
        grid=(r // tm,),
        in_specs=[row(d), _const_spec(gmix.shape), _const_spec(win.shape), _const_spec(wuq.shape),
                  _const_spec(gaq.shape), _const_spec(gak.shape), _const_spec(gqlat.shape),
                  _const_spec(gkvlat.shape), _const_spec(gqn.shape), _const_spec(gqr2.shape),
                  _const_spec(gkr2.shape), row(128), row(128), row(128), row(128)],
        out_specs=[row(w) for w, _ in outs],
        out_shape=[jax.ShapeDtypeStruct((r, w), dt) for w, dt in outs],
        compiler_params=_cparams("parallel"),
        name="ab_proj",
    )(h, gmix, win, wuq, gaq, gak, gqlat, gkvlat, gqn, gqr2, gkr2, *tabs)


def _ukv_kernel(lat_ref, w_ref, gkn_ref, kn_ref, vb_ref):
    lat = lat_ref[...].astype(BF16)
    z = jnp.dot(lat, w_ref[:, :B_HEADS * B_NOPE], preferred_element_type=F32)
    for h in range(B_HEADS):
        kn_ref[:, h * 128:(h + 1) * 128] = _rms(z[:, h * 128:(h + 1) * 128], gkn_ref[...]).astype(BF16)
    vb_ref[...] = jnp.dot(lat, w_ref[:, B_HEADS * B_NOPE:], preferred_element_type=F32).astype(BF16)


def _ukv(lat, w, gkn, tm):
    n = lat.shape[0]
    row = lambda wd: pl.BlockSpec((tm, wd), lambda i: (i, 0))
    return pl.pallas_call(
        _ukv_kernel,
        grid=(n // tm,),
        in_specs=[row(lat.shape[1]), _const_spec(w.shape), _const_spec(gkn.shape)],
        out_specs=[row(1024), row(1024)],
        out_shape=[jax.ShapeDtypeStruct((n, 1024), BF16)] * 2,
        compiler_params=_cparams("parallel"),
        name="mla_ukv",
    )(lat, w, gkn)


def _dot_t(a, b):
    return lax.dot_general(a, b, (((1,), (1,)), ((), ())), preferred_element_type=F32)


def _float_key(x):
    bits = pltpu.bitcast(x, jnp.int32)
    return bits ^ ((bits >> 31) & jnp.int32(0x7FFFFFFF))


def _counts_ge16(vals, cands):
    rows = cands[0].shape[0]
    c16 = [jnp.broadcast_to(c, (rows, LANES)).astype(jnp.int16) for c in cands]
    one, zero = jnp.int16(1), jnp.int16(0)
    accs = [jnp.zeros((rows, LANES), jnp.int16) for _ in cands]
    rest = [jnp.zeros((rows, 1), F32) for _ in cands]
    for v in vals:
        n_full = v.shape[1] // LANES * LANES
        for s0 in range(0, n_full, LANES):
            blk = v[:, s0:s0 + LANES]
            accs = [a + jnp.where(blk >= c, one, zero) for a, c in zip(accs, c16)]
        if n_full < v.shape[1]:
            blk = v[:, n_full:]
            w = blk.shape[1]
            rest = [r + jnp.sum(jnp.where(blk >= c[:, :w], one, zero).astype(F32), axis=-1, keepdims=True)
                    for r, c in zip(rest, c16)]
    return [jnp.sum(a.astype(F32), axis=-1, keepdims=True) + r for a, r in zip(accs, rest)]


SEARCH_STEPS = 8
UNROLL_SEARCH_MAX_KEYS = 1152
WIDE_QUERY_MAX_KEYS = 768


def _search16(vals, k, side_jobs):
    def advance(t, step):
        cnts = _counts_ge16(vals, [t + j * step for j in (1, 2, 3)])
        digit = functools.reduce(jnp.add, [jnp.where(c >= float(k), 1, 0) for c in cnts])
        return t + digit * step

    t = jnp.full((vals[0].shape[0], 1), -32768, jnp.int32)
    if sum(v.shape[1] for v in vals) > UNROLL_SEARCH_MAX_KEYS:
        return lax.fori_loop(0, SEARCH_STEPS, lambda i, t: advance(t, jnp.left_shift(jnp.int32(1), 14 - 2 * i)), t)
    for i in range(SEARCH_STEPS):
        t = advance(t, 1 << (14 - 2 * i))
        if i % 2 == 1 and side_jobs:
            side_jobs.pop(0)()
    return t


def _kth_largest(keys, k, side_jobs):
    his = [(key >> 16).astype(jnp.int16) for key in keys]
    los = [((key & 0xFFFF) - 32768).astype(jnp.int16) for key in keys]
    h = _search16(his, k, side_jobs)
    top, bottom = jnp.int16(32767), jnp.int16(-32768)
    los2 = []
    for hi, lo in zip(his, los):
        h16 = jnp.broadcast_to(h, hi.shape).astype(jnp.int16)
        los2.append(jnp.where(hi == h16, lo, jnp.where(hi > h16, top, bottom)))
    l = _search16(los2, k, side_jobs)
    return (h << 16) | ((l + 32768) & 0xFFFF)


def _softmax_pv(ss, vs):
    m = functools.reduce(jnp.maximum, [jnp.max(s, axis=-1, keepdims=True) for s in ss])
    acc = functools.reduce(jnp.add, [
        jnp.dot(jnp.exp(s - m).astype(BF16), jnp.concatenate([v, jnp.ones_like(v)], axis=1),
                preferred_element_type=F32) for s, v in zip(ss, vs)])
    d = vs[0].shape[1]
    return acc[:, :d] / acc[:, d:]


def _kv_head(x, c):
    return x(c) if callable(x) else x[:, c * A_HEAD_DIM:(c + 1) * A_HEAD_DIM]


def _tie_break(keys, thr, topk, bias_refs):
    nq = keys[0].shape[0]
    n_gt = functools.reduce(jnp.add, [jnp.sum(jnp.where(key > thr, 1.0, 0.0), axis=-1, keepdims=True)
                                      for key in keys])
    need = float(topk) - n_gt
    r_i = lax.broadcasted_iota(jnp.int32, (LANES, LANES), 0)
    c_i = lax.broadcasted_iota(jnp.int32, (LANES, LANES), 1)
    before = jnp.where(r_i < c_i, 1.0, 0.0).astype(BF16)
    seen = jnp.zeros((nq, 1), F32)
    for key, ref in zip(keys, bias_refs):
        for s0 in range(0, key.shape[1], LANES):
            blk = key[:, s0:s0 + LANES]
            w = blk.shape[1]
            eq = jnp.where(blk == thr, 1.0, 0.0)
            rank = seen + jnp.dot(eq.astype(BF16), before[:w, :w], preferred_element_type=F32)
            sel = jnp.logical_or(blk > thr, jnp.logical_and(eq > 0.0, rank < need))
            ref[:, s0:s0 + w] = jnp.where(jnp.logical_and(sel, blk > INT_MIN), 0.0, NEG)
            seen = seen + jnp.sum(eq, axis=-1, keepdims=True)


def _dsa_core(qi, wi, q, cq, pieces, topk, bias_refs, side_jobs):
    nq = qi.shape[0]
    keys = []
    for ki, _, _, ck in pieces:
        score = jnp.zeros((nq, ki.shape[0]), F32)
        for h in range(IDX_HEADS):
            s_h = _dot_t(qi[:, h * 64:(h + 1) * 64], ki)
            score = score + jnp.maximum(s_h, 0.0) * wi[:, h:h + 1]
        keys.append(jnp.where(ck <= cq, _float_key(score), INT_MIN))
    thr = _kth_largest(keys, topk, side_jobs)
    n_ge = jnp.zeros((nq, 1), F32)
    for key, ref in zip(keys, bias_refs):
        ge = key >= thr
        ref[...] = jnp.where(jnp.logical_and(ge, key > INT_MIN), 0.0, NEG)
        n_ge = n_ge + jnp.sum(jnp.where(ge, 1.0, 0.0), axis=-1, keepdims=True)
    tied = jnp.max(jnp.where(jnp.logical_and(n_ge > float(topk), thr > INT_MIN), 1.0, 0.0))
    pl.when(tied > 0.0)(functools.partial(_tie_break, keys, thr, topk, bias_refs))
    group = A_HEADS // A_KV_HEADS
    biases = [jnp.concatenate([ref[...]] * group, axis=0) for ref in bias_refs]
    outs = []
    for c in range(A_KV_HEADS):
        qg = jnp.concatenate([q[:, (c * group + g) * 128:(c * group + g + 1) * 128] for g in range(group)], axis=0)
        ss = [_dot_t(qg, _kv_head(k, c)) + bias for (_, k, _, _), bias in zip(pieces, biases)]
        o = _softmax_pv(ss, [_kv_head(v, c) for _, _, v, _ in pieces])
        outs.extend([o[g * nq:(g + 1) * nq] for g in range(group)])
    return jnp.concatenate(outs, axis=-1)


def _mla_jobs(qn, qr, cq, pieces, store):
    biases = [jnp.where(ck <= cq, 0.0, NEG) for _, _, _, ck in pieces]

    def head(h):
        ss = []
        for (kn, kr, _, _), bias in zip(pieces, biases):
            s = _dot_t(qn[:, h * B_NOPE:(h + 1) * B_NOPE], kn(h))
            s = s + _dot_t(qr[:, h * B_ROPE:(h + 1) * B_ROPE], kr)
            ss.append(s + bias)
        store(h, _softmax_pv(ss, [vb(h) for _, _, vb, _ in pieces]))

    return [functools.partial(head, h) for h in range(B_HEADS)]


def _attn_prompt_kernel(qi_ref, wi_ref, q_ref, qn_ref, qr_ref, cq_ref, mkikr_ref, mk_ref, mv_ref, mkn_ref, mvb_ref,
                        ckm_ref, *rest, topk, s_len):
    nf = 6 if s_len else 0
    oa_ref, ob_ref = rest[nf + 2:nf + 4]
    bias_refs = list(rest[nf + 4:])
    cq = cq_ref[:, 0:1]
    valid = cq >= 0
    head_of = lambda ref, n: lambda h: ref[:n, h * B_V:(h + 1) * B_V]
    kv_of = lambda ref, n: lambda c: ref[:n, c * A_HEAD_DIM:(c + 1) * A_HEAD_DIM]
    mb = mkikr_ref.shape[0]
    mkikr = mkikr_ref[...]
    ckm = ckm_ref[...]
    pieces_a = [(mkikr[:, :64], kv_of(mk_ref, mb), kv_of(mv_ref, mb), ckm)]
    pieces_b = [(head_of(mkn_ref, mb), mkikr[:, 64:], head_of(mvb_ref, mb), ckm)]
    if s_len:
        fkikr_ref, fk_ref, fv_ref, fkn_ref, fvb_ref, ckf_ref = rest[:nf]
        fkikr = fkikr_ref[:s_len, :]
        ckf = ckf_ref[:, :s_len]
        pieces_a.append((fkikr[:, :64], kv_of(fk_ref, s_len), kv_of(fv_ref, s_len), ckf))
        pieces_b.append((head_of(fkn_ref, s_len), fkikr[:, 64:], head_of(fvb_ref, s_len), ckf))

    def store_b(h, o):
        ob_ref[:, h * B_V:(h + 1) * B_V] = jnp.where(valid, o, 0.0).astype(BF16)

    jobs = _mla_jobs(qn_ref, qr_ref, cq, pieces_b, store_b)
    oa = _dsa_core(qi_ref[...], wi_ref[...], q_ref[...], cq, pieces_a, topk, bias_refs, jobs)
    oa_ref[...] = jnp.where(valid, oa, 0.0).astype(BF16)
    for job in jobs:
        job()


def _attn_prompt(rows, ckm, ckf, bufs, nb, seq, meta_row0, mb, topk, n_ranges):
    qi, wi, q, qn, qr, cq, kikr, k, v, kn, vb = rows
    tq = ROW_BLOCK
    nj = seq // tq
    edges = sorted({-(-nj * r // n_ranges) for r in range(n_ranges + 1)})
    mspec = lambda w: pl.BlockSpec((mb, w), lambda b, j: (meta_row0 // mb, 0))
    fspec = lambda w: pl.BlockSpec((seq, w), lambda b, j: (b, 0))
    key_arrays = (kikr, k, v, kn, vb)
    key_widths = (128, 256, 256, 1024, 1024)
    for j0, j1 in [(None, None)] + list(zip(edges[:-1], edges[1:])):
        if j0 is None:
            grid, s_len = (1, 1), 0
            qspec = lambda w: pl.BlockSpec((tq, w), lambda b, j: (meta_row0 // tq, 0))
            frame_specs, frame_args = [], ()
            scratch = [pltpu.VMEM((tq, mb), F32)]
        else:
            s_len, span = j1 * tq, j1 - j0
            wide = s_len <= WIDE_QUERY_MAX_KEYS and j0 % span == 0 and nj % span == 0
            rq, steps = (tq * span, 1) if wide else (tq, span)
            grid = (nb, steps)
            qspec = lambda w, j0=j0, rq=rq: pl.BlockSpec((rq, w), lambda b, j: ((b * nj + j0) * tq // rq + j, 0))
            frame_specs = [fspec(w) for w in key_widths] + [_const_spec(ckf.shape)]
            frame_args = key_arrays + (ckf,)
            scratch = [pltpu.VMEM((rq, mb), F32), pltpu.VMEM((rq, s_len), F32)]
        n_in = 12 + len(frame_specs)
        bufs = pl.pallas_call(
            functools.partial(_attn_prompt_kernel, topk=topk, s_len=s_len),
            grid=grid,
            in_specs=[qspec(1024), qspec(128), qspec(1024), qspec(1024), qspec(512), qspec(128)]
                     + [mspec(w) for w in key_widths] + [_const_spec(ckm.shape)] + frame_specs
                     + [pl.BlockSpec(memory_space=pl.ANY)] * 2,
            out_specs=[qspec(1024), qspec(1024)],
            out_shape=[jax.ShapeDtypeStruct(bufs[0].shape, BF16)] * 2,
            scratch_shapes=scratch,
            input_output_aliases={n_in: 0, n_in + 1: 1},
            compiler_params=_cparams("parallel", "arbitrary"),
            name="attn_prompt",
        )(qi, wi, q, qn, qr, cq, *key_arrays, ckm, *frame_args, *bufs)
    return bufs


def _attn_sample_kernel(qi_ref, wi_ref, q_ref, qn_ref, qr_ref, cq_ref, kikr_ref, k_ref, v_ref, kn_ref, vb_ref,
                        pki_ref, pk_ref, pv_ref, pkn_ref, pkr_ref, pvb_ref, ckp_ref, ckn_ref, bufa_ref, bufb_ref,
                        oa_ref, ob_ref, biasp_ref, biasn_ref, *, topk):
    cq = cq_ref[:, 0:1]
    ckp, ckn = ckp_ref[...], ckn_ref[...]
    kikr = kikr_ref[...]
    past = pki_ref.shape[1]
    heads = lambda ref: lambda c: ref[pl.ds(c, past, stride=A_KV_HEADS), :].astype(BF16)
    kv_of = lambda ref: lambda c: ref[:, c * A_HEAD_DIM:(c + 1) * A_HEAD_DIM]
    past_a = (pki_ref[0].astype(BF16), heads(pk_ref), heads(pv_ref), ckp)
    new_a = (kikr[:, :64], kv_of(k_ref), kv_of(v_ref), ckn)
    head_of = lambda ref: lambda h: ref[:, h * B_V:(h + 1) * B_V]
    past_b = (head_of(pkn_ref), pkr_ref[0].astype(BF16), head_of(pvb_ref), ckp)
    new_b = (head_of(kn_ref), kikr[:, 64:], head_of(vb_ref), ckn)

    def store_b(h, o):
        ob_ref[:, h * B_V:(h + 1) * B_V] = o.astype(BF16)

    jobs = _mla_jobs(qn_ref, qr_ref, cq, [past_b, new_b], store_b)
    oa_ref[...] = _dsa_core(qi_ref[...], wi_ref[...], q_ref[...], cq, [past_a, new_a], topk,
                            [biasp_ref, biasn_ref], jobs).astype(BF16)
    for job in jobs:
        job()


def _attn_sample(rows, pki, pk, pv, pkn, pkr, pvb, ckp, ckn, bufs, row0, nb, ts, topk):
    qi, wi, q, qn, qr, cq, kikr, k, v, kn, vb = rows
    past = pk.shape[1]
    blk0 = row0 // ts
    nspec = lambda w: pl.BlockSpec((ts, w), lambda b: (blk0 + b, 0))
    pspec = lambda w: pl.BlockSpec((1, past, w), lambda b: (b, 0, 0))
    p2spec = lambda w: pl.BlockSpec((past, w), lambda b: (b, 0))
    kvspec = pl.BlockSpec((None, past * A_KV_HEADS, A_HEAD_DIM), lambda b: (b, 0, 0))
    pk = pk.reshape(nb, past * A_KV_HEADS, A_HEAD_DIM)
    pv = pv.reshape(nb, past * A_KV_HEADS, A_HEAD_DIM)
    return pl.pallas_call(
        functools.partial(_attn_sample_kernel, topk=topk),
        grid=(nb,),
        in_specs=[nspec(1024), nspec(128), nspec(1024), nspec(1024), nspec(512), nspec(128),
                  nspec(128), nspec(256), nspec(256), nspec(1024), nspec(1024),
                  pspec(64), kvspec, kvspec, p2spec(1024), pspec(64), p2spec(1024),
                  _const_spec(ckp.shape), _const_spec(ckn.shape)] + [pl.BlockSpec(memory_space=pl.ANY)] * 2,
        out_specs=[nspec(1024), nspec(1024)],
        out_shape=[jax.ShapeDtypeStruct(bufs[0].shape, BF16)] * 2,
        scratch_shapes=[pltpu.VMEM((ts, past), F32), pltpu.VMEM((ts, ts), F32)],
        input_output_aliases={19: 0, 20: 1},
        compiler_params=_cparams("parallel"),
        name="attn_sample",
    )(qi, wi, q, qn, qr, cq, kikr, k, v, kn, vb, pki, pk, pv, pkn, pkr, pvb, ckp, ckn, *bufs)


def _out_proj_kernel(h_ref, oa_ref, ob_ref, w_ref, o_ref):
    half = oa_ref.shape[1]
    y = jnp.dot(oa_ref[...], w_ref[:half, :], preferred_element_type=F32)
    y = y + jnp.dot(ob_ref[...], w_ref[half:, :], preferred_element_type=F32)
    o_ref[...] = h_ref[...] + y


def _out_proj(h, oa, ob, w, tm):
    r, d = h.shape
    row = lambda wd: pl.BlockSpec((tm, wd), lambda i: (i, 0))
    return pl.pallas_call(
        _out_proj_kernel,
        grid=(r // tm,),
        in_specs=[row(d), row(oa.shape[1]), row(ob.shape[1]), _const_spec(w.shape)],
        out_specs=row(d),
        out_shape=jax.ShapeDtypeStruct((r, d), F32),
        compiler_params=_cparams("parallel"),
        name="out_proj",
    )(h, oa, ob, w)


def _mlp_kernel(h_ref, g_ref, wup_ref, wdn_ref, o_ref, xn_ref):
    @pl.when(pl.program_id(1) == 0)
    def _():
        x = h_ref[...]
        xn_ref[...] = _rms(x, g_ref[...]).astype(BF16)
        o_ref[...] = x

    u = jnp.dot(xn_ref[...], wup_ref[...], preferred_element_type=F32)
    u = jnp.maximum(u, 0.0)
    o_ref[...] += jnp.dot((u * u).astype(BF16), wdn_ref[...], preferred_element_type=F32)


def _mlp(h, g, wup, wdn, layer, tm, tf):
    r, d = h.shape
    dff = wup.shape[2]
    return pl.pallas_call(
        _mlp_kernel,
        grid=(r // tm, dff // tf),
        in_specs=[pl.BlockSpec((tm, d), lambda i, f: (i, 0)), pl.BlockSpec((1, d), lambda i, f: (0, 0)),
                  pl.BlockSpec((None, d, tf), lambda i, f: (layer, 0, f)),
                  pl.BlockSpec((None, tf, d), lambda i, f: (layer, f, 0))],
        out_specs=pl.BlockSpec((tm, d), lambda i, f: (i, 0)),
        out_shape=jax.ShapeDtypeStruct((r, d), F32),
        scratch_shapes=[pltpu.VMEM((tm, d), BF16)],
        compiler_params=_cparams("parallel", "arbitrary"),
        name="mlp",
    )(h, g, wup, wdn)


def _cd_proj_kernel(h_ref, g_ref, w_ref, z_ref, xn_ref):
    @pl.when(pl.program_id(1) == 0)
    def _():
        xn_ref[...] = _rms(h_ref[...], g_ref[...]).astype(BF16)

    z_ref[...] = jnp.dot(xn_ref[...], w_ref[...], preferred_element_type=F32)


def _cd_proj(h, g, w, tm, tn):
    r, d = h.shape
    n = w.shape[1]
    return pl.pallas_call(
        _cd_proj_kernel,
        grid=(r // tm, n // tn),
        in_specs=[pl.BlockSpec((tm, d), lambda i, j: (i, 0)), pl.BlockSpec((1, d), lambda i, j: (0, 0)),
                  pl.BlockSpec((d, tn), lambda i, j: (0, j))],
        out_specs=pl.BlockSpec((tm, tn), lambda i, j: (i, j)),
        out_shape=jax.ShapeDtypeStruct((r, n), F32),
        scratch_shapes=[pltpu.VMEM((tm, d), BF16)],
        compiler_params=_cparams("parallel", "arbitrary"),
        name="cd_proj",
    )(h, g, w)


def _rglru_kernel(cx_ref, cg_ref, valid_ref, prev0_ref, h0_ref, cw_ref, cb_ref, gw_ref, gab_ref, gxb_ref, lam_ref,
                  buf_ref, o_ref, hl_ref, prev_s, h_s, a_s, b_s):
    @pl.when(pl.program_id(1) == 0)
    def _():
        prev_s[...] = prev0_ref[0]
        h_s[...] = jnp.broadcast_to(h0_ref[0], h_s.shape)

    cx = cx_ref[...]
    tm = cx.shape[0]
    full = jnp.concatenate([prev_s[...], cx], axis=0)
    prev_s[...] = cx[tm - 8:, :]
    xc = cb_ref[...] + cx * cw_ref[CONV_W - 1:CONV_W, :]
    for j in range(CONV_W - 1):
        sh = CONV_W - 1 - j
        xc = xc + full[8 - sh:8 - sh + tm, :] * cw_ref[j:j + 1, :]
    valid = valid_ref[:, 0:1] >= 0
    softplus = jnp.logaddexp(-lam_ref[...], 0.0)
    for n in range(C_BLOCKS):
        sl = slice(n * C_BLOCK, (n + 1) * C_BLOCK)
        xb = xc[:, sl]
        gates = jnp.dot(xb.astype(BF16), gw_ref[n], preferred_element_type=F32)
        r = jax.nn.sigmoid(gates[:, :C_BLOCK] + gab_ref[:, sl])
        ig = jax.nn.sigmoid(gates[:, C_BLOCK:] + gxb_ref[:, sl])
        log_a = -RG_C * r * softplus[:, sl]
        a = jnp.exp(log_a)
        a_s[:, sl] = a
        b = jnp.sqrt(1.0 - a * a) * (ig * xb)
        b_s[:, sl] = jnp.where(valid, b, 0.0)

    rid = lax.broadcasted_iota(jnp.int32, h_s.shape, 0)

    def group(g, h):
        rows = pl.ds(pl.multiple_of(g * 8, 8), 8)
        ca, cb = a_s[rows, :], b_s[rows, :]
        for s in (1, 2, 4):
            m = rid >= s
            cb = jnp.where(m, ca * pltpu.roll(cb, s, 0) + cb, cb)
            ca = jnp.where(m, ca * pltpu.roll(ca, s, 0), ca)
        hh = ca * h + cb
        b_s[rows, :] = hh
        return jnp.broadcast_to(hh[7:8, :], hh.shape)

    h = lax.fori_loop(0, tm // 8, group, h_s[...])
    h_s[...] = h
    hl_ref[0] = h[0:1, :]
    o_ref[...] = (b_s[...] * jax.nn.gelu(cg_ref[...])).astype(BF16)


def _rglru(z, valid, prev0, h0, cw, cb, gw, gab, gxb, lam, buf, row0, nb, t, tm):
    cwid = C_BLOCKS * C_BLOCK
    nj = t // tm
    b0 = row0 // tm
    zspec = lambda c: pl.BlockSpec((tm, cwid), lambda b, j: (b0 + b * nj + j, c))
    return pl.pallas_call(
        _rglru_kernel,
        grid=(nb, nj),
        in_specs=[zspec(0), zspec(1), pl.BlockSpec((tm, LANES), lambda b, j: (b0 + b * nj + j, 0)),
                  pl.BlockSpec((1, 8, cwid), lambda b, j: (b, 0, 0)), pl.BlockSpec((1, 1, cwid), lambda b, j: (b, 0, 0)),
                  _const_spec(cw.shape), _const_spec(cb.shape), _const_spec(gw.shape), _const_spec(gab.shape),
                  _const_spec(gxb.shape), _const_spec(lam.shape), pl.BlockSpec(memory_space=pl.ANY)],
        out_specs=[pl.BlockSpec((tm, cwid), lambda b, j: (b0 + b * nj + j, 0)),
                   pl.BlockSpec((1, 1, cwid), lambda b, j: (b, 0, 0))],
        out_shape=[jax.ShapeDtypeStruct(buf.shape, BF16), jax.ShapeDtypeStruct((nb, 1, cwid), F32)],
        input_output_aliases={11: 0},
        scratch_shapes=[pltpu.VMEM((8, cwid), F32), pltpu.VMEM((8, cwid), F32), pltpu.VMEM((tm, cwid), F32),
                        pltpu.VMEM((tm, cwid), F32)],
        compiler_params=_cparams("parallel", "arbitrary"),
        name="rglru",
    )(z, z, valid, prev0, h0, cw, cb, gw, gab, gxb, lam, buf)


def _retention_kernel(q_ref, k_ref, v_ref, g_ref, cos_ref, sin_ref, s0_ref, gdn_ref, lg_ref, buf_ref,
                      o_ref, s_ref, *, blk):
    @pl.when(pl.program_id(1) == 0)
    def _():
        s_ref[...] = s0_ref[...]

    cos, sin_signed = cos_ref[...], sin_ref[...]

    def rope(x):
        return x * cos + pltpu.roll(x, D_KDIM // 2, 1) * sin_signed

    ri = lax.broadcasted_iota(jnp.int32, (blk, blk), 0)
    ci = lax.broadcasted_iota(jnp.int32, (blk, blk), 1)
    diff = (ri - ci).astype(F32)
    rows = lax.broadcasted_iota(jnp.int32, (blk, 1), 0).astype(F32)
    for hd in range(D_HEADS):
        lg = lg_ref[hd]
        qsl = slice(hd * D_KDIM, (hd + 1) * D_KDIM)
        vsl = slice(hd * D_VDIM, (hd + 1) * D_VDIM)
        q = rope(q_ref[:, qsl])
        k = rope(k_ref[:, qsl]) * (D_KDIM ** -0.5)
        decay = jnp.where(diff >= 0, jnp.exp(lg * jnp.maximum(diff, 0.0)), 0.0)
        qb, kb, vb = q.astype(BF16), k.astype(BF16), v_ref[:, vsl].astype(BF16)
        att = _dot_t(qb, kb) * decay
        o = jnp.dot(att.astype(BF16), vb, preferred_element_type=F32)
        s_prev = s_ref[0, hd]
        o = o + jnp.dot(qb, s_prev.astype(BF16), preferred_element_type=F32) * jnp.exp(lg * (rows + 1.0))
        kdec = (k * jnp.exp(lg * (blk - 1.0 - rows))).astype(BF16)
        s_ref[0, hd] = jnp.exp(lg * blk) * s_prev + lax.dot_general(kdec, vb, (((0,), (0,)), ((), ())),
                                                                    preferred_element_type=F32)
        od = _rms(o, gdn_ref[...]) * jax.nn.silu(g_ref[:, vsl])
        o_ref[:, vsl] = od.astype(BF16)


def _retention(z, cos, sin_signed, s0, gdn, log_gamma, buf, col0, row0, nb, t, blk):
    nj = t // blk
    b0 = row0 // blk
    wid = D_HEADS * D_KDIM
    c0 = col0 // wid
    zspec = lambda c: pl.BlockSpec((blk, wid), lambda b, j: (b0 + b * nj + j, c0 + c))
    tspec = pl.BlockSpec((blk, D_KDIM), lambda b, j: (b0 + b * nj + j, 0))
    sspec = pl.BlockSpec((1, D_HEADS, D_KDIM, D_VDIM), lambda b, j: (b, 0, 0, 0))
    return pl.pallas_call(
        functools.partial(_retention_kernel, blk=blk),
        grid=(nb, nj),
        in_specs=[zspec(0), zspec(1), zspec(2), zspec(3), tspec, tspec, sspec, _const_spec(gdn.shape),
                  pl.BlockSpec(memory_space=pltpu.SMEM), pl.BlockSpec(memory_space=pl.ANY)],
        out_specs=[pl.BlockSpec((blk, wid), lambda b, j: (b0 + b * nj + j, 0)), sspec],
        out_shape=[jax.ShapeDtypeStruct(buf.shape, BF16), jax.ShapeDtypeStruct(s0.shape, F32)],
        input_output_aliases={9: 0},
        compiler_params=_cparams("parallel", "arbitrary"),
        name="retention",
    )(z, z, z, z, cos, sin_signed, s0, gdn, log_gamma, buf)


def _rope_tables(pos_groups, reps_groups, d, width):
    inv = ROPE_THETA ** (-jnp.arange(0, d, 2, dtype=F32) / d)
    cos_rows, sin_rows = [], []
    for pos, reps in zip(pos_groups, reps_groups):
        ang = pos.astype(F32)[:, None] * inv[None, :]
        cos, sin = lax.optimization_barrier((jnp.cos(ang), jnp.sin(ang)))
        cos_rows.append(jnp.tile(jnp.tile(jnp.concatenate([cos, cos], axis=-1), (1, width // d)), (reps, 1)))
        sin_rows.append(jnp.tile(jnp.tile(jnp.concatenate([-sin, sin], axis=-1), (1, width // d)), (reps, 1)))
    return jnp.concatenate(cos_rows, axis=0), jnp.concatenate(sin_rows, axis=0)


def kernel(x_prompt, x_sample, cache_a_k, cache_a_v, cache_a_kidx, cache_b_latent, cache_b_krope, state_c_conv,
           state_c_h, state_d_s, meta_tokens, norm_mix, norm_mlp, ab_w_in, ab_w_out, a_q_norm, a_k_norm,
           b_q_lat_norm, b_w_uq, b_kv_lat_norm, b_w_ukv, b_qn_norm, b_qr_norm, b_kn_norm, b_kr_norm, cd_w_in,
           cd_w_out, c_conv_w, c_conv_b, c_gate_a_w, c_gate_a_b, c_gate_x_w, c_gate_x_b, c_lambda, d_out_norm,
           mlp_w_up, mlp_w_down):
    nb_p, seq_p, d_model = x_prompt.shape
    nb_s, seq_s, _ = x_sample.shape
    past = cache_a_k.shape[2]
    n_meta = meta_tokens.shape[0]
    depth = norm_mix.shape[0]
    mb = -(-n_meta // ROW_BLOCK) * ROW_BLOCK
    pad = mb - n_meta
    rf = nb_p * seq_p
    rs = nb_s * seq_s
    row_m, row_s = rf, rf + mb
    r_tot = rf + mb + rs
    assert seq_p % ROW_BLOCK == 0 and ROW_BLOCK % CHUNK == 0 and seq_s % 8 == 0 and past % 8 == 0
    assert row_s % seq_s == 0 and r_tot % ROW_BLOCK == 0
    blocks = r_tot // ROW_BLOCK
    tile = ROW_BLOCK * max(d for d in range(1, 9) if blocks % d == 0)
    topk_p = min(TOPK_MAX, seq_p // 4)
    topk_s = min(TOPK_MAX, (past + seq_s) // 4)

    i32 = jnp.int32
    pos_f = n_meta + jnp.arange(seq_p, dtype=i32)
    pos_m = jnp.concatenate([jnp.zeros((pad,), i32), jnp.arange(n_meta, dtype=i32)])
    ck_f = 1 + jnp.arange(seq_p, dtype=i32) // CHUNK
    cq_m = jnp.concatenate([jnp.full((pad,), -1, i32), jnp.zeros((n_meta,), i32)])
    ck_m = jnp.concatenate([jnp.full((pad,), 2 ** 30, i32), jnp.zeros((n_meta,), i32)])
    pos_s_all = jnp.arange(past + seq_s, dtype=i32)
    ck_s = pos_s_all // CHUNK
    pos_s = pos_s_all[past:]
    cq_rows = jnp.concatenate([jnp.tile(ck_f, nb_p), cq_m, jnp.tile(ck_s[past:], nb_s)])
    cq_rows = jnp.broadcast_to(cq_rows[:, None], (r_tot, LANES))
    groups, reps = (pos_f, pos_m, pos_s), (nb_p, 1, nb_s)
    tabs128 = _rope_tables(groups, reps, 128, 128)
    tabs64 = _rope_tables(groups, reps, 64, 128)
    tabs256 = _rope_tables(groups, reps, 256, 256)

    h = jnp.concatenate([x_prompt.reshape(rf, d_model), jnp.zeros((pad, d_model), F32), meta_tokens.astype(F32),
                         x_sample.reshape(rs, d_model)], axis=0)
    mlp_up, mlp_down = mlp_w_up.astype(BF16), mlp_w_down.astype(BF16)

    def prompt_rows(x):
        meta = jnp.broadcast_to(x[row_m + pad:row_s][None], (nb_p, n_meta, x.shape[1]))
        return jnp.concatenate([meta, x[:rf].reshape(nb_p, seq_p, -1)], axis=1)

    def sample_rows(x):
        return x[row_s:].reshape(nb_s, seq_s, -1)

    ab_p, ab_s, cd_p, cd_s = [], [], [], []
    for layer in range(depth):
        i = layer // 2
        gmix = norm_mix[layer][None, :]
        if layer % 2 == 0:
            w = ab_w_in[i]
            offs = np.cumsum([0, 1024, 256, 256, 1024, 64, 16, 512, 256, 64])
            sec = lambda n: w[:, offs[n]:offs[n + 1]]
            win = jnp.concatenate([sec(0), sec(1), sec(2), sec(3), sec(6), sec(7), sec(4), sec(8), sec(5),
                                   jnp.zeros((d_model, AB_IN_PAD - 3472), F32)], axis=1).astype(BF16)
            wuq = b_w_uq[i].reshape(-1, B_HEADS, B_NOPE + B_ROPE)
            wuq = jnp.concatenate([wuq[:, :, :B_NOPE].reshape(-1, B_HEADS * B_NOPE),
                                   wuq[:, :, B_NOPE:].reshape(-1, B_HEADS * B_ROPE)], axis=1).astype(BF16)
            wukv = b_w_ukv[i].reshape(-1, B_HEADS, B_NOPE + B_V)
            wukv = jnp.concatenate([wukv[:, :, :B_NOPE].reshape(-1, B_HEADS * B_NOPE),
                                    wukv[:, :, B_NOPE:].reshape(-1, B_HEADS * B_V)], axis=1).astype(BF16)
            two = lambda g: jnp.concatenate([g, g])[None, :]
            gkr2 = jnp.concatenate([jnp.ones((64,), F32), b_kr_norm[i]])[None, :]
            (q, kf, vf, kb, vb_a, qi, kikr, kikrb, wi, qn, qr, lat) = _ab_proj(
                h, gmix, win, wuq, a_q_norm[i][None, :], a_k_norm[i][None, :], b_q_lat_norm[i][None, :],
                b_kv_lat_norm[i][None, :], b_qn_norm[i][None, :], two(b_qr_norm[i]), gkr2,
                tabs128 + tabs64, tile)
            gkn = b_kn_norm[i][None, :]
            kn, vbb = _ukv(lat, wukv, gkn, tile)
            pkn, pvb = _ukv(cache_b_latent[i].reshape(nb_s * past, -1), wukv, gkn, 512)
            rows = (qi, wi, q, qn, qr, cq_rows, kikrb, kb, vb_a, kn, vbb)
            bufs = (jnp.zeros((r_tot, 1024), BF16), jnp.zeros((r_tot, 1024), BF16))
            bufs = _attn_prompt(rows, ck_m[None, :], ck_f[None, :], bufs, nb_p, seq_p, row_m, mb, topk_p, 8)
            oa, ob = _attn_sample(rows, cache_a_kidx[i], cache_a_k[i], cache_a_v[i], pkn, cache_b_krope[i], pvb,
                                  ck_s[None, :past], ck_s[None, past:], bufs, row_s, nb_s, seq_s, topk_s)
            h = _out_proj(h, oa, ob, ab_w_out[i].astype(BF16), tile)
            kv4 = lambda x: x.reshape(x.shape[0], x.shape[1], A_KV_HEADS, A_HEAD_DIM)
            ab_p.append((kv4(prompt_rows(kf)), kv4(prompt_rows(vf)), prompt_rows(kikr)[..., :64],
                         prompt_rows(lat), prompt_rows(kikr)[..., 64:]))
            ab_s.append((kv4(sample_rows(kf)), kv4(sample_rows(vf)), sample_rows(kikr)[..., :64],
                         sample_rows(lat), sample_rows(kikr)[..., 64:]))
        else:
            tile_cd = max(t for t in range(16, 1025, 16) if r_tot % t == 0)
            z = _cd_proj(h, gmix, cd_w_in[i].astype(BF16), tile_cd, 2048)
            cwid = C_BLOCKS * C_BLOCK
            nc = CONV_W - 1
            gw = jnp.concatenate([c_gate_a_w[i], c_gate_x_w[i]], axis=-1).astype(BF16)
            cargs = (c_conv_w[i], c_conv_b[i][None, :], gw, c_gate_a_b[i][None, :], c_gate_x_b[i][None, :],
                     c_lambda[i][None, :])
            oc = jnp.zeros((r_tot, cwid), BF16)
            oc, hl_m = _rglru(z, cq_rows, jnp.zeros((1, 8, cwid), F32), jnp.zeros((1, 1, cwid), F32), *cargs, oc,
                              row_m, 1, mb, ROW_BLOCK)
            prev_f = jnp.broadcast_to(z[row_s - 8:row_s, :cwid][None], (nb_p, 8, cwid))
            blk_f = 2 * ROW_BLOCK if seq_p % (2 * ROW_BLOCK) == 0 else ROW_BLOCK
            oc, hl_p = _rglru(z, cq_rows, prev_f, jnp.broadcast_to(hl_m, (nb_p, 1, cwid)), *cargs, oc,
                              0, nb_p, seq_p, blk_f)
            prev_s = jnp.concatenate([jnp.zeros((nb_s, 8 - nc, cwid), F32), state_c_conv[i]], axis=1)
            oc, hl_s = _rglru(z, cq_rows, prev_s, state_c_h[i][:, None, :], *cargs, oc, row_s, nb_s, seq_s, seq_s)
            hl_p, hl_s = hl_p[:, 0], hl_s[:, 0]
            log_gamma = jnp.log(1.0 - 2.0 ** (-5.0 - jnp.arange(D_HEADS, dtype=F32)))
            rargs = (d_out_norm[i][None, :], log_gamma)
            od = jnp.zeros((r_tot, D_HEADS * D_VDIM), BF16)
            od, ds_m = _retention(z, *tabs256, jnp.zeros((1, D_HEADS, D_KDIM, D_VDIM), F32), *rargs, od,
                                  2 * cwid, row_m, 1, mb, ROW_BLOCK)
            od, ds_p = _retention(z, *tabs256, jnp.broadcast_to(ds_m, (nb_p, D_HEADS, D_KDIM, D_VDIM)), *rargs, od,
                                  2 * cwid, 0, nb_p, seq_p, blk_f)
            od, ds_s = _retention(z, *tabs256, state_d_s[i], *rargs, od, 2 * cwid, row_s, nb_s, seq_s, seq_s)
            h = _out_proj(h, oc, od, cd_w_out[i].astype(BF16), tile)

            def seq_tails(row0, nb, t):
                n = min(nc, t)
                rws = row0 + t - n + (jnp.arange(nb, dtype=i32) * t)[:, None] + jnp.arange(n, dtype=i32)
                return jnp.take(z, rws.reshape(-1), axis=0)[:, :cwid].reshape(nb, n, cwid)

            assert seq_p >= nc
            cd_p.append((seq_tails(0, nb_p, seq_p), hl_p, ds_p))
            cd_s.append((jnp.concatenate([state_c_conv[i], seq_tails(row_s, nb_s, seq_s)], axis=1)[:, -nc:],
                         hl_s, ds_s))
        h = _mlp(h, norm_mlp[layer][None, :], mlp_up, mlp_down, layer, tile, 1024)

    def stack(entries, j):
        return jnp.stack([e[j] for e in entries])

    y_p = h[:rf].reshape(nb_p, seq_p, d_model)
    y_s = h[row_s:].reshape(nb_s, seq_s, d_model)
    return (y_p, y_s,
            stack(ab_p, 0), stack(ab_p, 1), stack(ab_p, 2), stack(ab_p, 3), stack(ab_p, 4),
            stack(cd_p, 0), stack(cd_p, 1), stack(cd_p, 2),
            stack(ab_s, 0), stack(ab_s, 1), stack(ab_s, 2), stack(ab_s, 3), stack(ab_s, 4),
            stack(cd_s, 0), stack(cd_s, 1), stack(cd_s, 2))
```

```python
import functools

import jax
import jax.numpy as jnp
import numpy as np
from jax import lax
from jax.experimental import pallas as pl
from jax.experimental.pallas import tpu as pltpu

F32 = jnp.float32
BF16 = jnp.bfloat16

CHUNK = 64
ROPE_THETA = 10000.0
EPS = 1e-6
A_HEADS, A_KV_HEADS, A_HEAD_DIM = 8, 2, 128
IDX_HEADS, IDX_DIM = 16, 64
TOPK_MAX = 256
B_HEADS, B_NOPE, B_ROPE, B_V = 8, 128, 64, 128
C_BLOCKS, C_BLOCK, CONV_W = 8, 128, 4
RG_C = 8.0
D_HEADS, D_KDIM, D_VDIM = 4, 256, 256

LANES = 128
ROW_BLOCK = 128
NEG = -1e30
INT_MIN = -2 ** 31
VMEM_LIMIT = 56 * 1024 * 1024


def _cparams(*sem):
    return pltpu.CompilerParams(dimension_semantics=sem, vmem_limit_bytes=VMEM_LIMIT)


def _const_spec(shape):
    nd = len(shape)
    return pl.BlockSpec(shape, lambda *_: (0,) * nd, pipeline_mode=pl.Buffered(1))


def _rms(x, g):
    ms = jnp.mean(x * x, axis=-1, keepdims=True)
    return x * lax.rsqrt(ms + EPS) * g


def _rms_half(x, g):
    lane = lax.broadcasted_iota(jnp.int32, x.shape, 1)
    lo = lane < 64
    xx = x * x
    s_lo = jnp.sum(jnp.where(lo, xx, 0.0), axis=-1, keepdims=True)
    s_hi = jnp.sum(jnp.where(lo, 0.0, xx), axis=-1, keepdims=True)
    ms = jnp.where(lo, s_lo, s_hi) * (1.0 / 64.0)
    return x * lax.rsqrt(ms + EPS) * g


def _rope128(x, cos, sin_signed):
    return x * cos + pltpu.roll(x, 64, 1) * sin_signed


def _rope64(x, cos, sin_signed):
    lane = lax.broadcasted_iota(jnp.int32, x.shape, 1)
    first = (lane % 64) < 32
    rot = jnp.where(first, pltpu.roll(x, 96, 1), pltpu.roll(x, 32, 1))
    return x * cos + rot * sin_signed


AB_COLS = dict(qa=(0, 1024), ka=(1024, 1280), va=(1280, 1536), qi=(1536, 2560), cq=(2560, 3072),
               ckv=(3072, 3328), kikr=(3328, 3456), wi=(3456, 3584))
AB_IN_PAD = 3584


def _ab_proj_kernel(h_ref, gmix_ref, win_ref, wuq_ref, gaq_ref, gak_ref, gqlat_ref, gkvlat_ref, gqn_ref,
                    gqr_ref, gkr_ref, c128_ref, s128_ref, c64_ref, s64_ref,
                    q_ref, kf_ref, vf_ref, kb_ref, vb_ref, qi_ref, kikr_ref, kikrb_ref, wi_ref, qn_ref, qr_ref,
                    lat_ref):
    xn = _rms(h_ref[...], gmix_ref[...]).astype(BF16)
    c128, s128, c64, s64 = c128_ref[...], s128_ref[...], c64_ref[...], s64_ref[...]

    def proj(name):
        a, b = AB_COLS[name]
        return jnp.dot(xn, win_ref[:, a:b], preferred_element_type=F32)

    z = proj('qa')
    qscale = A_HEAD_DIM ** -0.5
    for h in range(A_HEADS):
        x = _rope128(_rms(z[:, h * 128:(h + 1) * 128], gaq_ref[...]), c128, s128)
        q_ref[:, h * 128:(h + 1) * 128] = (x * qscale).astype(BF16)
    z = proj('ka')
    for h in range(A_KV_HEADS):
        x = _rope128(_rms(z[:, h * 128:(h + 1) * 128], gak_ref[...]), c128, s128)
        kf_ref[:, h * 128:(h + 1) * 128] = x
        kb_ref[:, h * 128:(h + 1) * 128] = x.astype(BF16)
    z = proj('va')
    vf_ref[...] = z
    vb_ref[...] = z.astype(BF16)
    z = proj('qi')
    for p in range(IDX_HEADS // 2):
        x = _rope64(z[:, p * 128:(p + 1) * 128], c64, s64)
        qi_ref[:, p * 128:(p + 1) * 128] = (x * (IDX_DIM ** -0.5)).astype(BF16)
    z = proj('kikr')
    lane = lax.broadcasted_iota(jnp.int32, z.shape, 1)
    x = jnp.where(lane < 64, z, _rms_half(z, gkr_ref[...]))
    x = _rope64(x, c64, s64)
    kikr_ref[...] = x
    kikrb_ref[...] = x.astype(BF16)
    wi_ref[...] = proj('wi') * (IDX_HEADS ** -0.5)
    lat = _rms(proj('ckv'), gkvlat_ref[...])
    lat_ref[...] = lat
    cq = _rms(proj('cq'), gqlat_ref[...]).astype(BF16)
    bscale = (B_NOPE + B_ROPE) ** -0.5
    zq = jnp.dot(cq, wuq_ref[:, :B_HEADS * B_NOPE], preferred_element_type=F32)
    for h in range(B_HEADS):
        x = _rms(zq[:, h * 128:(h + 1) * 128], gqn_ref[...])
        qn_ref[:, h * 128:(h + 1) * 128] = (x * bscale).astype(BF16)
    zq = jnp.dot(cq, wuq_ref[:, B_HEADS * B_NOPE:], preferred_element_type=F32)
    for p in range(B_HEADS // 2):
        x = _rope64(_rms_half(zq[:, p * 128:(p + 1) * 128], gqr_ref[...]), c64, s64)
        qr_ref[:, p * 128:(p + 1) * 128] = (x * bscale).astype(BF16)


def _ab_proj(h, gmix, win, wuq, gaq, gak, gqlat, gkvlat, gqn, gqr2, gkr2, tabs, tm):
    r, d = h.shape
    row = lambda w: pl.BlockSpec((tm, w), lambda i: (i, 0))
    outs = [(1024, BF16), (256, F32), (256, F32), (256, BF16), (256, BF16), (1024, BF16), (128, F32),
            (128, BF16), (128, F32), (1024, BF16), (512, BF16), (256, F32)]
    return pl.pallas_call(
        _ab_proj_kernel,
        grid=(r // tm,),
        in_specs=[row(d), _const_spec(gmix.shape), _const_spec(win.shape), _const_spec(wuq.shape),
                  _const_spec(gaq.shape), _const_spec(gak.shape), _const_spec(gqlat.shape),
                  _const_spec(gkvlat.shape), _const_spec(gqn.shape), _const_spec(gqr2.shape),
                  _const_spec(gkr2.shape), row(128), row(128), row(128), row(128)],
        out_specs=[row(w) for w, _ in outs],
        out_shape=[jax.ShapeDtypeStruct((r, w), dt) for w, dt in outs],
        compiler_params=_cparams("parallel"),
        name="ab_proj",
    )(h, gmix, win, wuq, gaq, gak, gqlat, gkvlat, gqn, gqr2, gkr2, *tabs)


def _ukv_kernel(lat_ref, w_ref, gkn_ref, kn_ref, vb_ref):
    lat = lat_ref[...].astype(BF16)
    z = jnp.dot(lat, w_ref[:, :B_HEADS * B_NOPE], preferred_element_type=F32)
    for h in range(B_HEADS):
        kn_ref[:, h * 128:(h + 1) * 128] = _rms(z[:, h * 128:(h + 1) * 128], gkn_ref[...]).astype(BF16)
    vb_ref[...] = jnp.dot(lat, w_ref[:, B_HEADS * B_NOPE:], preferred_element_type=F32).astype(BF16)


def _ukv(lat, w, gkn, tm):
    n = lat.shape[0]
    row = lambda wd: pl.BlockSpec((tm, wd), lambda i: (i, 0))
    return pl.pallas_call(
        _ukv_kernel,
        grid=(n // tm,),
        in_specs=[row(lat.shape[1]), _const_spec(w.shape), _const_spec(gkn.shape)],
        out_specs=[row(1024), row(1024)],
        out_shape=[jax.ShapeDtypeStruct((n, 1024), BF16)] * 2,
        compiler_params=_cparams("parallel"),
        name="mla_ukv",
    )(lat, w, gkn)


def _dot_t(a, b):
    return lax.dot_general(a, b, (((1,), (1,)), ((), ())), preferred_element_type=F32)


def _float_key(x):
    bits = pltpu.bitcast(x, jnp.int32)
    return bits ^ ((bits >> 31) & jnp.int32(0x7FFFFFFF))


def _counts_ge16(vals, cands):
    rows = cands[0].shape[0]
    c16 = [jnp.broadcast_to(c, (rows, LANES)).astype(jnp.int16) for c in cands]
    one, zero = jnp.int16(1), jnp.int16(0)
    accs = [jnp.zeros((rows, LANES), jnp.int16) for _ in cands]
    rest = [jnp.zeros((rows, 1), F32) for _ in cands]
    for v in vals:
        n_full = v.shape[1] // LANES * LANES
        for s0 in range(0, n_full, LANES):
            blk = v[:, s0:s0 + LANES]
            accs = [a + jnp.where(blk >= c, one, zero) for a, c in zip(accs, c16)]
        if n_full < v.shape[1]:
            blk = v[:, n_full:]
            w = blk.shape[1]
            rest = [r + jnp.sum(jnp.where(blk >= c[:, :w], one, zero).astype(F32), axis=-1, keepdims=True)
                    for r, c in zip(rest, c16)]
    return [jnp.sum(a.astype(F32), axis=-1, keepdims=True) + r for a, r in zip(accs, rest)]


SEARCH_STEPS = 8
UNROLL_SEARCH_MAX_KEYS = 1152
WIDE_QUERY_MAX_KEYS = 768


def _search16(vals, k, side_jobs):
    def advance(t, step):
        cnts = _counts_ge16(vals, [t + j * step for j in (1, 2, 3)])
        digit = functools.reduce(jnp.add, [jnp.where(c >= float(k), 1, 0) for c in cnts])
        return t + digit * step

    t = jnp.full((vals[0].shape[0], 1), -32768, jnp.int32)
    if sum(v.shape[1] for v in vals) > UNROLL_SEARCH_MAX_KEYS:
        return lax.fori_loop(0, SEARCH_STEPS, lambda i, t: advance(t, jnp.left_shift(jnp.int32(1), 14 - 2 * i)), t)
    for i in range(SEARCH_STEPS):
        t = advance(t, 1 << (14 - 2 * i))
        if i % 2 == 1 and side_jobs:
            side_jobs.pop(0)()
    return t


def _kth_largest(keys, k, side_jobs):
    his = [(key >> 16).astype(jnp.int16) for key in keys]
    los = [((key & 0xFFFF) - 32768).astype(jnp.int16) for key in keys]
    h = _search16(his, k, side_jobs)
    top, bottom = jnp.int16(32767), jnp.int16(-32768)
    los2 = []
    for hi, lo in zip(his, los):
        h16 = jnp.broadcast_to(h, hi.shape).astype(jnp.int16)
        los2.append(jnp.where(hi == h16, lo, jnp.where(hi > h16, top, bottom)))
    l = _search16(los2, k, side_jobs)
    return (h << 16) | ((l + 32768) & 0xFFFF)


def _softmax_pv(ss, vs):
    m = functools.reduce(jnp.maximum, [jnp.max(s, axis=-1, keepdims=True) for s in ss])
    acc = functools.reduce(jnp.add, [
        jnp.dot(jnp.exp(s - m).astype(BF16), jnp.concatenate([v, jnp.ones_like(v)], axis=1),
                preferred_element_type=F32) for s, v in zip(ss, vs)])
    d = vs[0].shape[1]
    return acc[:, :d] / acc[:, d:]


def _kv_head(x, c):
    return x(c) if callable(x) else x[:, c * A_HEAD_DIM:(c + 1) * A_HEAD_DIM]


def _tie_break(keys, thr, topk, bias_refs):
    nq = keys[0].shape[0]
    n_gt = functools.reduce(jnp.add, [jnp.sum(jnp.where(key > thr, 1.0, 0.0), axis=-1, keepdims=True)
                                      for key in keys])
    need = float(topk) - n_gt
    r_i = lax.broadcasted_iota(jnp.int32, (LANES, LANES), 0)
    c_i = lax.broadcasted_iota(jnp.int32, (LANES, LANES), 1)
    before = jnp.where(r_i < c_i, 1.0, 0.0).astype(BF16)
    seen = jnp.zeros((nq, 1), F32)
    for key, ref in zip(keys, bias_refs):
        for s0 in range(0, key.shape[1], LANES):
            blk = key[:, s0:s0 + LANES]
            w = blk.shape[1]
            eq = jnp.where(blk == thr, 1.0, 0.0)
            rank = seen + jnp.dot(eq.astype(BF16), before[:w, :w], preferred_element_type=F32)
            sel = jnp.logical_or(blk > thr, jnp.logical_and(eq > 0.0, rank < need))
            ref[:, s0:s0 + w] = jnp.where(jnp.logical_and(sel, blk > INT_MIN), 0.0, NEG)
            seen = seen + jnp.sum(eq, axis=-1, keepdims=True)


def _dsa_core(qi, wi, q, cq, pieces, topk, bias_refs, side_jobs):
    nq = qi.shape[0]
    keys = []
    for ki, _, _, ck in pieces:
        score = jnp.zeros((nq, ki.shape[0]), F32)
        for h in range(IDX_HEADS):
            s_h = _dot_t(qi[:, h * 64:(h + 1) * 64], ki)
            score = score + jnp.maximum(s_h, 0.0) * wi[:, h:h + 1]
        keys.append(jnp.where(ck <= cq, _float_key(score), INT_MIN))
    thr = _kth_largest(keys, topk, side_jobs)
    n_ge = jnp.zeros((nq, 1), F32)
    for key, ref in zip(keys, bias_refs):
        ge = key >= thr
        ref[...] = jnp.where(jnp.logical_and(ge, key > INT_MIN), 0.0, NEG)
        n_ge = n_ge + jnp.sum(jnp.where(ge, 1.0, 0.0), axis=-1, keepdims=True)
    tied = jnp.max(jnp.where(jnp.logical_and(n_ge > float(topk), thr > INT_MIN), 1.0, 0.0))
    pl.when(tied > 0.0)(functools.partial(_tie_break, keys, thr, topk, bias_refs))
    group = A_HEADS // A_KV_HEADS
    biases = [jnp.concatenate([ref[...]] * group, axis=0) for ref in bias_refs]
    outs = []
    for c in range(A_KV_HEADS):
        qg = jnp.concatenate([q[:, (c * group + g) * 128:(c * group + g + 1) * 128] for g in range(group)], axis=0)
        ss = [_dot_t(qg, _kv_head(k, c)) + bias for (_, k, _, _), bias in zip(pieces, biases)]
        o = _softmax_pv(ss, [_kv_head(v, c) for _, _, v, _ in pieces])
        outs.extend([o[g * nq:(g + 1) * nq] for g in range(group)])
    return jnp.concatenate(outs, axis=-1)


def _mla_jobs(qn, qr, cq, pieces, store):
    biases = [jnp.where(ck <= cq, 0.0, NEG) for _, _, _, ck in pieces]

    def head(h):
        ss = []
        for (kn, kr, _, _), bias in zip(pieces, biases):
            s = _dot_t(qn[:, h * B_NOPE:(h + 1) * B_NOPE], kn(h))
            s = s + _dot_t(qr[:, h * B_ROPE:(h + 1) * B_ROPE], kr)
            ss.append(s + bias)
        store(h, _softmax_pv(ss, [vb(h) for _, _, vb, _ in pieces]))

    return [functools.partial(head, h) for h in range(B_HEADS)]


def _attn_prompt_kernel(qi_ref, wi_ref, q_ref, qn_ref, qr_ref, cq_ref, mkikr_ref, mk_ref, mv_ref, mkn_ref, mvb_ref,
                        ckm_ref, *rest, topk, s_len):
    nf = 6 if s_len else 0
    oa_ref, ob_ref = rest[nf + 2:nf + 4]
    bias_refs = list(rest[nf + 4:])
    cq = cq_ref[:, 0:1]
    valid = cq >= 0
    head_of = lambda ref, n: lambda h: ref[:n, h * B_V:(h + 1) * B_V]
    kv_of = lambda ref, n: lambda c: ref[:n, c * A_HEAD_DIM:(c + 1) * A_HEAD_DIM]
    mb = mkikr_ref.shape[0]
    mkikr = mkikr_ref[...]
    ckm = ckm_ref[...]
    pieces_a = [(mkikr[:, :64], kv_of(mk_ref, mb), kv_of(mv_ref, mb), ckm)]
    pieces_b = [(head_of(mkn_ref, mb), mkikr[:, 64:], head_of(mvb_ref, mb), ckm)]
    if s_len:
        fkikr_ref, fk_ref, fv_ref, fkn_ref, fvb_ref, ckf_ref = rest[:nf]
        fkikr = fkikr_ref[:s_len, :]
        ckf = ckf_ref[:, :s_len]
        pieces_a.append((fkikr[:, :64], kv_of(fk_ref, s_len), kv_of(fv_ref, s_len), ckf))
        pieces_b.append((head_of(fkn_ref, s_len), fkikr[:, 64:], head_of(fvb_ref, s_len), ckf))

    def store_b(h, o):
        ob_ref[:, h * B_V:(h + 1) * B_V] = jnp.where(valid, o, 0.0).astype(BF16)

    jobs = _mla_jobs(qn_ref, qr_ref, cq, pieces_b, store_b)
    oa = _dsa_core(qi_ref[...], wi_ref[...], q_ref[...], cq, pieces_a, topk, bias_refs, jobs)
    oa_ref[...] = jnp.where(valid, oa, 0.0).astype(BF16)
    for job in jobs:
        job()


def _attn_prompt(rows, ckm, ckf, bufs, nb, seq, meta_row0, mb, topk, n_ranges):
    qi, wi, q, qn, qr, cq, kikr, k, v, kn, vb = rows
    tq = ROW_BLOCK
    nj = seq // tq
    edges = sorted({-(-nj * r // n_ranges) for r in range(n_ranges + 1)})
    mspec = lambda w: pl.BlockSpec((mb, w), lambda b, j: (meta_row0 // mb, 0))
    fspec = lambda w: pl.BlockSpec((seq, w), lambda b, j: (b, 0))
    key_arrays = (kikr, k, v, kn, vb)
    key_widths = (128, 256, 256, 1024, 1024)
    for j0, j1 in [(None, None)] + list(zip(edges[:-1], edges[1:])):
        if j0 is None:
            grid, s_len = (1, 1), 0
            qspec = lambda w: pl.BlockSpec((tq, w), lambda b, j: (meta_row0 // tq, 0))
            frame_specs, frame_args = [], ()
            scratch = [pltpu.VMEM((tq, mb), F32)]
        else:
            s_len, span = j1 * tq, j1 - j0
            wide = s_len <= WIDE_QUERY_MAX_KEYS and j0 % span == 0 and nj % span == 0
            rq, steps = (tq * span, 1) if wide else (tq, span)
            grid = (nb, steps)
            qspec = lambda w, j0=j0, rq=rq: pl.BlockSpec((rq, w), lambda b, j: ((b * nj + j0) * tq // rq + j, 0))
            frame_specs = [fspec(w) for w in key_widths] + [_const_spec(ckf.shape)]
            frame_args = key_arrays + (ckf,)
            scratch = [pltpu.VMEM((rq, mb), F32), pltpu.VMEM((rq, s_len), F32)]
        n_in = 12 + len(frame_specs)
        bufs = pl.pallas_call(
            functools.partial(_attn_prompt_kernel, topk=topk, s_len=s_len),
            grid=grid,
            in_specs=[qspec(1024), qspec(128), qspec(1024), qspec(1024), qspec(512), qspec(128)]
                     + [mspec(w) for w in key_widths] + [_const_spec(ckm.shape)] + frame_specs
                     + [pl.BlockSpec(memory_space=pl.ANY)] * 2,
            out_specs=[qspec(1024), qspec(1024)],
            out_shape=[jax.ShapeDtypeStruct(bufs[0].shape, BF16)] * 2,
            scratch_shapes=scratch,
            input_output_aliases={n_in: 0, n_in + 1: 1},
            compiler_params=_cparams("parallel", "arbitrary"),
            name="attn_prompt",
        )(qi, wi, q, qn, qr, cq, *key_arrays, ckm, *frame_args, *bufs)
    return bufs


def _attn_sample_kernel(qi_ref, wi_ref, q_ref, qn_ref, qr_ref, cq_ref, kikr_ref, k_ref, v_ref, kn_ref, vb_ref,
                        pki_ref, pk_ref, pv_ref, pkn_ref, pkr_ref, pvb_ref, ckp_ref, ckn_ref, bufa_ref, bufb_ref,
                        oa_ref, ob_ref, biasp_ref, biasn_ref, *, topk):
    cq = cq_ref[:, 0:1]
    ckp, ckn = ckp_ref[...], ckn_ref[...]
    kikr = kikr_ref[...]
    past = pki_ref.shape[1]
    heads = lambda ref: lambda c: ref[pl.ds(c, past, stride=A_KV_HEADS), :].astype(BF16)
    kv_of = lambda ref: lambda c: ref[:, c * A_HEAD_DIM:(c + 1) * A_HEAD_DIM]
    past_a = (pki_ref[0].astype(BF16), heads(pk_ref), heads(pv_ref), ckp)
    new_a = (kikr[:, :64], kv_of(k_ref), kv_of(v_ref), ckn)
    head_of = lambda ref: lambda h: ref[:, h * B_V:(h + 1) * B_V]
    past_b = (head_of(pkn_ref), pkr_ref[0].astype(BF16), head_of(pvb_ref), ckp)
    new_b = (head_of(kn_ref), kikr[:, 64:], head_of(vb_ref), ckn)

    def store_b(h, o):
        ob_ref[:, h * B_V:(h + 1) * B_V] = o.astype(BF16)

    jobs = _mla_jobs(qn_ref, qr_ref, cq, [past_b, new_b], store_b)
    oa_ref[...] = _dsa_core(qi_ref[...], wi_ref[...], q_ref[...], cq, [past_a, new_a], topk,
                            [biasp_ref, biasn_ref], jobs).astype(BF16)
    for job in jobs:
        job()


def _attn_sample(rows, pki, pk, pv, pkn, pkr, pvb, ckp, ckn, bufs, row0, nb, ts, topk):
    qi, wi, q, qn, qr, cq, kikr, k, v, kn, vb = rows
    past = pk.shape[1]
    blk0 = row0 // ts
    nspec = lambda w: pl.BlockSpec((ts, w), lambda b: (blk0 + b, 0))
    pspec = lambda w: pl.BlockSpec((1, past, w), lambda b: (b, 0, 0))
    p2spec = lambda w: pl.BlockSpec((past, w), lambda b: (b, 0))
    kvspec = pl.BlockSpec((None, past * A_KV_HEADS, A_HEAD_DIM), lambda b: (b, 0, 0))
    pk = pk.reshape(nb, past * A_KV_HEADS, A_HEAD_DIM)
    pv = pv.reshape(nb, past * A_KV_HEADS, A_HEAD_DIM)
    return pl.pallas_call(
        functools.partial(_attn_sample_kernel, topk=topk),
        grid=(nb,),
        in_specs=[nspec(1024), nspec(128), nspec(1024), nspec(1024), nspec(512), nspec(128),
                  nspec(128), nspec(256), nspec(256), nspec(1024), nspec(1024),
                  pspec(64), kvspec, kvspec, p2spec(1024), pspec(64), p2spec(1024),
                  _const_spec(ckp.shape), _const_spec(ckn.shape)] + [pl.BlockSpec(memory_space=pl.ANY)] * 2,
        out_specs=[nspec(1024), nspec(1024)],
        out_shape=[jax.ShapeDtypeStruct(bufs[0].shape, BF16)] * 2,
        scratch_shapes=[pltpu.VMEM((ts, past), F32), pltpu.VMEM((ts, ts), F32)],
        input_output_aliases={19: 0, 20: 1},
        compiler_params=_cparams("parallel"),
        name="attn_sample",
    )(qi, wi, q, qn, qr, cq, kikr, k, v, kn, vb, pki, pk, pv, pkn, pkr, pvb, ckp, ckn, *bufs)


def _out_proj_kernel(h_ref, oa_ref, ob_ref, w_ref, o_ref):
    half = oa_ref.shape[1]
    y = jnp.dot(oa_ref[...], w_ref[:half, :], preferred_element_type=F32)
    y = y + jnp.dot(ob_ref[...], w_ref[half:, :], preferred_element_type=F32)
    o_ref[...] = h_ref[...] + y


def _out_proj(h, oa, ob, w, tm):
    r, d = h.shape
    row = lambda wd: pl.BlockSpec((tm, wd), lambda i: (i, 0))
    return pl.pallas_call(
        _out_proj_kernel,
        grid=(r // tm,),
        in_specs=[row(d), row(oa.shape[1]), row(ob.shape[1]), _const_spec(w.shape)],
        out_specs=row(d),
        out_shape=jax.ShapeDtypeStruct((r, d), F32),
        compiler_params=_cparams("parallel"),
        name="out_proj",
    )(h, oa, ob, w)


def _mlp_kernel(h_ref, g_ref, wup_ref, wdn_ref, o_ref, xn_ref):
    @pl.when(pl.program_id(1) == 0)
    def _():
        x = h_ref[...]
        xn_ref[...] = _rms(x, g_ref[...]).astype(BF16)
        o_ref[...] = x

    u = jnp.dot(xn_ref[...], wup_ref[...], preferred_element_type=F32)
    u = jnp.maximum(u, 0.0)
    o_ref[...] += jnp.dot((u * u).astype(BF16), wdn_ref[...], preferred_element_type=F32)


def _mlp(h, g, wup, wdn, layer, tm, tf):
    r, d = h.shape
    dff = wup.shape[2]
    return pl.pallas_call(
        _mlp_kernel,
        grid=(r // tm, dff // tf),
        in_specs=[pl.BlockSpec((tm, d), lambda i, f: (i, 0)), pl.BlockSpec((1, d), lambda i, f: (0, 0)),
                  pl.BlockSpec((None, d, tf), lambda i, f: (layer, 0, f)),
                  pl.BlockSpec((None, tf, d), lambda i, f: (layer, f, 0))],
        out_specs=pl.BlockSpec((tm, d), lambda i, f: (i, 0)),
        out_shape=jax.ShapeDtypeStruct((r, d), F32),
        scratch_shapes=[pltpu.VMEM((tm, d), BF16)],
        compiler_params=_cparams("parallel", "arbitrary"),
        name="mlp",
    )(h, g, wup, wdn)


def _cd_proj_kernel(h_ref, g_ref, w_ref, z_ref, xn_ref):
    @pl.when(pl.program_id(1) == 0)
    def _():
        xn_ref[...] = _rms(h_ref[...], g_ref[...]).astype(BF16)

    z_ref[...] = jnp.dot(xn_ref[...], w_ref[...], preferred_element_type=F32)


def _cd_proj(h, g, w, tm, tn):
    r, d = h.shape
    n = w.shape[1]
    return pl.pallas_call(
        _cd_proj_kernel,
        grid=(r // tm, n // tn),
        in_specs=[pl.BlockSpec((tm, d), lambda i, j: (i, 0)), pl.BlockSpec((1, d), lambda i, j: (0, 0)),
                  pl.BlockSpec((d, tn), lambda i, j: (0, j))],
        out_specs=pl.BlockSpec((tm, tn), lambda i, j: (i, j)),
        out_shape=jax.ShapeDtypeStruct((r, n), F32),
        scratch_shapes=[pltpu.VMEM((tm, d), BF16)],
        compiler_params=_cparams("parallel", "arbitrary"),
        name="cd_proj",
    )(h, g, w)


def _rglru_kernel(cx_ref, cg_ref, valid_ref, prev0_ref, h0_ref, cw_ref, cb_ref, gw_ref, gab_ref, gxb_ref, lam_ref,
                  buf_ref, o_ref, hl_ref, prev_s, h_s, a_s, b_s):
    @pl.when(pl.program_id(1) == 0)
    def _():
        prev_s[...] = prev0_ref[0]
        h_s[...] = jnp.broadcast_to(h0_ref[0], h_s.shape)

    cx = cx_ref[...]
    tm = cx.shape[0]
    full = jnp.concatenate([prev_s[...], cx], axis=0)
    prev_s[...] = cx[tm - 8:, :]
    xc = cb_ref[...] + cx * cw_ref[CONV_W - 1:CONV_W, :]
    for j in range(CONV_W - 1):
        sh = CONV_W - 1 - j
        xc = xc + full[8 - sh:8 - sh + tm, :] * cw_ref[j:j + 1, :]
    valid = valid_ref[:, 0:1] >= 0
    softplus = jnp.logaddexp(-lam_ref[...], 0.0)
    for n in range(C_BLOCKS):
        sl = slice(n * C_BLOCK, (n + 1) * C_BLOCK)
        xb = xc[:, sl]
        gates = jnp.dot(xb.astype(BF16), gw_ref[n], preferred_element_type=F32)
        r = jax.nn.sigmoid(gates[:, :C_BLOCK] + gab_ref[:, sl])
        ig = jax.nn.sigmoid(gates[:, C_BLOCK:] + gxb_ref[:, sl])
        log_a = -RG_C * r * softplus[:, sl]
        a = jnp.exp(log_a)
        a_s[:, sl] = a
        b = jnp.sqrt(1.0 - a * a) * (ig * xb)
        b_s[:, sl] = jnp.where(valid, b, 0.0)

    rid = lax.broadcasted_iota(jnp.int32, h_s.shape, 0)

    def group(g, h):
        rows = pl.ds(pl.multiple_of(g * 8, 8), 8)
        ca, cb = a_s[rows, :], b_s[rows, :]
        for s in (1, 2, 4):
            m = rid >= s
            cb = jnp.where(m, ca * pltpu.roll(cb, s, 0) + cb, cb)
            ca = jnp.where(m, ca * pltpu.roll(ca, s, 0), ca)
        hh = ca * h + cb
        b_s[rows, :] = hh
        return jnp.broadcast_to(hh[7:8, :], hh.shape)

    h = lax.fori_loop(0, tm // 8, group, h_s[...])
    h_s[...] = h
    hl_ref[0] = h[0:1, :]
    o_ref[...] = (b_s[...] * jax.nn.gelu(cg_ref[...])).astype(BF16)


def _rglru(z, valid, prev0, h0, cw, cb, gw, gab, gxb, lam, buf, row0, nb, t, tm):
    cwid = C_BLOCKS * C_BLOCK
    nj = t // tm
    b0 = row0 // tm
    zspec = lambda c: pl.BlockSpec((tm, cwid), lambda b, j: (b0 + b * nj + j, c))
    return pl.pallas_call(
        _rglru_kernel,
        grid=(nb, nj),
        in_specs=[zspec(0), zspec(1), pl.BlockSpec((tm, LANES), lambda b, j: (b0 + b * nj + j, 0)),
                  pl.BlockSpec((1, 8, cwid), lambda b, j: (b, 0, 0)), pl.BlockSpec((1, 1, cwid), lambda b, j: (b, 0, 0)),
                  _const_spec(cw.shape), _const_spec(cb.shape), _const_spec(gw.shape), _const_spec(gab.shape),
                  _const_spec(gxb.shape), _const_spec(lam.shape), pl.BlockSpec(memory_space=pl.ANY)],
        out_specs=[pl.BlockSpec((tm, cwid), lambda b, j: (b0 + b * nj + j, 0)),
                   pl.BlockSpec((1, 1, cwid), lambda b, j: (b, 0, 0))],
        out_shape=[jax.ShapeDtypeStruct(buf.shape, BF16), jax.ShapeDtypeStruct((nb, 1, cwid), F32)],
        input_output_aliases={11: 0},
        scratch_shapes=[pltpu.VMEM((8, cwid), F32), pltpu.VMEM((8, cwid), F32), pltpu.VMEM((tm, cwid), F32),
                        pltpu.VMEM((tm, cwid), F32)],
        compiler_params=_cparams("parallel", "arbitrary"),
        name="rglru",
    )(z, z, valid, prev0, h0, cw, cb, gw, gab, gxb, lam, buf)


def _retention_kernel(q_ref, k_ref, v_ref, g_ref, cos_ref, sin_ref, s0_ref, gdn_ref, lg_ref, buf_ref,
                      o_ref, s_ref, *, blk):
    @pl.when(pl.program_id(1) == 0)
    def _():
        s_ref[...] = s0_ref[...]

    cos, sin_signed = cos_ref[...], sin_ref[...]

    def rope(x):
        return x * cos + pltpu.roll(x, D_KDIM // 2, 1) * sin_signed

    ri = lax.broadcasted_iota(jnp.int32, (blk, blk), 0)
    ci = lax.broadcasted_iota(jnp.int32, (blk, blk), 1)
    diff = (ri - ci).astype(F32)
    rows = lax.broadcasted_iota(jnp.int32, (blk, 1), 0).astype(F32)
    for hd in range(D_HEADS):
        lg = lg_ref[hd]
        qsl = slice(hd * D_KDIM, (hd + 1) * D_KDIM)
        vsl = slice(hd * D_VDIM, (hd + 1) * D_VDIM)
        q = rope(q_ref[:, qsl])
        k = rope(k_ref[:, qsl]) * (D_KDIM ** -0.5)
        decay = jnp.where(diff >= 0, jnp.exp(lg * jnp.maximum(diff, 0.0)), 0.0)
        qb, kb, vb = q.astype(BF16), k.astype(BF16), v_ref[:, vsl].astype(BF16)
        att = _dot_t(qb, kb) * decay
        o = jnp.dot(att.astype(BF16), vb, preferred_element_type=F32)
        s_prev = s_ref[0, hd]
        o = o + jnp.dot(qb, s_prev.astype(BF16), preferred_element_type=F32) * jnp.exp(lg * (rows + 1.0))
        kdec = (k * jnp.exp(lg * (blk - 1.0 - rows))).astype(BF16)
        s_ref[0, hd] = jnp.exp(lg * blk) * s_prev + lax.dot_general(kdec, vb, (((0,), (0,)), ((), ())),
                                                                    preferred_element_type=F32)
        od = _rms(o, gdn_ref[...]) * jax.nn.silu(g_ref[:, vsl])
        o_ref[:, vsl] = od.astype(BF16)


def _retention(z, cos, sin_signed, s0, gdn, log_gamma, buf, col0, row0, nb, t, blk):
    nj = t // blk
    b0 = row0 // blk
    wid = D_HEADS * D_KDIM
    c0 = col0 // wid
    zspec = lambda c: pl.BlockSpec((blk, wid), lambda b, j: (b0 + b * nj + j, c0 + c))
    tspec = pl.BlockSpec((blk, D_KDIM), lambda b, j: (b0 + b * nj + j, 0))
    sspec = pl.BlockSpec((1, D_HEADS, D_KDIM, D_VDIM), lambda b, j: (b, 0, 0, 0))
    return pl.pallas_call(
        functools.partial(_retention_kernel, blk=blk),
        grid=(nb, nj),
        in_specs=[zspec(0), zspec(1), zspec(2), zspec(3), tspec, tspec, sspec, _const_spec(gdn.shape),
                  pl.BlockSpec(memory_space=pltpu.SMEM), pl.BlockSpec(memory_space=pl.ANY)],
        out_specs=[pl.BlockSpec((blk, wid), lambda b, j: (b0 + b * nj + j, 0)), sspec],
        out_shape=[jax.ShapeDtypeStruct(buf.shape, BF16), jax.ShapeDtypeStruct(s0.shape, F32)],
        input_output_aliases={9: 0},
        compiler_params=_cparams("parallel", "arbitrary"),
        name="retention",
    )(z, z, z, z, cos, sin_signed, s0, gdn, log_gamma, buf)


def _rope_tables(pos_groups, reps_groups, d, width):
    inv = ROPE_THETA ** (-jnp.arange(0, d, 2, dtype=F32) / d)
    cos_rows, sin_rows = [], []
    for pos, reps in zip(pos_groups, reps_groups):
        ang = pos.astype(F32)[:, None] * inv[None, :]
        cos, sin = lax.optimization_barrier((jnp.cos(ang), jnp.sin(ang)))
        cos_rows.append(jnp.tile(jnp.tile(jnp.concatenate([cos, cos], axis=-1), (1, width // d)), (reps, 1)))
        sin_rows.append(jnp.tile(jnp.tile(jnp.concatenate([-sin, sin], axis=-1), (1, width // d)), (reps, 1)))
    return jnp.concatenate(cos_rows, axis=0), jnp.concatenate(sin_rows, axis=0)


def kernel(x_prompt, x_sample, cache_a_k, cache_a_v, cache_a_kidx, cache_b_latent, cache_b_krope, state_c_conv,
           state_c_h, state_d_s, meta_tokens, norm_mix, norm_mlp, ab_w_in, ab_w_out, a_q_norm, a_k_norm,
           b_q_lat_norm, b_w_uq, b_kv_lat_norm, b_w_ukv, b_qn_norm, b_qr_norm, b_kn_norm, b_kr_norm, cd_w_in,
           cd_w_out, c_conv_w, c_conv_b, c_gate_a_w, c_gate_a_b, c_gate_x_w, c_gate_x_b, c_lambda, d_out_norm,
           mlp_w_up, mlp_w_down):
    nb_p, seq_p, d_model = x_prompt.shape
    nb_s, seq_s, _ = x_sample.shape
    past = cache_a_k.shape[2]
    n_meta = meta_tokens.shape[0]
    depth = norm_mix.shape[0]
    mb = -(-n_meta // ROW_BLOCK) * ROW_BLOCK
    pad = mb - n_meta
    rf = nb_p * seq_p
    rs = nb_s * seq_s
    row_m, row_s = rf, rf + mb
    r_tot = rf + mb + rs
    assert seq_p % ROW_BLOCK == 0 and ROW_BLOCK % CHUNK == 0 and seq_s % 8 == 0 and past % 8 == 0
    assert row_s % seq_s == 0 and r_tot % ROW_BLOCK == 0
    blocks = r_tot // ROW_BLOCK
    tile = ROW_BLOCK * max(d for d in range(1, 9) if blocks % d == 0)
    tile_tall = max(t for t in range(16, 1025, 16) if r_tot % t == 0)
    topk_p = min(TOPK_MAX, seq_p // 4)
    topk_s = min(TOPK_MAX, (past + seq_s) // 4)

    i32 = jnp.int32
    pos_f = n_meta + jnp.arange(seq_p, dtype=i32)
    pos_m = jnp.concatenate([jnp.zeros((pad,), i32), jnp.arange(n_meta, dtype=i32)])
    ck_f = 1 + jnp.arange(seq_p, dtype=i32) // CHUNK
    cq_m = jnp.concatenate([jnp.full((pad,), -1, i32), jnp.zeros((n_meta,), i32)])
    ck_m = jnp.concatenate([jnp.full((pad,), 2 ** 30, i32), jnp.zeros((n_meta,), i32)])
    pos_s_all = jnp.arange(past + seq_s, dtype=i32)
    ck_s = pos_s_all // CHUNK
    pos_s = pos_s_all[past:]
    cq_rows = jnp.concatenate([jnp.tile(ck_f, nb_p), cq_m, jnp.tile(ck_s[past:], nb_s)])
    cq_rows = jnp.broadcast_to(cq_rows[:, None], (r_tot, LANES))
    groups, reps = (pos_f, pos_m, pos_s), (nb_p, 1, nb_s)
    tabs128 = _rope_tables(groups, reps, 128, 128)
    tabs64 = _rope_tables(groups, reps, 64, 128)
    tabs256 = _rope_tables(groups, reps, 256, 256)

    h = jnp.concatenate([x_prompt.reshape(rf, d_model), jnp.zeros((pad, d_model), F32), meta_tokens.astype(F32),
                         x_sample.reshape(rs, d_model)], axis=0)
    mlp_up, mlp_down = mlp_w_up.astype(BF16), mlp_w_down.astype(BF16)

    def prompt_rows(x):
        meta = jnp.broadcast_to(x[row_m + pad:row_s][None], (nb_p, n_meta, x.shape[1]))
        return jnp.concatenate([meta, x[:rf].reshape(nb_p, seq_p, -1)], axis=1)

    def sample_rows(x):
        return x[row_s:].reshape(nb_s, seq_s, -1)

    ab_p, ab_s, cd_p, cd_s = [], [], [], []
    for layer in range(depth):
        i = layer // 2
        gmix = norm_mix[layer][None, :]
        if layer % 2 == 0:
            w = ab_w_in[i]
            offs = np.cumsum([0, 1024, 256, 256, 1024, 64, 16, 512, 256, 64])
            sec = lambda n: w[:, offs[n]:offs[n + 1]]
            win = jnp.concatenate([sec(0), sec(1), sec(2), sec(3), sec(6), sec(7), sec(4), sec(8), sec(5),
                                   jnp.zeros((d_model, AB_IN_PAD - 3472), F32)], axis=1).astype(BF16)
            wuq = b_w_uq[i].reshape(-1, B_HEADS, B_NOPE + B_ROPE)
            wuq = jnp.concatenate([wuq[:, :, :B_NOPE].reshape(-1, B_HEADS * B_NOPE),
                                   wuq[:, :, B_NOPE:].reshape(-1, B_HEADS * B_ROPE)], axis=1).astype(BF16)
            wukv = b_w_ukv[i].reshape(-1, B_HEADS, B_NOPE + B_V)
            wukv = jnp.concatenate([wukv[:, :, :B_NOPE].reshape(-1, B_HEADS * B_NOPE),
                                    wukv[:, :, B_NOPE:].reshape(-1, B_HEADS * B_V)], axis=1).astype(BF16)
            two = lambda g: jnp.concatenate([g, g])[None, :]
            gkr2 = jnp.concatenate([jnp.ones((64,), F32), b_kr_norm[i]])[None, :]
            (q, kf, vf, kb, vb_a, qi, kikr, kikrb, wi, qn, qr, lat) = _ab_proj(
                h, gmix, win, wuq, a_q_norm[i][None, :], a_k_norm[i][None, :], b_q_lat_norm[i][None, :],
                b_kv_lat_norm[i][None, :], b_qn_norm[i][None, :], two(b_qr_norm[i]), gkr2,
                tabs128 + tabs64, tile)
            gkn = b_kn_norm[i][None, :]
            kn, vbb = _ukv(lat, wukv, gkn, tile)
            pkn, pvb = _ukv(cache_b_latent[i].reshape(nb_s * past, -1), wukv, gkn, 512)
            rows = (qi, wi, q, qn, qr, cq_rows, kikrb, kb, vb_a, kn, vbb)
            bufs = (jnp.zeros((r_tot, 1024), BF16), jnp.zeros((r_tot, 1024), BF16))
            bufs = _attn_prompt(rows, ck_m[None, :], ck_f[None, :], bufs, nb_p, seq_p, row_m, mb, topk_p, 8)
            oa, ob = _attn_sample(rows, cache_a_kidx[i], cache_a_k[i], cache_a_v[i], pkn, cache_b_krope[i], pvb,
                                  ck_s[None, :past], ck_s[None, past:], bufs, row_s, nb_s, seq_s, topk_s)
            h = _out_proj(h, oa, ob, ab_w_out[i].astype(BF16), tile_tall)
            kv4 = lambda x: x.reshape(x.shape[0], x.shape[1], A_KV_HEADS, A_HEAD_DIM)
            ab_p.append((kv4(prompt_rows(kf)), kv4(prompt_rows(vf)), prompt_rows(kikr)[..., :64],
                         prompt_rows(lat), prompt_rows(kikr)[..., 64:]))
            ab_s.append((kv4(sample_rows(kf)), kv4(sample_rows(vf)), sample_rows(kikr)[..., :64],
                         sample_rows(lat), sample_rows(kikr)[..., 64:]))
        else:
            z = _cd_proj(h, gmix, cd_w_in[i].astype(BF16), tile_tall, 2048)
            cwid = C_BLOCKS * C_BLOCK
            nc = CONV_W - 1
            gw = jnp.concatenate([c_gate_a_w[i], c_gate_x_w[i]], axis=-1).astype(BF16)
            cargs = (c_conv_w[i], c_conv_b[i][None, :], gw, c_gate_a_b[i][None, :], c_gate_x_b[i][None, :],
                     c_lambda[i][None, :])
            oc = jnp.zeros((r_tot, cwid), BF16)
            oc, hl_m = _rglru(z, cq_rows, jnp.zeros((1, 8, cwid), F32), jnp.zeros((1, 1, cwid), F32), *cargs, oc,
                              row_m, 1, mb, ROW_BLOCK)
            prev_f = jnp.broadcast_to(z[row_s - 8:row_s, :cwid][None], (nb_p, 8, cwid))
            blk_f = 2 * ROW_BLOCK if seq_p % (2 * ROW_BLOCK) == 0 else ROW_BLOCK
            oc, hl_p = _rglru(z, cq_rows, prev_f, jnp.broadcast_to(hl_m, (nb_p, 1, cwid)), *cargs, oc,
                              0, nb_p, seq_p, blk_f)
            prev_s = jnp.concatenate([jnp.zeros((nb_s, 8 - nc, cwid), F32), state_c_conv[i]], axis=1)
            oc, hl_s = _rglru(z, cq_rows, prev_s, state_c_h[i][:, None, :], *cargs, oc, row_s, nb_s, seq_s, seq_s)
            hl_p, hl_s = hl_p[:, 0], hl_s[:, 0]
            log_gamma = jnp.log(1.0 - 2.0 ** (-5.0 - jnp.arange(D_HEADS, dtype=F32)))
            rargs = (d_out_norm[i][None, :], log_gamma)
            od = jnp.zeros((r_tot, D_HEADS * D_VDIM), BF16)
            od, ds_m = _retention(z, *tabs256, jnp.zeros((1, D_HEADS, D_KDIM, D_VDIM), F32), *rargs, od,
                                  2 * cwid, row_m, 1, mb, ROW_BLOCK)
            od, ds_p = _retention(z, *tabs256, jnp.broadcast_to(ds_m, (nb_p, D_HEADS, D_KDIM, D_VDIM)), *rargs, od,
                                  2 * cwid, 0, nb_p, seq_p, blk_f)
            od, ds_s = _retention(z, *tabs256, state_d_s[i], *rargs, od, 2 * cwid, row_s, nb_s, seq_s, seq_s)
            h = _out_proj(h, oc, od, cd_w_out[i].astype(BF16), tile_tall)

            def seq_tails(row0, nb, t):
                n = min(nc, t)
                rws = row0 + t - n + (jnp.arange(nb, dtype=i32) * t)[:, None] + jnp.arange(n, dtype=i32)
                return jnp.take(z, rws.reshape(-1), axis=0)[:, :cwid].reshape(nb, n, cwid)

            assert seq_p >= nc
            cd_p.append((seq_tails(0, nb_p, seq_p), hl_p, ds_p))
            cd_s.append((jnp.concatenate([state_c_conv[i], seq_tails(row_s, nb_s, seq_s)], axis=1)[:, -nc:],
                         hl_s, ds_s))
        h = _mlp(h, norm_mlp[layer][None, :], mlp_up, mlp_down, layer, tile, 1024)

    def stack(entries, j):
        return jnp.stack([e[j] for e in entries])

    y_p = h[:rf].reshape(nb_p, seq_p, d_model)
    y_s = h[row_s:].reshape(nb_s, seq_s, d_model)
    return (y_p, y_s,
            stack(ab_p, 0), stack(ab_p, 1), stack(ab_p, 2), stack(ab_p, 3), stack(ab_p, 4),
            stack(cd_p, 0), stack(cd_p, 1), stack(cd_p, 2),
            stack(ab_s, 0), stack(ab_s, 1), stack(ab_s, 2), stack(ab_s, 3), stack(ab_s, 4),
            stack(cd_s, 0), stack(cd_s, 1), stack(cd_s, 2))
```
